```python
import math, functools
import jax, jax.numpy as jnp
from jax import lax
import numpy as np

D_MODEL = 1024
BATCH = 16
SEQ = 2048
DEPTH = 2
DEC_BATCH = 128
DEC_SEQ = 1
PAST_LEN = 16384
PAGE_SIZE = 128

W_A = D_MODEL // 2
HEAD_SIZE_A = 64
H_A = W_A // HEAD_SIZE_A
DECAY_LORA = 64
A_LORA = 64
SHIFT_W = 3 * W_A + DECAY_LORA + A_LORA
GN_EPS = 64e-5
W_B = D_MODEL // 2
HEAD_DIM = 64
H_B = W_B // HEAD_DIM
KV_HEADS = 2
Q_PER_KV = H_B // KV_HEADS
KV_W = KV_HEADS * HEAD_DIM
WINDOW = 128
N_BUCKETS = 32
MAX_DISTANCE = 128
NORM_EPS = 1e-6
NEG_INF = -1e30
IN_SPLITS = (W_A, W_B, KV_W, KV_W, W_B, D_MODEL, D_MODEL)
IN_COLS = SHIFT_W + W_A + 2 * W_B + 2 * KV_W + 2 * D_MODEL

kernel_name = "hybrid_rwkv7_swa_gated_decoder_step"


def rms_norm(x, gain):
    xf = x.astype(jnp.float32)
    y = xf * lax.rsqrt(jnp.mean(xf * xf, axis=-1, keepdims=True) + NORM_EPS)
    return (y * gain.astype(jnp.float32)).astype(x.dtype)


def t5_bucket(dist):
    max_exact = N_BUCKETS // 2
    d = jnp.maximum(dist, 0)
    log_ratio = jnp.log(jnp.maximum(d, 1).astype(jnp.float32) / max_exact) / math.log(MAX_DISTANCE / max_exact)
    large = jnp.minimum(max_exact + (log_ratio * (N_BUCKETS - max_exact)).astype(jnp.int32), N_BUCKETS - 1)
    return jnp.where(d < max_exact, d, large)


def head_bias(dist, rel_bias):
    b = rel_bias[t5_bucket(dist)].astype(jnp.float32)
    return jnp.moveaxis(b, -1, 0).reshape(KV_HEADS, Q_PER_KV, *dist.shape)


def sink_attend(logits, sinks, v, eq):
    s = sinks.astype(jnp.float32).reshape(KV_HEADS, Q_PER_KV, 1, 1)
    m = jnp.maximum(jnp.max(logits, axis=-1, keepdims=True), s)
    p = jnp.exp(logits - m)
    probs = p / (jnp.sum(p, axis=-1, keepdims=True) + jnp.exp(s - m))
    return jnp.einsum(eq, probs.astype(v.dtype), v)


def swa_prompt(q, k, v, rel_bias, sinks):
    B, T = q.shape[:2]
    nb = T // WINDOW
    qb = q.reshape(B, nb, WINDOW, KV_HEADS, Q_PER_KV, HEAD_DIM)

    def band(t):
        tb = t.reshape(B, nb, WINDOW, KV_HEADS, HEAD_DIM)
        prev = jnp.concatenate([jnp.zeros_like(tb[:, :1]), tb[:, :-1]], axis=1)
        return jnp.concatenate([prev, tb], axis=2)

    kb, vb = band(k), band(v)
    logits = jnp.einsum('bnqgrd,bnkgd->bngrqk', qb, kb).astype(jnp.float32) * HEAD_DIM ** -0.5
    qi = jnp.arange(WINDOW)[:, None]
    kj = jnp.arange(2 * WINDOW)[None, :]
    dist = qi + WINDOW - kj
    band_ok = (dist >= 0) & (dist <= WINDOW)
    blk = jnp.arange(nb)[:, None, None]
    valid = band_ok[None] & ((blk > 0) | (kj >= WINDOW)[None])
    logits = jnp.where(valid[None, :, None, None], logits + head_bias(dist, rel_bias), NEG_INF)
    out = sink_attend(logits, sinks, vb, 'bngrqk,bnkgd->bnqgrd')
    return out.reshape(B, T, W_B), k[:, -WINDOW:], v[:, -WINDOW:]


def swa_sample(q, k, v, k_buf, v_buf, rel_bias, sinks):
    B, T = q.shape[:2]
    kc = jnp.concatenate([k_buf.astype(k.dtype), k], axis=1)
    vc = jnp.concatenate([v_buf.astype(v.dtype), v], axis=1)
    qg = q.reshape(B, T, KV_HEADS, Q_PER_KV, HEAD_DIM)
    logits = jnp.einsum('bqgrd,bkgd->bgrqk', qg, kc).astype(jnp.float32) * HEAD_DIM ** -0.5
    dist = jnp.arange(T)[:, None] + WINDOW - jnp.arange(WINDOW + T)[None, :]
    valid = (dist >= 0) & (dist <= WINDOW)
    logits = jnp.where(valid, logits + head_bias(dist, rel_bias), NEG_INF)
    out = sink_attend(logits, sinks, vc, 'bgrqk,bkgd->bqgrd')
    return out.reshape(B, T, W_B), kc[:, -WINDOW:], vc[:, -WINDOW:]


def rwkv_mix(u_sh, prev_row, s0, mu, w0, w_up, a0, a_up, k_k, k_a, r_k, lnx_g, lnx_b):
    B, T, _ = u_sh.shape
    f32 = jnp.float32
    u_prev = jnp.concatenate([prev_row[:, None, :].astype(u_sh.dtype), u_sh[:, :-1]], axis=1)
    xs = u_sh + (u_prev - u_sh) * mu
    r, k, v, wd, ad = jnp.split(xs, [W_A, 2 * W_A, 3 * W_A, 3 * W_A + DECAY_LORA], axis=-1)
    w_log = -jax.nn.softplus(-(w0 + jnp.tanh(wd) @ w_up).astype(f32)) - 0.5
    decay = jnp.exp(-jnp.exp(w_log))
    a = jax.nn.sigmoid((a0 + ad @ a_up).astype(f32))
    heads = lambda t: t.astype(f32).reshape(B, T, H_A, HEAD_SIZE_A)
    kk = heads(k * k_k)
    kk = kk / jnp.maximum(jnp.sqrt(jnp.sum(kk * kk, axis=-1, keepdims=True)), 1e-12)
    k_mod = k.astype(f32) * (1.0 + (a - 1.0) * k_a)
    r_h, k_h, v_h, w_h, a_h = heads(r), heads(k_mod), heads(v), heads(decay), heads(a)
    seq = lambda t: jnp.swapaxes(t, 0, 1)

    def step(S, inp):
        r_t, w_t, k_t, v_t, kk_t, a_t = inp
        sa = jnp.einsum('bhvk,bhk->bhv', S, -kk_t)
        S = (S * w_t[:, :, None, :] + sa[..., None] * (kk_t * a_t)[:, :, None, :]
             + v_t[..., None] * k_t[:, :, None, :])
        return S, jnp.einsum('bhvk,bhk->bhv', S, r_t)

    S_fin, o = lax.scan(step, s0.astype(f32), (seq(r_h), seq(w_h), seq(k_h), seq(v_h), seq(kk), seq(a_h)))
    o = seq(o)
    mean = jnp.mean(o, axis=-1, keepdims=True)
    var = jnp.mean((o - mean) ** 2, axis=-1, keepdims=True)
    o = ((o - mean) * lax.rsqrt(var + GN_EPS)).reshape(B, T, W_A) * lnx_g + lnx_b
    bonus = jnp.sum(r_h * k_h * r_k, axis=-1, keepdims=True) * v_h
    y = o + bonus.reshape(B, T, W_A)
    return y.astype(u_sh.dtype), S_fin, u_sh[:, -1]


def mixer_layer(x, c, shift0, wkv0, attn_fn, norm_g, w_ada, b_ada, w_in, mu_shift, w0, w_decay_up, a0,
                w_a_up, k_k, k_a, r_k, lnx_g, lnx_b, w_o_a, q_norm_g, k_norm_g, w_o_b, w_out):
    B, T, _ = x.shape
    mod = (jax.nn.silu(c) @ w_ada + b_ada)[:, None, :]
    shift, scale, gate = jnp.split(mod, 3, axis=-1)
    h = rms_norm(x, norm_g) * (1.0 + scale) + shift
    u = h @ w_in
    u_sh = u[..., :SHIFT_W]
    z_a, q, k, v, z_b, g_a, g_b = jnp.split(u[..., SHIFT_W:], np.cumsum(IN_SPLITS)[:-1].tolist(), axis=-1)
    o_a, wkv_new, shift_new = rwkv_mix(u_sh, shift0, wkv0, mu_shift, w0, w_decay_up, a0, w_a_up,
                                       k_k, k_a, r_k, lnx_g, lnx_b)
    y_a = o_a * jax.nn.silu(z_a)
    qh = rms_norm(q.reshape(B, T, H_B, HEAD_DIM), q_norm_g)
    kh = rms_norm(k.reshape(B, T, KV_HEADS, HEAD_DIM), k_norm_g)
    vh = v.reshape(B, T, KV_HEADS, HEAD_DIM)
    o_b, k_state, v_state = attn_fn(qh, kh, vh)
    y_b = o_b * jax.nn.silu(z_b)
    merged = jax.nn.sigmoid(g_a) * (y_a @ w_o_a) + jax.nn.sigmoid(g_b) * (y_b @ w_o_b)
    return x + gate * (merged @ w_out), wkv_new, shift_new, k_state, v_state


def setup_inputs(seed: int = 0) -> dict:
    key = jax.random.key(seed)
    ks = iter(jax.random.split(key, 40))
    n = lambda shape, s=1.0: s * jax.random.normal(next(ks), shape, jnp.float32)
    D = D_MODEL
    return {
        "x_prompt": n((BATCH, SEQ, D)),
        "x_sample": n((DEC_BATCH, DEC_SEQ, D)),
        "c_prompt": n((BATCH, D)),
        "c_sample": n((DEC_BATCH, D)),
        "state_wkv": n((DEPTH, DEC_BATCH, H_A, HEAD_SIZE_A, HEAD_SIZE_A), 0.3),
        "state_shift": n((DEPTH, DEC_BATCH, SHIFT_W)),
        "cache_k": n((DEPTH, DEC_BATCH, WINDOW, KV_HEADS, HEAD_DIM)),
        "cache_v": n((DEPTH, DEC_BATCH, WINDOW, KV_HEADS, HEAD_DIM)),
        "norm_g": 1.0 + n((DEPTH, D), 0.02),
        "w_ada": n((DEPTH, D, 3 * D), 0.5 * D ** -0.5),
        "b_ada": n((DEPTH, 3 * D), 0.02),
        "w_in": n((DEPTH, D, IN_COLS), D ** -0.5),
        "mu_shift": jax.random.uniform(next(ks), (DEPTH, SHIFT_W), jnp.float32),
        "w0": jax.random.uniform(next(ks), (DEPTH, W_A), jnp.float32, -6.0, 1.0),
        "w_decay_up": n((DEPTH, DECAY_LORA, W_A), 0.5 * DECAY_LORA ** -0.5),
        "a0": n((DEPTH, W_A), 0.1),
        "w_a_up": n((DEPTH, A_LORA, W_A), A_LORA ** -0.5),
        "k_k": 0.85 + n((DEPTH, W_A), 0.1),
        "k_a": 1.0 + n((DEPTH, W_A), 0.1),
        "r_k": n((DEPTH, H_A, HEAD_SIZE_A), 0.1),
        "lnx_g": 1.0 + n((DEPTH, W_A), 0.02),
        "lnx_b": n((DEPTH, W_A), 0.02),
        "w_o_a": n((DEPTH, W_A, D), W_A ** -0.5),
        "q_norm_g": 1.0 + n((DEPTH, HEAD_DIM), 0.02),
        "k_norm_g": 1.0 + n((DEPTH, HEAD_DIM), 0.02),
        "rel_bias": n((N_BUCKETS, H_B), 0.5),
        "sinks": n((DEPTH, H_B)),
        "w_o_b": n((DEPTH, W_B, D), W_B ** -0.5),
        "w_out": n((DEPTH, D, D), D ** -0.5),
    }


def reference(x_prompt, x_sample, c_prompt, c_sample, state_wkv, state_shift, cache_k, cache_v,
              norm_g, w_ada, b_ada, w_in, mu_shift, w0, w_decay_up, a0, w_a_up, k_k, k_a, r_k,
              lnx_g, lnx_b, w_o_a, q_norm_g, k_norm_g, rel_bias, sinks, w_o_b, w_out):
    xp, xs = x_prompt, x_sample
    wkv_p, shift_p, kw_p, vw_p, wkv_s, shift_s, kw_s, vw_s = [], [], [], [], [], [], [], []
    nb_p = x_prompt.shape[0]
    zero_shift = jnp.zeros((nb_p, SHIFT_W), x_prompt.dtype)
    zero_wkv = jnp.zeros((nb_p, H_A, HEAD_SIZE_A, HEAD_SIZE_A), jnp.float32)
    for l in range(DEPTH):
        lw = (norm_g[l], w_ada[l], b_ada[l], w_in[l], mu_shift[l], w0[l], w_decay_up[l], a0[l], w_a_up[l],
              k_k[l], k_a[l], r_k[l], lnx_g[l], lnx_b[l], w_o_a[l], q_norm_g[l], k_norm_g[l], w_o_b[l], w_out[l])
        attn_p = functools.partial(swa_prompt, rel_bias=rel_bias, sinks=sinks[l])
        xp, s_wkv, s_sh, s_k, s_v = mixer_layer(xp, c_prompt, zero_shift, zero_wkv, attn_p, *lw)
        wkv_p.append(s_wkv); shift_p.append(s_sh); kw_p.append(s_k); vw_p.append(s_v)
        attn_s = functools.partial(swa_sample, k_buf=cache_k[l], v_buf=cache_v[l], rel_bias=rel_bias, sinks=sinks[l])
        xs, s_wkv, s_sh, s_k, s_v = mixer_layer(xs, c_sample, state_shift[l], state_wkv[l], attn_s, *lw)
        wkv_s.append(s_wkv); shift_s.append(s_sh); kw_s.append(s_k); vw_s.append(s_v)
    y_prompt, y_sample = xp, xs
    return (y_prompt, y_sample, jnp.stack(wkv_p), jnp.stack(shift_p), jnp.stack(kw_p), jnp.stack(vw_p),
            jnp.stack(wkv_s), jnp.stack(shift_s), jnp.stack(kw_s), jnp.stack(vw_s))
```

```python
import functools
import math

import numpy as np
import jax
import jax.numpy as jnp
from jax import lax
from jax.experimental import pallas as pl
from jax.experimental.pallas import tpu as pltpu

F32 = jnp.float32
BF16 = jnp.bfloat16

D_MODEL = 1024
W_A = 512
HEAD_SIZE = 64
H_A = W_A // HEAD_SIZE
LORA = 64
SHIFT_W = 3 * W_A + 2 * LORA
W_B = 512
HEAD_DIM = 64
H_B = W_B // HEAD_DIM
KV_HEADS = 2
Q_PER_KV = H_B // KV_HEADS
KV_W = KV_HEADS * HEAD_DIM
WINDOW = 128
N_BUCKETS = 32
MAX_DISTANCE = 128
NORM_EPS = 1e-6
GN_EPS = 64e-5
NEG_INF = -1e30
MID_W = 3 * 512 + 2 * KV_W
GATE_W = 2 * D_MODEL
IN_COLS = SHIFT_W + MID_W + GATE_W
LANES = 128


def _dot(a, b):
    return jnp.dot(a, b, preferred_element_type=F32)


def _split_dot(x, g):
    hi = x.astype(BF16)
    lo = (x - hi.astype(F32)).astype(BF16)
    return _dot(hi, g) + _dot(lo, g)


def _silu(x):
    return x * jax.nn.sigmoid(x)


def _ada_kernel(c_ref, w_ref, b_ref, o_ref):
    c = c_ref[...]
    o_ref[...] = _dot(_silu(c).astype(BF16), w_ref[...]) + b_ref[...]


def _ada(c, w_bf, b):
    nb, d = c.shape
    n = w_bf.shape[1]
    tn = 1024
    return pl.pallas_call(
        _ada_kernel,
        grid=(n // tn,),
        in_specs=[pl.BlockSpec((nb, d), lambda j: (0, 0)),
                  pl.BlockSpec((d, tn), lambda j: (0, j)),
                  pl.BlockSpec((1, tn), lambda j: (0, j))],
        out_specs=pl.BlockSpec((nb, tn), lambda j: (0, j)),
        out_shape=jax.ShapeDtypeStruct((nb, n), F32),
        name="ada",
    )(c, w_bf, b.reshape(1, n))


def _in_proj_kernel(x_ref, g_ref, scale_ref, shift_ref, w_ref, ush_ref, mid_ref, gate_ref):
    x = x_ref[...]
    ms = jnp.mean(x * x, axis=-1, keepdims=True)
    h = (x * lax.rsqrt(ms + NORM_EPS)) * g_ref[...]
    h = h * (1.0 + scale_ref[...]) + shift_ref[...]
    hb = h.astype(BF16)
    ush_ref[...] = _dot(hb, w_ref[:, 0:SHIFT_W])
    mid_ref[...] = _dot(hb, w_ref[:, SHIFT_W:SHIFT_W + MID_W])
    gate_ref[...] = _dot(hb, w_ref[:, SHIFT_W + MID_W:IN_COLS])


def _mod_spec(per_row, tm, rows_per_seq):
    if per_row:
        return pl.BlockSpec((tm, D_MODEL), lambda i: (i, 0))
    return pl.BlockSpec((None, 1, D_MODEL), lambda i: ((i * tm) // rows_per_seq, 0, 0))


def _in_proj(x, norm_g, scale, shift, w_bf, tm, rows_per_seq, per_row):
    m = x.shape[0]
    return pl.pallas_call(
        _in_proj_kernel,
        grid=(m // tm,),
        in_specs=[pl.BlockSpec((tm, D_MODEL), lambda i: (i, 0)),
                  pl.BlockSpec((1, D_MODEL), lambda i: (0, 0)),
                  _mod_spec(per_row, tm, rows_per_seq),
                  _mod_spec(per_row, tm, rows_per_seq),
                  pl.BlockSpec((D_MODEL, IN_COLS), lambda i: (0, 0))],
        out_specs=[pl.BlockSpec((tm, SHIFT_W), lambda i: (i, 0)),
                   pl.BlockSpec((tm, MID_W), lambda i: (i, 0)),
                   pl.BlockSpec((tm, GATE_W), lambda i: (i, 0))],
        out_shape=[jax.ShapeDtypeStruct((m, SHIFT_W), F32),
                   jax.ShapeDtypeStruct((m, MID_W), F32),
                   jax.ShapeDtypeStruct((m, GATE_W), F32)],
        name="in_proj",
    )(x, norm_g.reshape(1, D_MODEL), scale, shift, w_bf)


def _prep_kernel(u_ref, prev_ref, shift0_ref, mu_ref, w0_ref, a0_ref, lora_ref, kk_ref, ka_ref, rk_ref, g_ref,
                 r_o, w_o, k_o, v_o, kk_o, b_o, bonus_o, *, tiles_per_seq, per_row):
    u = u_ref[...]
    if per_row:
        u_prev = shift0_ref[...]
    else:
        first = (pl.program_id(0) % tiles_per_seq) == 0
        prev_row = jnp.where(first, shift0_ref[...], prev_ref[7:8, :])
        row = lax.broadcasted_iota(jnp.int32, u.shape, 0)
        u_prev = jnp.where(row == 0, prev_row, pltpu.roll(u, 1, axis=0))
    xs = u + (u_prev - u) * mu_ref[...]
    r = xs[:, 0:W_A]
    k = xs[:, W_A:2 * W_A]
    v = xs[:, 2 * W_A:3 * W_A]
    tail = xs[:, 3 * W_A:SHIFT_W]
    lane = lax.broadcasted_iota(jnp.int32, tail.shape, 1)
    lora_in = jnp.where(lane < LORA, jnp.tanh(tail), tail).astype(BF16)
    up = _dot(lora_in, lora_ref[...])
    wl = w0_ref[...] + up[:, 0:W_A]
    neg = -wl
    softplus = jnp.maximum(neg, 0.0) + jnp.log(1.0 + jnp.exp(-jnp.abs(neg)))
    w_log = -softplus - 0.5
    decay = jnp.exp(-jnp.exp(w_log))
    a = jax.nn.sigmoid(a0_ref[...] + up[:, W_A:2 * W_A])
    kk = k * kk_ref[...]
    norm = jnp.sqrt(_split_dot(kk * kk, g_ref[...]))
    kk = kk / jnp.maximum(norm, 1e-12)
    k_mod = k * (1.0 + (a - 1.0) * ka_ref[...])
    r_o[...] = r
    w_o[...] = decay
    k_o[...] = k_mod
    v_o[...] = v
    kk_o[...] = kk
    b_o[...] = kk * a
    bonus_o[...] = _split_dot(r * k_mod * rk_ref[...], g_ref[...]) * v


def _prep(ush, shift0, mu, w0, a0, lora_bf, k_k, k_a, r_k, g512, tm, rows_per_seq, per_row):
    m = ush.shape[0]
    tiles_per_seq = max(rows_per_seq // tm, 1)
    row = lambda a: a.reshape(1, -1)
    vec = lambda n: pl.BlockSpec((1, n), lambda i: (0, 0))
    if per_row:
        prev_spec = pl.BlockSpec((8, SHIFT_W), lambda i: (0, 0))
        shift0_spec = pl.BlockSpec((tm, SHIFT_W), lambda i: (i, 0))
    else:
        prev_spec = pl.BlockSpec((8, SHIFT_W), lambda i: (jnp.maximum(i * (tm // 8) - 1, 0), 0))
        shift0_spec = pl.BlockSpec((None, 1, SHIFT_W), lambda i: (i // tiles_per_seq, 0, 0))
    out = jax.ShapeDtypeStruct((m, W_A), F32)
    return pl.pallas_call(
        functools.partial(_prep_kernel, tiles_per_seq=tiles_per_seq, per_row=per_row),
        grid=(m // tm,),
        in_specs=[pl.BlockSpec((tm, SHIFT_W), lambda i: (i, 0)), prev_spec, shift0_spec,
                  vec(SHIFT_W), vec(W_A), vec(W_A),
                  pl.BlockSpec((2 * LORA, 2 * W_A), lambda i: (0, 0)),
                  vec(W_A), vec(W_A), vec(W_A),
                  pl.BlockSpec((W_A, W_A), lambda i: (0, 0))],
        out_specs=[pl.BlockSpec((tm, W_A), lambda i: (i, 0))] * 7,
        out_shape=[out] * 7,
        name="rwkv_prep",
    )(ush, ush, shift0, row(mu), row(w0), row(a0), lora_bf, row(k_k), row(k_a), row(r_k), g512)


def _scan_kernel(*refs, tt, zero_init):
    if zero_init:
        r_ref, w_ref, k_ref, v_ref, kk_ref, b_ref, o_ref, pf_ref, p_ref = refs
    else:
        r_ref, w_ref, k_ref, v_ref, kk_ref, b_ref, p0_ref, o_ref, pf_ref, p_ref = refs
    ti = pl.program_id(1)

    @pl.when(ti == 0)
    def _():
        if zero_init:
            p_ref[...] = jnp.zeros(p_ref.shape, F32)
        else:
            p_ref[...] = p0_ref[...]

    def step(t, carry):
        sa = jnp.zeros((HEAD_SIZE, LANES), F32)
        for kc in range(HEAD_SIZE):
            sa = sa - p_ref[kc] * kk_ref[t, pl.ds(kc, 1), :]
        vt = v_ref[t]
        out = jnp.zeros((HEAD_SIZE, LANES), F32)
        for kc in range(HEAD_SIZE):
            p = (p_ref[kc] * w_ref[t, pl.ds(kc, 1), :] + sa * b_ref[t, pl.ds(kc, 1), :]
                 + vt * k_ref[t, pl.ds(kc, 1), :])
            p_ref[kc] = p
            out = out + p * r_ref[t, pl.ds(kc, 1), :]
        o_ref[t] = out
        return carry

    lax.fori_loop(0, tt, step, 0)

    @pl.when(ti == pl.num_programs(1) - 1)
    def _():
        pf_ref[...] = p_ref[...]


def _scan(seqs, p0, tt):
    t_len, _, n = seqs[0].shape
    seq_spec = pl.BlockSpec((tt, HEAD_SIZE, LANES), lambda j, i: (i, 0, j))
    state_spec = pl.BlockSpec((HEAD_SIZE, HEAD_SIZE, LANES), lambda j, i: (0, 0, j))
    zero_init = p0 is None
    args = list(seqs) + ([] if zero_init else [p0])
    return pl.pallas_call(
        functools.partial(_scan_kernel, tt=tt, zero_init=zero_init),
        grid=(n // LANES, t_len // tt),
        in_specs=[seq_spec] * 6 + ([] if zero_init else [state_spec]),
        out_specs=[seq_spec, state_spec],
        out_shape=[jax.ShapeDtypeStruct((t_len, HEAD_SIZE, n), F32),
                   jax.ShapeDtypeStruct((HEAD_SIZE, HEAD_SIZE, n), F32)],
        scratch_shapes=[pltpu.VMEM((HEAD_SIZE, HEAD_SIZE, LANES), F32)],
        compiler_params=pltpu.CompilerParams(dimension_semantics=("arbitrary", "arbitrary")),
        name="rwkv_scan",
    )(*args)


def _attn_kernel(q_ref, kc_ref, kp_ref, vc_ref, vp_ref, zb_ref, qg_ref, kg_ref, bias_ref, sink_ref, g512_ref, g128_ref,
                 yb_ref, kwin_ref, vwin_ref):
    n = pl.program_id(1)
    q = q_ref[...]
    qn = q * lax.rsqrt(_split_dot(q * q, g512_ref[...]) * (1.0 / HEAD_DIM) + NORM_EPS) * qg_ref[...]

    def knorm(kx):
        return kx * lax.rsqrt(_split_dot(kx * kx, g128_ref[...]) * (1.0 / HEAD_DIM) + NORM_EPS) * kg_ref[...]

    kn_c = knorm(kc_ref[...])
    kn_p = knorm(kp_ref[...])
    keys = jnp.concatenate([kn_p, kn_c], axis=0).astype(BF16)
    vals = jnp.concatenate([vp_ref[...], vc_ref[...]], axis=0).astype(BF16)
    rows = Q_PER_KV * WINDOW
    qi = lax.broadcasted_iota(jnp.int32, (rows, 2 * WINDOW), 0) % WINDOW
    kj = lax.broadcasted_iota(jnp.int32, (rows, 2 * WINDOW), 1)
    dist = qi + WINDOW - kj
    valid = (dist >= 0) & (dist <= WINDOW) & ((n > 0) | (kj >= WINDOW))
    qb = qn.astype(BF16)
    pieces = []
    for g in range(KV_HEADS):
        q4 = jnp.concatenate([qb[:, (g * Q_PER_KV + r) * HEAD_DIM:(g * Q_PER_KV + r + 1) * HEAD_DIM]
                              for r in range(Q_PER_KV)], axis=0)
        kg = keys[:, g * HEAD_DIM:(g + 1) * HEAD_DIM]
        vg = vals[:, g * HEAD_DIM:(g + 1) * HEAD_DIM]
        logits = lax.dot_general(q4, kg, (((1,), (1,)), ((), ())), preferred_element_type=F32)
        logits = jnp.where(valid, logits * (HEAD_DIM ** -0.5) + bias_ref[g], NEG_INF)
        s = sink_ref[g]
        m = jnp.maximum(jnp.max(logits, axis=-1, keepdims=True), s)
        p = jnp.exp(logits - m)
        den = jnp.sum(p, axis=-1, keepdims=True) + jnp.exp(s - m)
        o = _dot(p.astype(BF16), vg) / den
        pieces += [o[r * WINDOW:(r + 1) * WINDOW] for r in range(Q_PER_KV)]
    o_b = jnp.concatenate(pieces, axis=-1)
    yb_ref[...] = o_b * _silu(zb_ref[...])

    @pl.when(n == pl.num_programs(1) - 1)
    def _():
        kwin_ref[...] = kn_c
        vwin_ref[...] = vc_ref[...]


def _attn(mid, nb_seq, q_gain, k_gain, bias_tab, sink_col, g512, g128):
    m = mid.shape[0]
    nblk = m // (nb_seq * WINDOW)
    blk = lambda b, n: b * nblk + n
    prev = lambda b, n: jnp.maximum(b * nblk + n - 1, 0)
    kcol, vcol = (3 * 512) // KV_W, (3 * 512) // KV_W + 1
    rows = Q_PER_KV * WINDOW
    return pl.pallas_call(
        _attn_kernel,
        grid=(nb_seq, nblk),
        in_specs=[pl.BlockSpec((WINDOW, W_B), lambda b, n: (blk(b, n), 0)),
                  pl.BlockSpec((WINDOW, KV_W), lambda b, n: (blk(b, n), kcol)),
                  pl.BlockSpec((WINDOW, KV_W), lambda b, n: (prev(b, n), kcol)),
                  pl.BlockSpec((WINDOW, KV_W), lambda b, n: (blk(b, n), vcol)),
                  pl.BlockSpec((WINDOW, KV_W), lambda b, n: (prev(b, n), vcol)),
                  pl.BlockSpec((WINDOW, W_B), lambda b, n: (blk(b, n), 2)),
                  pl.BlockSpec((1, W_B), lambda b, n: (0, 0)),
                  pl.BlockSpec((1, KV_W), lambda b, n: (0, 0)),
                  pl.BlockSpec((KV_HEADS, rows, 2 * WINDOW), lambda b, n: (0, 0, 0)),
                  pl.BlockSpec((KV_HEADS, rows, 1), lambda b, n: (0, 0, 0)),
                  pl.BlockSpec((W_B, W_B), lambda b, n: (0, 0)),
                  pl.BlockSpec((KV_W, KV_W), lambda b, n: (0, 0))],
        out_specs=[pl.BlockSpec((WINDOW, W_B), lambda b, n: (blk(b, n), 0)),
                   pl.BlockSpec((None, WINDOW, KV_W), lambda b, n: (b, 0, 0)),
                   pl.BlockSpec((None, WINDOW, KV_W), lambda b, n: (b, 0, 0))],
        out_shape=[jax.ShapeDtypeStruct((m, W_B), F32),
                   jax.ShapeDtypeStruct((nb_seq, WINDOW, KV_W), F32),
                   jax.ShapeDtypeStruct((nb_seq, WINDOW, KV_W), F32)],
        compiler_params=pltpu.CompilerParams(dimension_semantics=("arbitrary", "arbitrary")),
        name="swa_prompt",
    )(mid, mid, mid, mid, mid, mid, q_gain, k_gain, bias_tab, sink_col, g512, g128)


def _attn_step_kernel(q_ref, kn_ref, vn_ref, zb_ref, ck_ref, cv_ref, qg_ref, kg_ref, bias_ref, bias0_ref, sink_ref,
                      g512_ref, g128_ref, yb_ref, ko_ref, vo_ref):
    q = q_ref[...]
    qn = q * lax.rsqrt(_split_dot(q * q, g512_ref[...]) * (1.0 / HEAD_DIM) + NORM_EPS) * qg_ref[...]
    kx = kn_ref[...]
    kn = kx * lax.rsqrt(_split_dot(kx * kx, g128_ref[...]) * (1.0 / HEAD_DIM) + NORM_EPS) * kg_ref[...]
    vn = vn_ref[...]
    ck = ck_ref[...]
    cv = cv_ref[...]
    bt = q.shape[0]
    scale = HEAD_DIM ** -0.5
    lo, hi = [], []
    for r in range(Q_PER_KV):
        qsel = jnp.concatenate([qn[:, r * HEAD_DIM:(r + 1) * HEAD_DIM],
                                qn[:, (Q_PER_KV + r) * HEAD_DIM:(Q_PER_KV + r + 1) * HEAD_DIM]], axis=-1)
        prod = (ck * qsel[:, None, :]).reshape(bt * WINDOW, KV_W)
        lg = _split_dot(prod, g128_ref[...]).reshape(bt, WINDOW, KV_W) * scale + bias_ref[r]
        lgn = _split_dot(kn * qsel, g128_ref[...]) * scale + bias0_ref[r]
        s = sink_ref[r]
        m = jnp.maximum(jnp.maximum(jnp.max(lg, axis=1), lgn), s)
        p = jnp.exp(lg - m[:, None, :])
        pn = jnp.exp(lgn - m)
        den = jnp.sum(p, axis=1) + pn + jnp.exp(s - m)
        o = (jnp.sum(p * cv, axis=1) + pn * vn) / den
        lo.append(o[:, 0:HEAD_DIM])
        hi.append(o[:, HEAD_DIM:KV_W])
    o_b = jnp.concatenate(lo + hi, axis=-1)
    yb_ref[...] = o_b * _silu(zb_ref[...])
    j = lax.broadcasted_iota(jnp.int32, ck.shape, 1)
    ko_ref[...] = jnp.where(j == WINDOW - 1, kn[:, None, :], pltpu.roll(ck, WINDOW - 1, axis=1))
    vo_ref[...] = jnp.where(j == WINDOW - 1, vn[:, None, :], pltpu.roll(cv, WINDOW - 1, axis=1))


def _attn_step(mid, cache_k, cache_v, q_gain, k_gain, bias_rows, bias0, sink_rows, g512, g128, bt):
    nb = mid.shape[0]
    kcol, vcol = (3 * 512) // KV_W, (3 * 512) // KV_W + 1
    cache_spec = pl.BlockSpec((bt, WINDOW, KV_W), lambda i: (i, 0, 0))
    return pl.pallas_call(
        _attn_step_kernel,
        grid=(nb // bt,),
        in_specs=[pl.BlockSpec((bt, W_B), lambda i: (i, 0)),
                  pl.BlockSpec((bt, KV_W), lambda i: (i, kcol)),
                  pl.BlockSpec((bt, KV_W), lambda i: (i, vcol)),
                  pl.BlockSpec((bt, W_B), lambda i: (i, 2)),
                  cache_spec, cache_spec,
                  pl.BlockSpec((1, W_B), lambda i: (0, 0)),
                  pl.BlockSpec((1, KV_W), lambda i: (0, 0)),
                  pl.BlockSpec((Q_PER_KV, WINDOW, KV_W), lambda i: (0, 0, 0)),
                  pl.BlockSpec((Q_PER_KV, 1, KV_W), lambda i: (0, 0, 0)),
                  pl.BlockSpec((Q_PER_KV, 1, KV_W), lambda i: (0, 0, 0)),
                  pl.BlockSpec((W_B, W_B), lambda i: (0, 0)),
                  pl.BlockSpec((KV_W, KV_W), lambda i: (0, 0))],
        out_specs=[pl.BlockSpec((bt, W_B), lambda i: (i, 0)), cache_spec, cache_spec],
        out_shape=[jax.ShapeDtypeStruct((nb, W_B), F32),
                   jax.ShapeDtypeStruct(cache_k.shape, F32),
                   jax.ShapeDtypeStruct(cache_v.shape, F32)],
        name="swa_step",
    )(mid, mid, mid, mid, cache_k, cache_v, q_gain, k_gain, bias_rows, bias0, sink_rows, g512, g128)


def _merge_kernel(o_ref, bonus_ref, za_ref, yb_ref, ga_ref, gb_ref, x_ref, gate_ref, lng_ref, lnb_ref, g512_ref,
                  woa_ref, wob_ref, wout_ref, out_ref):
    o = o_ref[...]
    mean = _split_dot(o, g512_ref[...]) * (1.0 / HEAD_SIZE)
    c = o - mean
    var = _split_dot(c * c, g512_ref[...]) * (1.0 / HEAD_SIZE)
    on = c * lax.rsqrt(var + GN_EPS) * lng_ref[...] + lnb_ref[...]
    ya = (on + bonus_ref[...]) * _silu(za_ref[...])
    pa = _dot(ya.astype(BF16), woa_ref[...])
    pb = _dot(yb_ref[...].astype(BF16), wob_ref[...])
    merged = jax.nn.sigmoid(ga_ref[...]) * pa + jax.nn.sigmoid(gb_ref[...]) * pb
    out_ref[...] = x_ref[...] + gate_ref[...] * _dot(merged.astype(BF16), wout_ref[...])


def _merge(o, bonus, mid, yb, gates, x, gate, lnx_g, lnx_b, g512, woa_bf, wob_bf, wout_bf, tm, rows_per_seq, per_row):
    m = x.shape[0]
    half = lambda c: pl.BlockSpec((tm, W_A), lambda i: (i, c))
    full = lambda c: pl.BlockSpec((tm, D_MODEL), lambda i: (i, c))
    vec = pl.BlockSpec((1, W_A), lambda i: (0, 0))
    return pl.pallas_call(
        _merge_kernel,
        grid=(m // tm,),
        in_specs=[half(0), half(0), half(1), half(0), full(0), full(1), full(0),
                  _mod_spec(per_row, tm, rows_per_seq), vec, vec,
                  pl.BlockSpec((W_A, W_A), lambda i: (0, 0)),
                  pl.BlockSpec((W_A, D_MODEL), lambda i: (0, 0)),
                  pl.BlockSpec((W_B, D_MODEL), lambda i: (0, 0)),
                  pl.BlockSpec((D_MODEL, D_MODEL), lambda i: (0, 0))],
        out_specs=full(0),
        out_shape=jax.ShapeDtypeStruct((m, D_MODEL), F32),
        name="merge_out",
    )(o, bonus, mid, yb, gates, gates, x, gate, lnx_g.reshape(1, W_A), lnx_b.reshape(1, W_A), g512,
      woa_bf, wob_bf, wout_bf)


def _t5_bucket(dist):
    max_exact = N_BUCKETS // 2
    d = jnp.maximum(dist, 0)
    log_ratio = jnp.log(jnp.maximum(d, 1).astype(F32) / max_exact) / math.log(MAX_DISTANCE / max_exact)
    large = jnp.minimum(max_exact + (log_ratio * (N_BUCKETS - max_exact)).astype(jnp.int32), N_BUCKETS - 1)
    return jnp.where(d < max_exact, d, large)


def _block_ones(n, blk):
    i = np.arange(n) // blk
    return jnp.asarray((i[:, None] == i[None, :]).astype(np.float32), dtype=BF16)


def _to_lanes(a, nb, t_len, n_pad):
    a = a.reshape(nb, t_len, H_A, HEAD_SIZE).transpose(1, 3, 0, 2).reshape(t_len, HEAD_SIZE, nb * H_A)
    if n_pad != nb * H_A:
        a = jnp.pad(a, ((0, 0), (0, 0), (0, n_pad - nb * H_A)))
    return a


def _pick(cands, n):
    for c in cands:
        if n % c == 0:
            return c
    return n


def _rwkv(ush, shift0, state0, nb, t_len, lw, g512, per_row):
    m = nb * t_len
    tm = _pick((256, 128, 64, 32, 16, 8), m if per_row else t_len)
    outs = _prep(ush, shift0, lw["mu"], lw["w0"], lw["a0"], lw["lora"], lw["k_k"], lw["k_a"], lw["r_k"], g512,
                 tm, t_len, per_row)
    bonus = outs[6]
    n = nb * H_A
    n_pad = -(-n // LANES) * LANES
    seqs = [_to_lanes(a, nb, t_len, n_pad) for a in outs[:6]]
    if state0 is None:
        p0 = None
    else:
        p0 = state0.transpose(3, 2, 0, 1).reshape(HEAD_SIZE, HEAD_SIZE, n)
        if n_pad != n:
            p0 = jnp.pad(p0, ((0, 0), (0, 0), (0, n_pad - n)))
    o_t, pf = _scan(seqs, p0, _pick((16, 8, 4, 2, 1), t_len))
    o = o_t[:, :, :n].reshape(t_len, HEAD_SIZE, nb, H_A).transpose(2, 0, 3, 1).reshape(m, W_A)
    wkv = pf[:, :, :n].reshape(HEAD_SIZE, HEAD_SIZE, nb, H_A).transpose(2, 3, 1, 0)
    return o, bonus, wkv


def kernel(x_prompt, x_sample, c_prompt, c_sample, state_wkv, state_shift, cache_k, cache_v, norm_g, w_ada, b_ada, w_in, mu_shift, w0, w_decay_up, a0, w_a_up, k_k, k_a, r_k, lnx_g, lnx_b, w_o_a, q_norm_g, k_norm_g, rel_bias, sinks, w_o_b, w_out):
    nbp, t_len, _ = x_prompt.shape
    nbs = x_sample.shape[0]
    depth = norm_g.shape[0]
    mp = nbp * t_len
    g512 = _block_ones(W_A, HEAD_SIZE)
    g128 = _block_ones(KV_W, HEAD_DIM)

    qi = jnp.arange(WINDOW)[:, None]
    kj = jnp.arange(2 * WINDOW)[None, :]
    bias_p = rel_bias[_t5_bucket(qi + WINDOW - kj)].astype(F32)
    bias_p = jnp.moveaxis(bias_p, -1, 0).reshape(KV_HEADS, Q_PER_KV * WINDOW, 2 * WINDOW)
    bias_s = rel_bias[_t5_bucket(WINDOW - jnp.arange(WINDOW + 1))].astype(F32)
    pair = lambda a: jnp.concatenate([jnp.repeat(a[..., 0:Q_PER_KV, None], HEAD_DIM, axis=-1),
                                      jnp.repeat(a[..., Q_PER_KV:H_B, None], HEAD_DIM, axis=-1)], axis=-1)
    bias_rows = pair(bias_s[:WINDOW]).transpose(1, 0, 2)
    bias0 = pair(bias_s[WINDOW:])
    bias0 = bias0.transpose(1, 0, 2)

    c_all = jnp.concatenate([c_prompt, c_sample], axis=0)
    xp = x_prompt.reshape(mp, D_MODEL)
    xs = x_sample.reshape(nbs, D_MODEL)
    zero_shift = jnp.zeros((nbp, 1, SHIFT_W), F32)
    tm_p = _pick((256, 128), t_len)
    tm_s = _pick((128, 64, 32, 16, 8), nbs)
    outs = {k: [] for k in ("wkv_p", "shift_p", "kw_p", "vw_p", "wkv_s", "shift_s", "kw_s", "vw_s")}
    for l in range(depth):
        wl = w_in[l]
        base = SHIFT_W
        z_a, q, kb, vb, z_b, gts = (wl[:, base:base + 512], wl[:, base + 512:base + 1024],
                                    wl[:, base + 1024:base + 1152], wl[:, base + 1152:base + 1280],
                                    wl[:, base + 1280:base + 1792], wl[:, base + 1792:])
        w_in_bf = jnp.concatenate([wl[:, :SHIFT_W], q, z_a, z_b, kb, vb, gts], axis=1).astype(BF16)
        zeros = jnp.zeros((LORA, W_A), F32)
        lora = jnp.concatenate([jnp.concatenate([w_decay_up[l], zeros], axis=1),
                                jnp.concatenate([zeros, w_a_up[l]], axis=1)], axis=0).astype(BF16)
        lw = dict(mu=mu_shift[l], w0=w0[l], a0=a0[l], lora=lora, k_k=k_k[l], k_a=k_a[l], r_k=r_k[l].reshape(-1))
        woa_bf, wob_bf, wout_bf = w_o_a[l].astype(BF16), w_o_b[l].astype(BF16), w_out[l].astype(BF16)
        q_gain = jnp.tile(q_norm_g[l], H_B).reshape(1, W_B)
        k_gain = jnp.tile(k_norm_g[l], KV_HEADS).reshape(1, KV_W)
        sink_col = jnp.repeat(sinks[l].reshape(KV_HEADS, Q_PER_KV), WINDOW, axis=1).reshape(
            KV_HEADS, Q_PER_KV * WINDOW, 1)
        sink_rows = pair(sinks[l].reshape(1, H_B)).transpose(1, 0, 2)

        mod = _ada(c_all, w_ada[l].astype(BF16), b_ada[l])
        shift, scale, gate = mod[:, :D_MODEL], mod[:, D_MODEL:2 * D_MODEL], mod[:, 2 * D_MODEL:]

        sp, scp, gp = (a[:nbp].reshape(nbp, 1, D_MODEL) for a in (shift, scale, gate))
        ush, mid, gts_p = _in_proj(xp, norm_g[l], scp, sp, w_in_bf, tm_p, t_len, False)
        o, bonus, wkv = _rwkv(ush, zero_shift, None, nbp, t_len, lw, g512, False)
        yb, kwin, vwin = _attn(mid, nbp, q_gain, k_gain, bias_p, sink_col, g512, g128)
        xp = _merge(o, bonus, mid, yb, gts_p, xp, gp, lnx_g[l], lnx_b[l], g512, woa_bf, wob_bf, wout_bf,
                    tm_p, t_len, False)
        outs["wkv_p"].append(wkv)
        outs["shift_p"].append(ush.reshape(nbp, t_len, SHIFT_W)[:, -1])
        outs["kw_p"].append(kwin.reshape(nbp, WINDOW, KV_HEADS, HEAD_DIM))
        outs["vw_p"].append(vwin.reshape(nbp, WINDOW, KV_HEADS, HEAD_DIM))

        ss, scs, gs = shift[nbp:], scale[nbp:], gate[nbp:]
        ush, mid, gts_s = _in_proj(xs, norm_g[l], scs, ss, w_in_bf, tm_s, 1, True)
        o, bonus, wkv = _rwkv(ush, state_shift[l], state_wkv[l], nbs, 1, lw, g512, True)
        yb, kwin, vwin = _attn_step(mid, cache_k[l].reshape(nbs, WINDOW, KV_W), cache_v[l].reshape(nbs, WINDOW, KV_W),
                                    q_gain, k_gain, bias_rows, bias0, sink_rows, g512, g128,
                                    _pick((8,), nbs))
        xs = _merge(o, bonus, mid, yb, gts_s, xs, gs, lnx_g[l], lnx_b[l], g512, woa_bf, wob_bf, wout_bf,
                    tm_s, 1, True)
        outs["wkv_s"].append(wkv)
        outs["shift_s"].append(ush)
        outs["kw_s"].append(kwin.reshape(nbs, WINDOW, KV_HEADS, HEAD_DIM))
        outs["vw_s"].append(vwin.reshape(nbs, WINDOW, KV_HEADS, HEAD_DIM))

    st = lambda k: jnp.stack(outs[k])
    return (xp.reshape(nbp, t_len, D_MODEL), xs.reshape(nbs, 1, D_MODEL),
            st("wkv_p"), st("shift_p"), st("kw_p"), st("vw_p"),
            st("wkv_s"), st("shift_s"), st("kw_s"), st("vw_s"))
```

```python
import functools
import math

import numpy as np
import jax
import jax.numpy as jnp
from jax import lax
from jax.experimental import pallas as pl
from jax.experimental.pallas import tpu as pltpu

F32 = jnp.float32
BF16 = jnp.bfloat16

D_MODEL = 1024
W_A = 512
HEAD_SIZE = 64
H_A = W_A // HEAD_SIZE
LORA = 64
SHIFT_W = 3 * W_A + 2 * LORA
W_B = 512
HEAD_DIM = 64
H_B = W_B // HEAD_DIM
KV_HEADS = 2
Q_PER_KV = H_B // KV_HEADS
KV_W = KV_HEADS * HEAD_DIM
WINDOW = 128
N_BUCKETS = 32
MAX_DISTANCE = 128
NORM_EPS = 1e-6
GN_EPS = 64e-5
NEG_INF = -1e30
QKV_W = W_B + 2 * KV_W
Z_W = W_A + W_B
GATE_W = 2 * D_MODEL
IN_COLS = SHIFT_W + QKV_W + Z_W + GATE_W
LANES = 128
HEAD_PERM = np.concatenate([np.arange(HEAD_DIM) + (g * Q_PER_KV + j) * HEAD_DIM
                            for j in range(Q_PER_KV) for g in range(KV_HEADS)])
ATTN_BLOCKS = 4
BIAS_SPAN = 3 * WINDOW
KEY_UNROLL = 32


def _dot(a, b):
    return jnp.dot(a, b, preferred_element_type=F32)


def _dot_nt(a, b):
    return lax.dot_general(a, b, (((1,), (1,)), ((), ())), preferred_element_type=F32)


def _split_dot(x, g):
    hi = x.astype(BF16)
    lo = (x - hi.astype(F32)).astype(BF16)
    return _dot(hi, g) + _dot(lo, g)


def _silu(x):
    return x * jax.nn.sigmoid(x)


def _ada_kernel(c_ref, w_ref, b_ref, o_ref):
    c = c_ref[...]
    o_ref[...] = _dot(_silu(c).astype(BF16), w_ref[...]) + b_ref[...]


def _ada(c, w_bf, b):
    nb, d = c.shape
    n = w_bf.shape[1]
    tn = 1024
    return pl.pallas_call(
        _ada_kernel,
        grid=(n // tn,),
        in_specs=[pl.BlockSpec((nb, d), lambda j: (0, 0)),
                  pl.BlockSpec((d, tn), lambda j: (0, j)),
                  pl.BlockSpec((1, tn), lambda j: (0, j))],
        out_specs=pl.BlockSpec((nb, tn), lambda j: (0, j)),
        out_shape=jax.ShapeDtypeStruct((nb, n), F32),
        name="ada",
    )(c, w_bf, b.reshape(1, n))


def _in_proj_kernel(x_ref, g_ref, scale_ref, shift_ref, w_ref, qg_ref, kg_ref, g512_ref, g128_ref,
                    ush_ref, qkv_ref, kv32_ref, z_ref, gate_ref):
    x = x_ref[...]
    ms = jnp.mean(x * x, axis=-1, keepdims=True)
    h = (x * lax.rsqrt(ms + NORM_EPS)) * g_ref[...]
    h = h * (1.0 + scale_ref[...]) + shift_ref[...]
    hb = h.astype(BF16)
    ush_ref[...] = _dot(hb, w_ref[:, 0:SHIFT_W])
    c0 = SHIFT_W
    qkv = _dot(hb, w_ref[:, c0:c0 + QKV_W])
    q, k, v = qkv[:, 0:W_B], qkv[:, W_B:W_B + KV_W], qkv[:, W_B + KV_W:QKV_W]
    qn = q * lax.rsqrt(_split_dot(q * q, g512_ref[...]) * (1.0 / HEAD_DIM) + NORM_EPS) * qg_ref[...]
    kn = k * lax.rsqrt(_split_dot(k * k, g128_ref[...]) * (1.0 / HEAD_DIM) + NORM_EPS) * kg_ref[...]
    qkv_ref[:, 0:W_B] = (qn * (HEAD_DIM ** -0.5)).astype(BF16)
    qkv_ref[:, W_B:W_B + KV_W] = kn.astype(BF16)
    qkv_ref[:, W_B + KV_W:QKV_W] = v.astype(BF16)
    kv32_ref[:, 0:KV_W] = kn
    kv32_ref[:, KV_W:2 * KV_W] = v
    c0 += QKV_W
    z_ref[...] = _dot(hb, w_ref[:, c0:c0 + Z_W])
    c0 += Z_W
    gate_ref[...] = _dot(hb, w_ref[:, c0:c0 + GATE_W])


def _mod_spec(per_row, tm, rows_per_seq):
    if per_row:
        return pl.BlockSpec((tm, D_MODEL), lambda i: (i, 0))
    return pl.BlockSpec((None, 1, D_MODEL), lambda i: ((i * tm) // rows_per_seq, 0, 0))


def _in_proj(x, norm_g, scale, shift, w_bf, q_gain, k_gain, g512, g128, tm, rows_per_seq, per_row):
    m = x.shape[0]
    const = lambda r, c: pl.BlockSpec((r, c), lambda i: (0, 0))
    widths = (SHIFT_W, QKV_W, 2 * KV_W, Z_W, GATE_W)
    dtypes = (F32, BF16, F32, F32, F32)
    return pl.pallas_call(
        _in_proj_kernel,
        grid=(m // tm,),
        in_specs=[pl.BlockSpec((tm, D_MODEL), lambda i: (i, 0)),
                  const(1, D_MODEL),
                  _mod_spec(per_row, tm, rows_per_seq),
                  _mod_spec(per_row, tm, rows_per_seq),
                  const(D_MODEL, IN_COLS), const(1, W_B), const(1, KV_W), const(W_B, W_B), const(KV_W, KV_W)],
        out_specs=[pl.BlockSpec((tm, w), lambda i: (i, 0)) for w in widths],
        out_shape=[jax.ShapeDtypeStruct((m, w), dt) for w, dt in zip(widths, dtypes)],
        name="in_proj",
    )(x, norm_g.reshape(1, D_MODEL), scale, shift, w_bf, q_gain, k_gain, g512, g128)


def _prep_kernel(u_ref, prev_ref, shift0_ref, mu_ref, w0_ref, a0_ref, lora_ref, kk_ref, ka_ref, rk_ref, g_ref,
                 r_o, w_o, k_o, v_o, kk_o, b_o, bonus_o, *, tiles_per_seq, per_row):
    u = u_ref[...]
    if per_row:
        u_prev = shift0_ref[...]
    else:
        first = (pl.program_id(0) % tiles_per_seq) == 0
        prev_row = jnp.where(first, shift0_ref[...], prev_ref[7:8, :])
        row = lax.broadcasted_iota(jnp.int32, u.shape, 0)
        u_prev = jnp.where(row == 0, prev_row, pltpu.roll(u, 1, axis=0))
    xs = u + (u_prev - u) * mu_ref[...]
    r = xs[:, 0:W_A]
    k = xs[:, W_A:2 * W_A]
    v = xs[:, 2 * W_A:3 * W_A]
    tail = xs[:, 3 * W_A:SHIFT_W]
    lane = lax.broadcasted_iota(jnp.int32, tail.shape, 1)
    lora_in = jnp.where(lane < LORA, jnp.tanh(tail), tail).astype(BF16)
    up = _dot(lora_in, lora_ref[...])
    wl = w0_ref[...] + up[:, 0:W_A]
    neg = -wl
    softplus = jnp.maximum(neg, 0.0) + jnp.log(1.0 + jnp.exp(-jnp.abs(neg)))
    w_log = -softplus - 0.5
    decay = jnp.exp(-jnp.exp(w_log))
    a = jax.nn.sigmoid(a0_ref[...] + up[:, W_A:2 * W_A])
    kk = k * kk_ref[...]
    norm = jnp.sqrt(_split_dot(kk * kk, g_ref[...]))
    kk = kk / jnp.maximum(norm, 1e-12)
    k_mod = k * (1.0 + (a - 1.0) * ka_ref[...])
    r_o[...] = r
    w_o[...] = decay
    k_o[...] = k_mod
    v_o[...] = v
    kk_o[...] = kk
    b_o[...] = kk * a
    bonus_o[...] = _split_dot(r * k_mod * rk_ref[...], g_ref[...]) * v


def _prep(ush, shift0, mu, w0, a0, lora_bf, k_k, k_a, r_k, g512, tm, rows_per_seq, per_row):
    m = ush.shape[0]
    tiles_per_seq = max(rows_per_seq // tm, 1)
    row = lambda a: a.reshape(1, -1)
    vec = lambda n: pl.BlockSpec((1, n), lambda i: (0, 0))
    if per_row:
        prev_spec = pl.BlockSpec((8, SHIFT_W), lambda i: (0, 0))
        shift0_spec = pl.BlockSpec((tm, SHIFT_W), lambda i: (i, 0))
    else:
        prev_spec = pl.BlockSpec((8, SHIFT_W), lambda i: (jnp.maximum(i * (tm // 8) - 1, 0), 0))
        shift0_spec = pl.BlockSpec((None, 1, SHIFT_W), lambda i: (i // tiles_per_seq, 0, 0))
    out = jax.ShapeDtypeStruct((m, W_A), F32)
    return pl.pallas_call(
        functools.partial(_prep_kernel, tiles_per_seq=tiles_per_seq, per_row=per_row),
        grid=(m // tm,),
        in_specs=[pl.BlockSpec((tm, SHIFT_W), lambda i: (i, 0)), prev_spec, shift0_spec,
                  vec(SHIFT_W), vec(W_A), vec(W_A),
                  pl.BlockSpec((2 * LORA, 2 * W_A), lambda i: (0, 0)),
                  vec(W_A), vec(W_A), vec(W_A),
                  pl.BlockSpec((W_A, W_A), lambda i: (0, 0))],
        out_specs=[pl.BlockSpec((tm, W_A), lambda i: (i, 0))] * 7,
        out_shape=[out] * 7,
        name="rwkv_prep",
    )(ush, ush, shift0, row(mu), row(w0), row(a0), lora_bf, row(k_k), row(k_a), row(r_k), g512)


def _scan_kernel(*refs, tt, zero_init):
    if zero_init:
        r_ref, w_ref, k_ref, v_ref, kk_ref, b_ref, o_ref, pf_ref, p_ref = refs
    else:
        r_ref, w_ref, k_ref, v_ref, kk_ref, b_ref, p0_ref, o_ref, pf_ref, p_ref = refs
    ti = pl.program_id(1)

    @pl.when(ti == 0)
    def _():
        if zero_init:
            p_ref[...] = jnp.zeros(p_ref.shape, F32)
        else:
            p_ref[...] = p0_ref[...]

    def step(t, carry):
        zero = jnp.zeros((HEAD_SIZE, LANES), F32)

        def reduce_keys(kb, part):
            part = list(part)
            for u in range(KEY_UNROLL):
                kc = kb * KEY_UNROLL + u
                part[u % 2] = part[u % 2] + p_ref[kc] * kk_ref[t, pl.ds(kc, 1), :]
            return tuple(part)

        part = lax.fori_loop(0, HEAD_SIZE // KEY_UNROLL, reduce_keys, (zero, zero))
        sa = -(part[0] + part[1])
        vt = v_ref[t]

        def update_keys(kb, out):
            for u in range(KEY_UNROLL):
                kc = kb * KEY_UNROLL + u
                p = (p_ref[kc] * w_ref[t, pl.ds(kc, 1), :] + sa * b_ref[t, pl.ds(kc, 1), :]
                     + vt * k_ref[t, pl.ds(kc, 1), :])
                p_ref[kc] = p
                out = out + p * r_ref[t, pl.ds(kc, 1), :]
            return out

        o_ref[t] = lax.fori_loop(0, HEAD_SIZE // KEY_UNROLL, update_keys, zero)
        return carry

    lax.fori_loop(0, tt, step, 0)

    @pl.when(ti == pl.num_programs(1) - 1)
    def _():
        pf_ref[...] = p_ref[...]


def _scan(seqs, p0, tt):
    t_len, _, n = seqs[0].shape
    seq_spec = pl.BlockSpec((tt, HEAD_SIZE, LANES), lambda j, i: (i, 0, j))
    state_spec = pl.BlockSpec((HEAD_SIZE, HEAD_SIZE, LANES), lambda j, i: (0, 0, j))
    zero_init = p0 is None
    args = list(seqs) + ([] if zero_init else [p0])
    return pl.pallas_call(
        functools.partial(_scan_kernel, tt=tt, zero_init=zero_init),
        grid=(n // LANES, t_len // tt),
        in_specs=[seq_spec] * 6 + ([] if zero_init else [state_spec]),
        out_specs=[seq_spec, state_spec],
        out_shape=[jax.ShapeDtypeStruct((t_len, HEAD_SIZE, n), F32),
                   jax.ShapeDtypeStruct((HEAD_SIZE, HEAD_SIZE, n), F32)],
        scratch_shapes=[pltpu.VMEM((HEAD_SIZE, HEAD_SIZE, LANES), F32)],
        compiler_params=pltpu.CompilerParams(dimension_semantics=("arbitrary", "arbitrary")),
        name="rwkv_scan",
    )(*args)


def _attn_kernel(cur_ref, prev_ref, zb_ref, gtab_ref, sink_ref, yb_ref, bias_ref, *, blocks):
    s = pl.program_id(1)
    rows = Q_PER_KV * WINDOW

    @pl.when((pl.program_id(0) == 0) & (s == 0))
    def _():
        qi = lax.broadcasted_iota(jnp.int32, (WINDOW, 2 * WINDOW), 0)
        kj = lax.broadcasted_iota(jnp.int32, (WINDOW, 2 * WINDOW), 1)
        dist = qi + WINDOW - kj
        band = (dist >= 0) & (dist <= WINDOW)
        for g in range(KV_HEADS):
            for j in range(Q_PER_KV):
                h = g * Q_PER_KV + j
                row = jnp.broadcast_to(gtab_ref[h:h + 1, :], (WINDOW, BIAS_SPAN))
                toe = pltpu.roll(row, 0, 1, stride=1, stride_axis=0)[:, WINDOW:BIAS_SPAN]
                bias_ref[g, j * WINDOW:(j + 1) * WINDOW, :] = jnp.where(band, toe, NEG_INF)

    cur = cur_ref[...]
    prev = prev_ref[...]
    k_all = jnp.concatenate([prev[:, W_B:W_B + KV_W], cur[:, W_B:W_B + KV_W]], axis=0)
    v_all = jnp.concatenate([prev[:, W_B + KV_W:QKV_W], cur[:, W_B + KV_W:QKV_W]], axis=0)
    lane = lax.broadcasted_iota(jnp.int32, k_all.shape, 1)
    zero = jnp.zeros_like(k_all)
    k_g = [jnp.where(lane < HEAD_DIM, k_all, zero), jnp.where(lane >= HEAD_DIM, k_all, zero)]
    v_g = [jnp.where(lane < HEAD_DIM, v_all, zero), jnp.where(lane >= HEAD_DIM, v_all, zero)]
    ones = jnp.ones((2 * WINDOW, LANES), BF16)
    kj = lax.broadcasted_iota(jnp.int32, (rows, 2 * WINDOW), 1)
    for i in range(blocks):
        r0 = i * WINDOW
        q4 = jnp.concatenate([cur[r0:r0 + WINDOW, j * LANES:(j + 1) * LANES] for j in range(Q_PER_KV)], axis=0)
        acc = None
        for g in range(KV_HEADS):
            logits = _dot_nt(q4, k_g[g][r0:r0 + 2 * WINDOW]) + bias_ref[g]
            if i == 0:
                logits = jnp.where((kj >= WINDOW) | (s > 0), logits, NEG_INF)
            sink = sink_ref[g]
            m = jnp.maximum(jnp.max(logits, axis=-1, keepdims=True), sink)
            p = jnp.exp(logits - m).astype(BF16)
            den = _dot(p, ones) + jnp.exp(sink - m)
            o = _dot(p, v_g[g][r0:r0 + 2 * WINDOW]) / den
            acc = o if acc is None else acc + o
        for j in range(Q_PER_KV):
            zb = zb_ref[r0:r0 + WINDOW, j * LANES:(j + 1) * LANES]
            yb_ref[r0:r0 + WINDOW, j * LANES:(j + 1) * LANES] = (
                acc[j * WINDOW:(j + 1) * WINDOW] * _silu(zb)).astype(BF16)


def _attn(qkv, z, nb_seq, gtab, sink_col):
    m = qkv.shape[0]
    nblk = m // (nb_seq * WINDOW)
    blocks = _pick((ATTN_BLOCKS, 2, 1), nblk)
    steps = nblk // blocks
    rows = Q_PER_KV * WINDOW
    return pl.pallas_call(
        functools.partial(_attn_kernel, blocks=blocks),
        grid=(nb_seq, steps),
        in_specs=[pl.BlockSpec((blocks * WINDOW, QKV_W), lambda b, s: (b * steps + s, 0)),
                  pl.BlockSpec((WINDOW, QKV_W), lambda b, s: (jnp.maximum((b * steps + s) * blocks - 1, 0), 0)),
                  pl.BlockSpec((blocks * WINDOW, W_B), lambda b, s: (b * steps + s, 1)),
                  pl.BlockSpec((H_B, BIAS_SPAN), lambda b, s: (0, 0)),
                  pl.BlockSpec((KV_HEADS, rows, 1), lambda b, s: (0, 0, 0))],
        out_specs=pl.BlockSpec((blocks * WINDOW, W_B), lambda b, s: (b * steps + s, 0)),
        out_shape=jax.ShapeDtypeStruct((m, W_B), BF16),
        scratch_shapes=[pltpu.VMEM((KV_HEADS, rows, 2 * WINDOW), F32)],
        compiler_params=pltpu.CompilerParams(dimension_semantics=("arbitrary", "arbitrary")),
        name="swa_prompt",
    )(qkv, qkv, z, gtab, sink_col)


def _attn_step_kernel(q_ref, kn_ref, vn_ref, zb_ref, ck_ref, cv_ref, bias_ref, bias0_ref, sink_ref, g128_ref,
                      yb_ref, ko_ref, vo_ref):
    q = q_ref[...].astype(F32)
    kn = kn_ref[...]
    vn = vn_ref[...]
    ck = ck_ref[...]
    cv = cv_ref[...]
    bt = q.shape[0]
    for r in range(Q_PER_KV):
        qsel = q[:, r * LANES:(r + 1) * LANES]
        prod = (ck * qsel[:, None, :]).reshape(bt * WINDOW, KV_W)
        lg = _split_dot(prod, g128_ref[...]).reshape(bt, WINDOW, KV_W) + bias_ref[r]
        lgn = _split_dot(kn * qsel, g128_ref[...]) + bias0_ref[r]
        s = sink_ref[r]
        m = jnp.maximum(jnp.maximum(jnp.max(lg, axis=1), lgn), s)
        p = jnp.exp(lg - m[:, None, :])
        pn = jnp.exp(lgn - m)
        den = jnp.sum(p, axis=1) + pn + jnp.exp(s - m)
        o = (jnp.sum(p * cv, axis=1) + pn * vn) / den
        zb = zb_ref[:, r * LANES:(r + 1) * LANES]
        yb_ref[:, r * LANES:(r + 1) * LANES] = (o * _silu(zb)).astype(BF16)
    j = lax.broadcasted_iota(jnp.int32, ck.shape, 1)
    ko_ref[...] = jnp.where(j == WINDOW - 1, kn[:, None, :], pltpu.roll(ck, WINDOW - 1, axis=1))
    vo_ref[...] = jnp.where(j == WINDOW - 1, vn[:, None, :], pltpu.roll(cv, WINDOW - 1, axis=1))


def _attn_step(qkv, kv32, z, cache_k, cache_v, bias_rows, bias0, sink_rows, g128, bt):
    nb = qkv.shape[0]
    cache_spec = pl.BlockSpec((bt, WINDOW, KV_W), lambda i: (i, 0, 0))
    return pl.pallas_call(
        _attn_step_kernel,
        grid=(nb // bt,),
        in_specs=[pl.BlockSpec((bt, W_B), lambda i: (i, 0)),
                  pl.BlockSpec((bt, KV_W), lambda i: (i, 0)),
                  pl.BlockSpec((bt, KV_W), lambda i: (i, 1)),
                  pl.BlockSpec((bt, W_B), lambda i: (i, 1)),
                  cache_spec, cache_spec,
                  pl.BlockSpec((Q_PER_KV, WINDOW, KV_W), lambda i: (0, 0, 0)),
                  pl.BlockSpec((Q_PER_KV, 1, KV_W), lambda i: (0, 0, 0)),
                  pl.BlockSpec((Q_PER_KV, 1, KV_W), lambda i: (0, 0, 0)),
                  pl.BlockSpec((KV_W, KV_W), lambda i: (0, 0))],
        out_specs=[pl.BlockSpec((bt, W_B), lambda i: (i, 0)), cache_spec, cache_spec],
        out_shape=[jax.ShapeDtypeStruct((nb, W_B), BF16),
                   jax.ShapeDtypeStruct(cache_k.shape, F32),
                   jax.ShapeDtypeStruct(cache_v.shape, F32)],
        name="swa_step",
    )(qkv, kv32, kv32, z, cache_k, cache_v, bias_rows, bias0, sink_rows, g128)


def _merge_kernel(o_ref, bonus_ref, za_ref, yb_ref, ga_ref, gb_ref, x_ref, gate_ref, lng_ref, lnb_ref, g512_ref,
                  woa_ref, wob_ref, wout_ref, out_ref):
    o = o_ref[...]
    mean = _split_dot(o, g512_ref[...]) * (1.0 / HEAD_SIZE)
    c = o - mean
    var = _split_dot(c * c, g512_ref[...]) * (1.0 / HEAD_SIZE)
    on = c * lax.rsqrt(var + GN_EPS) * lng_ref[...] + lnb_ref[...]
    ya = (on + bonus_ref[...]) * _silu(za_ref[...])
    pa = _dot(ya.astype(BF16), woa_ref[...])
    pb = _dot(yb_ref[...], wob_ref[...])
    merged = jax.nn.sigmoid(ga_ref[...]) * pa + jax.nn.sigmoid(gb_ref[...]) * pb
    out_ref[...] = x_ref[...] + gate_ref[...] * _dot(merged.astype(BF16), wout_ref[...])


def _merge(o, bonus, z, yb, gates, x, gate, lnx_g, lnx_b, g512, woa_bf, wob_bf, wout_bf, tm, rows_per_seq, per_row):
    m = x.shape[0]
    half = lambda c: pl.BlockSpec((tm, W_A), lambda i: (i, c))
    full = lambda c: pl.BlockSpec((tm, D_MODEL), lambda i: (i, c))
    vec = pl.BlockSpec((1, W_A), lambda i: (0, 0))
    return pl.pallas_call(
        _merge_kernel,
        grid=(m // tm,),
        in_specs=[half(0), half(0), half(0), half(0), full(0), full(1), full(0),
                  _mod_spec(per_row, tm, rows_per_seq), vec, vec,
                  pl.BlockSpec((W_A, W_A), lambda i: (0, 0)),
                  pl.BlockSpec((W_A, D_MODEL), lambda i: (0, 0)),
                  pl.BlockSpec((W_B, D_MODEL), lambda i: (0, 0)),
                  pl.BlockSpec((D_MODEL, D_MODEL), lambda i: (0, 0))],
        out_specs=full(0),
        out_shape=jax.ShapeDtypeStruct((m, D_MODEL), F32),
        name="merge_out",
    )(o, bonus, z, yb, gates, gates, x, gate, lnx_g.reshape(1, W_A), lnx_b.reshape(1, W_A), g512,
      woa_bf, wob_bf, wout_bf)


def _t5_bucket(dist):
    max_exact = N_BUCKETS // 2
    d = jnp.maximum(dist, 0)
    log_ratio = jnp.log(jnp.maximum(d, 1).astype(F32) / max_exact) / math.log(MAX_DISTANCE / max_exact)
    large = jnp.minimum(max_exact + (log_ratio * (N_BUCKETS - max_exact)).astype(jnp.int32), N_BUCKETS - 1)
    return jnp.where(d < max_exact, d, large)


def _block_ones(n, blk):
    i = np.arange(n) // blk
    return jnp.asarray((i[:, None] == i[None, :]).astype(np.float32), dtype=BF16)


def _to_lanes(a, nb, t_len, n_pad):
    a = a.reshape(nb, t_len, H_A, HEAD_SIZE).transpose(1, 3, 0, 2).reshape(t_len, HEAD_SIZE, nb * H_A)
    if n_pad != nb * H_A:
        a = jnp.pad(a, ((0, 0), (0, 0), (0, n_pad - nb * H_A)))
    return a


def _pick(cands, n):
    for c in cands:
        if n % c == 0:
            return c
    return n


def _rwkv(ush, shift0, state0, nb, t_len, lw, g512, per_row):
    m = nb * t_len
    tm = _pick((256, 128, 64, 32, 16, 8), m if per_row else t_len)
    outs = _prep(ush, shift0, lw["mu"], lw["w0"], lw["a0"], lw["lora"], lw["k_k"], lw["k_a"], lw["r_k"], g512,
                 tm, t_len, per_row)
    bonus = outs[6]
    n = nb * H_A
    n_pad = -(-n // LANES) * LANES
    seqs = [_to_lanes(a, nb, t_len, n_pad) for a in outs[:6]]
    if state0 is None:
        p0 = None
    else:
        p0 = state0.transpose(3, 2, 0, 1).reshape(HEAD_SIZE, HEAD_SIZE, n)
        if n_pad != n:
            p0 = jnp.pad(p0, ((0, 0), (0, 0), (0, n_pad - n)))
    o_t, pf = _scan(seqs, p0, _pick((16, 8, 4, 2, 1), t_len))
    o = o_t[:, :, :n].reshape(t_len, HEAD_SIZE, nb, H_A).transpose(2, 0, 3, 1).reshape(m, W_A)
    wkv = pf[:, :, :n].reshape(HEAD_SIZE, HEAD_SIZE, nb, H_A).transpose(2, 3, 1, 0)
    return o, bonus, wkv


def kernel(x_prompt, x_sample, c_prompt, c_sample, state_wkv, state_shift, cache_k, cache_v, norm_g, w_ada, b_ada, w_in, mu_shift, w0, w_decay_up, a0, w_a_up, k_k, k_a, r_k, lnx_g, lnx_b, w_o_a, q_norm_g, k_norm_g, rel_bias, sinks, w_o_b, w_out):
    nbp, t_len, _ = x_prompt.shape
    nbs = x_sample.shape[0]
    depth = norm_g.shape[0]
    mp = nbp * t_len
    g512 = _block_ones(W_A, HEAD_SIZE)
    g128 = _block_ones(KV_W, HEAD_DIM)

    gtab = rel_bias[_t5_bucket(2 * WINDOW - jnp.arange(BIAS_SPAN))].astype(F32).T
    bias_s = rel_bias[_t5_bucket(WINDOW - jnp.arange(WINDOW + 1))].astype(F32)
    pair = lambda a: jnp.concatenate([jnp.repeat(a[..., 0:Q_PER_KV, None], HEAD_DIM, axis=-1),
                                      jnp.repeat(a[..., Q_PER_KV:H_B, None], HEAD_DIM, axis=-1)], axis=-1)
    bias_rows = pair(bias_s[:WINDOW]).transpose(1, 0, 2)
    bias0 = pair(bias_s[WINDOW:]).transpose(1, 0, 2)

    c_all = jnp.concatenate([c_prompt, c_sample], axis=0)
    xp = x_prompt.reshape(mp, D_MODEL)
    xs = x_sample.reshape(nbs, D_MODEL)
    zero_shift = jnp.zeros((nbp, 1, SHIFT_W), F32)
    tm_p = _pick((256, 128), t_len)
    tm_s = _pick((128, 64, 32, 16, 8), nbs)
    outs = {k: [] for k in ("wkv_p", "shift_p", "kw_p", "vw_p", "wkv_s", "shift_s", "kw_s", "vw_s")}
    for l in range(depth):
        wl = w_in[l]
        base = SHIFT_W
        z_a, q, kb, vb, z_b, gts = (wl[:, base:base + 512], wl[:, base + 512:base + 1024],
                                    wl[:, base + 1024:base + 1152], wl[:, base + 1152:base + 1280],
                                    wl[:, base + 1280:base + 1792], wl[:, base + 1792:])
        w_in_bf = jnp.concatenate([wl[:, :SHIFT_W], q[:, HEAD_PERM], kb, vb, z_a, z_b[:, HEAD_PERM], gts],
                                  axis=1).astype(BF16)
        zeros = jnp.zeros((LORA, W_A), F32)
        lora = jnp.concatenate([jnp.concatenate([w_decay_up[l], zeros], axis=1),
                                jnp.concatenate([zeros, w_a_up[l]], axis=1)], axis=0).astype(BF16)
        lw = dict(mu=mu_shift[l], w0=w0[l], a0=a0[l], lora=lora, k_k=k_k[l], k_a=k_a[l], r_k=r_k[l].reshape(-1))
        woa_bf, wout_bf = w_o_a[l].astype(BF16), w_out[l].astype(BF16)
        wob_bf = w_o_b[l][HEAD_PERM, :].astype(BF16)
        q_gain = jnp.tile(q_norm_g[l], H_B).reshape(1, W_B)
        k_gain = jnp.tile(k_norm_g[l], KV_HEADS).reshape(1, KV_W)
        sink_col = jnp.repeat(sinks[l].reshape(KV_HEADS, Q_PER_KV), WINDOW, axis=1).reshape(
            KV_HEADS, Q_PER_KV * WINDOW, 1)
        sink_rows = pair(sinks[l].reshape(1, H_B)).transpose(1, 0, 2)

        mod = _ada(c_all, w_ada[l].astype(BF16), b_ada[l])
        shift, scale, gate = mod[:, :D_MODEL], mod[:, D_MODEL:2 * D_MODEL], mod[:, 2 * D_MODEL:]

        sp, scp, gp = (a[:nbp].reshape(nbp, 1, D_MODEL) for a in (shift, scale, gate))
        ush, qkv, kv32, z, gts_p = _in_proj(xp, norm_g[l], scp, sp, w_in_bf, q_gain, k_gain, g512, g128,
                                            tm_p, t_len, False)
        o, bonus, wkv = _rwkv(ush, zero_shift, None, nbp, t_len, lw, g512, False)
        yb = _attn(qkv, z, nbp, gtab, sink_col)
        xp = _merge(o, bonus, z, yb, gts_p, xp, gp, lnx_g[l], lnx_b[l], g512, woa_bf, wob_bf, wout_bf,
                    tm_p, t_len, False)
        win = kv32.reshape(nbp, t_len, 2, KV_HEADS, HEAD_DIM)[:, t_len - WINDOW:]
        outs["wkv_p"].append(wkv)
        outs["shift_p"].append(ush.reshape(nbp, t_len, SHIFT_W)[:, -1])
        outs["kw_p"].append(win[:, :, 0])
        outs["vw_p"].append(win[:, :, 1])

        ss, scs, gs = shift[nbp:], scale[nbp:], gate[nbp:]
        ush, qkv, kv32, z, gts_s = _in_proj(xs, norm_g[l], scs, ss, w_in_bf, q_gain, k_gain, g512, g128,
                                            tm_s, 1, True)
        o, bonus, wkv = _rwkv(ush, state_shift[l], state_wkv[l], nbs, 1, lw, g512, True)
        yb, kwin, vwin = _attn_step(qkv, kv32, z, cache_k[l].reshape(nbs, WINDOW, KV_W),
                                    cache_v[l].reshape(nbs, WINDOW, KV_W),
                                    bias_rows, bias0, sink_rows, g128, _pick((16, 8), nbs))
        xs = _merge(o, bonus, z, yb, gts_s, xs, gs, lnx_g[l], lnx_b[l], g512, woa_bf, wob_bf, wout_bf,
                    tm_s, 1, True)
        outs["wkv_s"].append(wkv)
        outs["shift_s"].append(ush)
        outs["kw_s"].append(kwin.reshape(nbs, WINDOW, KV_HEADS, HEAD_DIM))
        outs["vw_s"].append(vwin.reshape(nbs, WINDOW, KV_HEADS, HEAD_DIM))

    st = lambda k: jnp.stack(outs[k])
    return (xp.reshape(nbp, t_len, D_MODEL), xs.reshape(nbs, 1, D_MODEL),
            st("wkv_p"), st("shift_p"), st("kw_p"), st("vw_p"),
            st("wkv_s"), st("shift_s"), st("kw_s"), st("vw_s"))
```

```python
import functools
import math

import numpy as np
import jax
import jax.numpy as jnp
from jax import lax
from jax.experimental import pallas as pl
from jax.experimental.pallas import tpu as pltpu

F32 = jnp.float32
BF16 = jnp.bfloat16

D_MODEL = 1024
W_A = 512
HEAD_SIZE = 64
H_A = W_A // HEAD_SIZE
LORA = 64
SHIFT_W = 3 * W_A + 2 * LORA
W_B = 512
HEAD_DIM = 64
H_B = W_B // HEAD_DIM
KV_HEADS = 2
Q_PER_KV = H_B // KV_HEADS
KV_W = KV_HEADS * HEAD_DIM
WINDOW = 128
N_BUCKETS = 32
MAX_DISTANCE = 128
NORM_EPS = 1e-6
GN_EPS = 64e-5
NEG_INF = -1e30
QKV_W = W_B + 2 * KV_W
GATE_W = 2 * D_MODEL
ROW_COLS = QKV_W + W_B + GATE_W
T_COLS = SHIFT_W + W_A
LANES = 128
HEAD_PERM = np.concatenate([np.arange(HEAD_DIM) + (g * Q_PER_KV + j) * HEAD_DIM
                            for j in range(Q_PER_KV) for g in range(KV_HEADS)])
ATTN_BLOCKS = 4
BIAS_SPAN = 3 * WINDOW
KEY_UNROLL = 32
SWAP_CHANNELS = 8


def _dot(a, b):
    return jnp.dot(a, b, preferred_element_type=F32)


def _dot_nt(a, b):
    return lax.dot_general(a, b, (((1,), (1,)), ((), ())), preferred_element_type=F32)


def _dot_tn(a, b):
    return lax.dot_general(a, b, (((0,), (0,)), ((), ())), preferred_element_type=F32)


def _split_dot(x, g):
    hi = x.astype(BF16)
    lo = (x - hi.astype(F32)).astype(BF16)
    return _dot(hi, g) + _dot(lo, g)


def _silu(x):
    return x * jax.nn.sigmoid(x)


def _pick(cands, n):
    for c in cands:
        if n % c == 0:
            return c
    return n


def _ada_kernel(c_ref, w_ref, b_ref, o_ref):
    c = c_ref[...]
    o_ref[...] = _dot(_silu(c).astype(BF16), w_ref[...]) + b_ref[...]


def _ada(c, w_bf, b):
    nb, d = c.shape
    n = w_bf.shape[1]
    tn = 1024
    return pl.pallas_call(
        _ada_kernel,
        grid=(n // tn,),
        in_specs=[pl.BlockSpec((nb, d), lambda j: (0, 0)),
                  pl.BlockSpec((d, tn), lambda j: (0, j)),
                  pl.BlockSpec((1, tn), lambda j: (0, j))],
        out_specs=pl.BlockSpec((nb, tn), lambda j: (0, j)),
        out_shape=jax.ShapeDtypeStruct((nb, n), F32),
        name="ada",
    )(c, w_bf, b.reshape(1, n))


def _in_proj_kernel(x_ref, g_ref, scale_ref, shift_ref, w_ref, wt_ref, qg_ref, kg_ref, g512_ref, g128_ref,
                    usht_ref, zat_ref, qkv_ref, kv32_ref, zb_ref, gate_ref):
    x = x_ref[...]
    ms = jnp.mean(x * x, axis=-1, keepdims=True)
    h = (x * lax.rsqrt(ms + NORM_EPS)) * g_ref[...]
    h = h * (1.0 + scale_ref[...]) + shift_ref[...]
    hb = h.astype(BF16)
    usht_ref[...] = _dot_nt(wt_ref[0:SHIFT_W, :], hb)
    zat_ref[...] = _dot_nt(wt_ref[SHIFT_W:T_COLS, :], hb)
    qkv = _dot(hb, w_ref[:, 0:QKV_W])
    q, k, v = qkv[:, 0:W_B], qkv[:, W_B:W_B + KV_W], qkv[:, W_B + KV_W:QKV_W]
    qn = q * lax.rsqrt(_split_dot(q * q, g512_ref[...]) * (1.0 / HEAD_DIM) + NORM_EPS) * qg_ref[...]
    kn = k * lax.rsqrt(_split_dot(k * k, g128_ref[...]) * (1.0 / HEAD_DIM) + NORM_EPS) * kg_ref[...]
    qkv_ref[:, 0:W_B] = (qn * (HEAD_DIM ** -0.5)).astype(BF16)
    qkv_ref[:, W_B:W_B + KV_W] = kn.astype(BF16)
    qkv_ref[:, W_B + KV_W:QKV_W] = v.astype(BF16)
    kv32_ref[:, 0:KV_W] = kn
    kv32_ref[:, KV_W:2 * KV_W] = v
    zb_ref[...] = _dot(hb, w_ref[:, QKV_W:QKV_W + W_B])
    gate_ref[...] = _dot(hb, w_ref[:, QKV_W + W_B:ROW_COLS])


def _mod_spec(per_row, tm, rows_per_seq):
    if per_row:
        return pl.BlockSpec((tm, D_MODEL), lambda i: (i, 0))
    return pl.BlockSpec((None, 1, D_MODEL), lambda i: ((i * tm) // rows_per_seq, 0, 0))


def _t_spec(cols, tm, rows_per_seq):
    per_seq = rows_per_seq // tm
    return pl.BlockSpec((None, cols, tm), lambda i: (i // per_seq, 0, i % per_seq))


def _in_proj(x, norm_g, scale, shift, w_bf, wt_bf, q_gain, k_gain, g512, g128, tm, rows_per_seq, per_row):
    m = x.shape[0]
    nseq = m // rows_per_seq
    const = lambda r, c: pl.BlockSpec((r, c), lambda i: (0, 0))
    widths = (QKV_W, 2 * KV_W, W_B, GATE_W)
    dtypes = (BF16, F32, F32, F32)
    return pl.pallas_call(
        _in_proj_kernel,
        grid=(m // tm,),
        in_specs=[pl.BlockSpec((tm, D_MODEL), lambda i: (i, 0)),
                  const(1, D_MODEL),
                  _mod_spec(per_row, tm, rows_per_seq),
                  _mod_spec(per_row, tm, rows_per_seq),
                  const(D_MODEL, ROW_COLS), const(T_COLS, D_MODEL),
                  const(1, W_B), const(1, KV_W), const(W_B, W_B), const(KV_W, KV_W)],
        out_specs=[_t_spec(SHIFT_W, tm, rows_per_seq), _t_spec(W_A, tm, rows_per_seq)]
                  + [pl.BlockSpec((tm, w), lambda i: (i, 0)) for w in widths],
        out_shape=[jax.ShapeDtypeStruct((nseq, SHIFT_W, rows_per_seq), F32),
                   jax.ShapeDtypeStruct((nseq, W_A, rows_per_seq), F32)]
                  + [jax.ShapeDtypeStruct((m, w), dt) for w, dt in zip(widths, dtypes)],
        name="in_proj",
    )(x, norm_g.reshape(1, D_MODEL), scale, shift, w_bf, wt_bf, q_gain, k_gain, g512, g128)


def _prep_kernel(u_ref, prev_ref, shift0_ref, mu_ref, w0_ref, a0_ref, lora_ref, kk_ref, ka_ref,
                 r_o, w_o, k_o, v_o, kk_o, b_o, *, per_token_state):
    u = u_ref[...]
    tm = u.shape[1]
    if per_token_state:
        u_prev = shift0_ref[...]
    else:
        first = pl.program_id(1) == 0
        prev_col = jnp.where(first, shift0_ref[...], prev_ref[:, LANES - 1:LANES])
        lane = lax.broadcasted_iota(jnp.int32, u.shape, 1)
        u_prev = jnp.where(lane == 0, prev_col, pltpu.roll(u, 1, axis=1))
    xs = u + (u_prev - u) * mu_ref[...]
    r = xs[0:W_A]
    k = xs[W_A:2 * W_A]
    v = xs[2 * W_A:3 * W_A]
    tail = xs[3 * W_A:SHIFT_W]
    row = lax.broadcasted_iota(jnp.int32, tail.shape, 0)
    lora_in = jnp.where(row < LORA, jnp.tanh(tail), tail).astype(BF16)
    up = _dot(lora_ref[...], lora_in)
    neg = -(w0_ref[...] + up[0:W_A])
    softplus = jnp.maximum(neg, 0.0) + jnp.log(1.0 + jnp.exp(-jnp.abs(neg)))
    w_log = -softplus - 0.5
    decay = jnp.exp(-jnp.exp(w_log))
    a = jax.nn.sigmoid(a0_ref[...] + up[W_A:2 * W_A])
    kk = (k * kk_ref[...]).reshape(H_A, HEAD_SIZE, tm)
    norm = jnp.sqrt(jnp.sum(kk * kk, axis=1, keepdims=True))
    kk = (kk / jnp.maximum(norm, 1e-12)).reshape(W_A, tm)
    r_o[...] = r
    w_o[...] = decay
    k_o[...] = k * (1.0 + (a - 1.0) * ka_ref[...])
    v_o[...] = v
    kk_o[...] = kk
    b_o[...] = kk * a


def _prep(usht, shift0t, mu, w0, a0, lorat_bf, k_k, k_a, tm, per_token_state):
    nseq, _, t_len = usht.shape
    col = lambda a: a.reshape(-1, 1)
    cvec = lambda n: pl.BlockSpec((n, 1), lambda s, j: (0, 0))
    blk = lambda cols: pl.BlockSpec((None, cols, tm), lambda s, j: (s, 0, j))
    if per_token_state:
        prev_spec = pl.BlockSpec((None, SHIFT_W, LANES), lambda s, j: (0, 0, 0))
        shift0_spec = blk(SHIFT_W)
    else:
        prev_spec = pl.BlockSpec((None, SHIFT_W, LANES),
                                 lambda s, j: (s, 0, jnp.maximum(j * (tm // LANES) - 1, 0)))
        shift0_spec = pl.BlockSpec((None, SHIFT_W, 1), lambda s, j: (s, 0, 0))
    out = jax.ShapeDtypeStruct((nseq, W_A, t_len), F32)
    return pl.pallas_call(
        functools.partial(_prep_kernel, per_token_state=per_token_state),
        grid=(nseq, t_len // tm),
        in_specs=[blk(SHIFT_W), prev_spec, shift0_spec, cvec(SHIFT_W), cvec(W_A), cvec(W_A),
                  pl.BlockSpec((2 * W_A, 2 * LORA), lambda s, j: (0, 0)), cvec(W_A), cvec(W_A)],
        out_specs=[blk(W_A)] * 6,
        out_shape=[out] * 6,
        name="rwkv_prep",
    )(usht, usht, shift0t, col(mu), col(w0), col(a0), lorat_bf, col(k_k), col(k_a))


def _swap_kernel(*refs):
    half = len(refs) // 2
    for x_ref, o_ref in zip(refs[:half], refs[half:]):
        for c in range(SWAP_CHANNELS):
            o_ref[:, c, :] = x_ref[:, c, :].T


def _swap(arrays):
    a_len, c_len, b_len = arrays[0].shape
    blk = (LANES, SWAP_CHANNELS, LANES)
    return pl.pallas_call(
        _swap_kernel,
        grid=(b_len // LANES, a_len // LANES, c_len // SWAP_CHANNELS),
        in_specs=[pl.BlockSpec(blk, lambda i, k, j: (k, j, i))] * len(arrays),
        out_specs=[pl.BlockSpec(blk, lambda i, k, j: (i, j, k))] * len(arrays),
        out_shape=[jax.ShapeDtypeStruct((b_len, c_len, a_len), F32)] * len(arrays),
        name="layout_swap",
    )(*arrays)


def _scan_kernel(*refs, tt, zero_init):
    if zero_init:
        r_ref, w_ref, k_ref, v_ref, kk_ref, b_ref, rk_ref, lng_ref, lnb_ref, y_ref, pf_ref, p_ref = refs
    else:
        r_ref, w_ref, k_ref, v_ref, kk_ref, b_ref, rk_ref, lng_ref, lnb_ref, p0_ref, y_ref, pf_ref, p_ref = refs
    ti = pl.program_id(1)

    @pl.when(ti == 0)
    def _():
        if zero_init:
            p_ref[...] = jnp.zeros(p_ref.shape, F32)
        else:
            p_ref[...] = p0_ref[...]

    def step(t, carry):
        zero = jnp.zeros((HEAD_SIZE, LANES), F32)

        def reduce_keys(kb, part):
            part = list(part)
            for u in range(KEY_UNROLL):
                kc = kb * KEY_UNROLL + u
                part[u % 2] = part[u % 2] + p_ref[kc] * kk_ref[t, pl.ds(kc, 1), :]
            return tuple(part)

        part = lax.fori_loop(0, HEAD_SIZE // KEY_UNROLL, reduce_keys, (zero, zero))
        sa = -(part[0] + part[1])
        vt = v_ref[t]

        def update_keys(kb, out):
            for u in range(KEY_UNROLL):
                kc = kb * KEY_UNROLL + u
                p = (p_ref[kc] * w_ref[t, pl.ds(kc, 1), :] + sa * b_ref[t, pl.ds(kc, 1), :]
                     + vt * k_ref[t, pl.ds(kc, 1), :])
                p_ref[kc] = p
                out = out + p * r_ref[t, pl.ds(kc, 1), :]
            return out

        o = lax.fori_loop(0, HEAD_SIZE // KEY_UNROLL, update_keys, zero)
        mean = jnp.sum(o, axis=0, keepdims=True) * (1.0 / HEAD_SIZE)
        c = o - mean
        var = jnp.sum(c * c, axis=0, keepdims=True) * (1.0 / HEAD_SIZE)
        coef = jnp.sum(r_ref[t] * k_ref[t] * rk_ref[...], axis=0, keepdims=True)
        y_ref[t] = c * lax.rsqrt(var + GN_EPS) * lng_ref[...] + lnb_ref[...] + coef * vt
        return carry

    lax.fori_loop(0, tt, step, 0)

    @pl.when(ti == pl.num_programs(1) - 1)
    def _():
        pf_ref[...] = p_ref[...]


def _scan(seqs, tiles, p0, tt):
    t_len, _, n = seqs[0].shape
    seq_spec = pl.BlockSpec((tt, HEAD_SIZE, LANES), lambda j, i: (i, 0, j))
    tile_spec = pl.BlockSpec((HEAD_SIZE, LANES), lambda j, i: (0, j))
    state_spec = pl.BlockSpec((HEAD_SIZE, HEAD_SIZE, LANES), lambda j, i: (0, 0, j))
    zero_init = p0 is None
    args = list(seqs) + list(tiles) + ([] if zero_init else [p0])
    return pl.pallas_call(
        functools.partial(_scan_kernel, tt=tt, zero_init=zero_init),
        grid=(n // LANES, t_len // tt),
        in_specs=[seq_spec] * 6 + [tile_spec] * 3 + ([] if zero_init else [state_spec]),
        out_specs=[seq_spec, state_spec],
        out_shape=[jax.ShapeDtypeStruct((t_len, HEAD_SIZE, n), F32),
                   jax.ShapeDtypeStruct((HEAD_SIZE, HEAD_SIZE, n), F32)],
        scratch_shapes=[pltpu.VMEM((HEAD_SIZE, HEAD_SIZE, LANES), F32)],
        compiler_params=pltpu.CompilerParams(dimension_semantics=("arbitrary", "arbitrary")),
        name="rwkv_scan",
    )(*args)


def _attn_kernel(cur_ref, prev_ref, zb_ref, gtab_ref, sink_ref, yb_ref, bias_ref, *, blocks):
    s = pl.program_id(1)
    rows = Q_PER_KV * WINDOW

    @pl.when((pl.program_id(0) == 0) & (s == 0))
    def _():
        qi = lax.broadcasted_iota(jnp.int32, (WINDOW, 2 * WINDOW), 0)
        kj = lax.broadcasted_iota(jnp.int32, (WINDOW, 2 * WINDOW), 1)
        dist = qi + WINDOW - kj
        band = (dist >= 0) & (dist <= WINDOW)
        for g in range(KV_HEADS):
            for j in range(Q_PER_KV):
                h = g * Q_PER_KV + j
                row = jnp.broadcast_to(gtab_ref[h:h + 1, :], (WINDOW, BIAS_SPAN))
                toe = pltpu.roll(row, 0, 1, stride=1, stride_axis=0)[:, WINDOW:BIAS_SPAN]
                bias_ref[g, j * WINDOW:(j + 1) * WINDOW, :] = jnp.where(band, toe, NEG_INF)

    cur = cur_ref[...]
    prev = prev_ref[...]
    k_all = jnp.concatenate([prev[:, W_B:W_B + KV_W], cur[:, W_B:W_B + KV_W]], axis=0)
    v_all = jnp.concatenate([prev[:, W_B + KV_W:QKV_W], cur[:, W_B + KV_W:QKV_W]], axis=0)
    lane = lax.broadcasted_iota(jnp.int32, k_all.shape, 1)
    zero = jnp.zeros_like(k_all)
    k_g = [jnp.where(lane < HEAD_DIM, k_all, zero), jnp.where(lane >= HEAD_DIM, k_all, zero)]
    v_g = [jnp.where(lane < HEAD_DIM, v_all, zero), jnp.where(lane >= HEAD_DIM, v_all, zero)]
    ones = jnp.ones((2 * WINDOW, LANES), BF16)
    kj = lax.broadcasted_iota(jnp.int32, (rows, 2 * WINDOW), 1)
    for i in range(blocks):
        r0 = i * WINDOW
        q4 = jnp.concatenate([cur[r0:r0 + WINDOW, j * LANES:(j + 1) * LANES] for j in range(Q_PER_KV)], axis=0)
        acc = None
        for g in range(KV_HEADS):
            logits = _dot_nt(q4, k_g[g][r0:r0 + 2 * WINDOW]) + bias_ref[g]
            if i == 0:
                logits = jnp.where((kj >= WINDOW) | (s > 0), logits, NEG_INF)
            sink = sink_ref[g]
            m = jnp.maximum(jnp.max(logits, axis=-1, keepdims=True), sink)
            p = jnp.exp(logits - m).astype(BF16)
            den = _dot(p, ones) + jnp.exp(sink - m)
            o = _dot(p, v_g[g][r0:r0 + 2 * WINDOW]) / den
            acc = o if acc is None else acc + o
        for j in range(Q_PER_KV):
            zb = zb_ref[r0:r0 + WINDOW, j * LANES:(j + 1) * LANES]
            yb_ref[r0:r0 + WINDOW, j * LANES:(j + 1) * LANES] = (
                acc[j * WINDOW:(j + 1) * WINDOW] * _silu(zb)).astype(BF16)


def _attn(qkv, zb, nb_seq, gtab, sink_col):
    m = qkv.shape[0]
    nblk = m // (nb_seq * WINDOW)
    blocks = _pick((ATTN_BLOCKS, 2, 1), nblk)
    steps = nblk // blocks
    rows = Q_PER_KV * WINDOW
    return pl.pallas_call(
        functools.partial(_attn_kernel, blocks=blocks),
        grid=(nb_seq, steps),
        in_specs=[pl.BlockSpec((blocks * WINDOW, QKV_W), lambda b, s: (b * steps + s, 0)),
                  pl.BlockSpec((WINDOW, QKV_W), lambda b, s: (jnp.maximum((b * steps + s) * blocks - 1, 0), 0)),
                  pl.BlockSpec((blocks * WINDOW, W_B), lambda b, s: (b * steps + s, 0)),
                  pl.BlockSpec((H_B, BIAS_SPAN), lambda b, s: (0, 0)),
                  pl.BlockSpec((KV_HEADS, rows, 1), lambda b, s: (0, 0, 0))],
        out_specs=pl.BlockSpec((blocks * WINDOW, W_B), lambda b, s: (b * steps + s, 0)),
        out_shape=jax.ShapeDtypeStruct((m, W_B), BF16),
        scratch_shapes=[pltpu.VMEM((KV_HEADS, rows, 2 * WINDOW), F32)],
        compiler_params=pltpu.CompilerParams(dimension_semantics=("arbitrary", "arbitrary")),
        name="swa_prompt",
    )(qkv, qkv, zb, gtab, sink_col)


def _attn_step_kernel(q_ref, kn_ref, vn_ref, zb_ref, ck_ref, cv_ref, bias_ref, bias0_ref, sink_ref, g128_ref,
                      yb_ref, ko_ref, vo_ref):
    q = q_ref[...].astype(F32)
    kn = kn_ref[...]
    vn = vn_ref[...]
    ck = ck_ref[...]
    cv = cv_ref[...]
    bt = q.shape[0]
    for r in range(Q_PER_KV):
        qsel = q[:, r * LANES:(r + 1) * LANES]
        prod = (ck * qsel[:, None, :]).reshape(bt * WINDOW, KV_W)
        lg = _split_dot(prod, g128_ref[...]).reshape(bt, WINDOW, KV_W) + bias_ref[r]
        lgn = _split_dot(kn * qsel, g128_ref[...]) + bias0_ref[r]
        s = sink_ref[r]
        m = jnp.maximum(jnp.maximum(jnp.max(lg, axis=1), lgn), s)
        p = jnp.exp(lg - m[:, None, :])
        pn = jnp.exp(lgn - m)
        den = jnp.sum(p, axis=1) + pn + jnp.exp(s - m)
        o = (jnp.sum(p * cv, axis=1) + pn * vn) / den
        zb = zb_ref[:, r * LANES:(r + 1) * LANES]
        yb_ref[:, r * LANES:(r + 1) * LANES] = (o * _silu(zb)).astype(BF16)
    j = lax.broadcasted_iota(jnp.int32, ck.shape, 1)
    ko_ref[...] = jnp.where(j == WINDOW - 1, kn[:, None, :], pltpu.roll(ck, WINDOW - 1, axis=1))
    vo_ref[...] = jnp.where(j == WINDOW - 1, vn[:, None, :], pltpu.roll(cv, WINDOW - 1, axis=1))


def _attn_step(qkv, kv32, zb, cache_k, cache_v, bias_rows, bias0, sink_rows, g128, bt):
    nb = qkv.shape[0]
    cache_spec = pl.BlockSpec((bt, WINDOW, KV_W), lambda i: (i, 0, 0))
    return pl.pallas_call(
        _attn_step_kernel,
        grid=(nb // bt,),
        in_specs=[pl.BlockSpec((bt, W_B), lambda i: (i, 0)),
                  pl.BlockSpec((bt, KV_W), lambda i: (i, 0)),
                  pl.BlockSpec((bt, KV_W), lambda i: (i, 1)),
                  pl.BlockSpec((bt, W_B), lambda i: (i, 0)),
                  cache_spec, cache_spec,
                  pl.BlockSpec((Q_PER_KV, WINDOW, KV_W), lambda i: (0, 0, 0)),
                  pl.BlockSpec((Q_PER_KV, 1, KV_W), lambda i: (0, 0, 0)),
                  pl.BlockSpec((Q_PER_KV, 1, KV_W), lambda i: (0, 0, 0)),
                  pl.BlockSpec((KV_W, KV_W), lambda i: (0, 0))],
        out_specs=[pl.BlockSpec((bt, W_B), lambda i: (i, 0)), cache_spec, cache_spec],
        out_shape=[jax.ShapeDtypeStruct((nb, W_B), BF16),
                   jax.ShapeDtypeStruct(cache_k.shape, F32),
                   jax.ShapeDtypeStruct(cache_v.shape, F32)],
        name="swa_step",
    )(qkv, kv32, kv32, zb, cache_k, cache_v, bias_rows, bias0, sink_rows, g128)


def _merge_kernel(yt_ref, zat_ref, yb_ref, ga_ref, gb_ref, x_ref, gate_ref, woa_ref, wob_ref, wout_ref, out_ref):
    yat = (yt_ref[...] * _silu(zat_ref[...])).astype(BF16)
    pa = _dot_tn(yat, woa_ref[...])
    pb = _dot(yb_ref[...], wob_ref[...])
    merged = jax.nn.sigmoid(ga_ref[...]) * pa + jax.nn.sigmoid(gb_ref[...]) * pb
    out_ref[...] = x_ref[...] + gate_ref[...] * _dot(merged.astype(BF16), wout_ref[...])


def _merge(yt, zat, yb, gates, x, gate, woa_bf, wob_bf, wout_bf, tm, rows_per_seq, per_row):
    m = x.shape[0]
    full = lambda c: pl.BlockSpec((tm, D_MODEL), lambda i: (i, c))
    return pl.pallas_call(
        _merge_kernel,
        grid=(m // tm,),
        in_specs=[_t_spec(W_A, tm, rows_per_seq), _t_spec(W_A, tm, rows_per_seq),
                  pl.BlockSpec((tm, W_B), lambda i: (i, 0)), full(0), full(1), full(0),
                  _mod_spec(per_row, tm, rows_per_seq),
                  pl.BlockSpec((W_A, D_MODEL), lambda i: (0, 0)),
                  pl.BlockSpec((W_B, D_MODEL), lambda i: (0, 0)),
                  pl.BlockSpec((D_MODEL, D_MODEL), lambda i: (0, 0))],
        out_specs=full(0),
        out_shape=jax.ShapeDtypeStruct((m, D_MODEL), F32),
        name="merge_out",
    )(yt, zat, yb, gates, gates, x, gate, woa_bf, wob_bf, wout_bf)


def _t5_bucket(dist):
    max_exact = N_BUCKETS // 2
    d = jnp.maximum(dist, 0)
    log_ratio = jnp.log(jnp.maximum(d, 1).astype(F32) / max_exact) / math.log(MAX_DISTANCE / max_exact)
    large = jnp.minimum(max_exact + (log_ratio * (N_BUCKETS - max_exact)).astype(jnp.int32), N_BUCKETS - 1)
    return jnp.where(d < max_exact, d, large)


def _block_ones(n, blk):
    i = np.arange(n) // blk
    return jnp.asarray((i[:, None] == i[None, :]).astype(np.float32), dtype=BF16)


def _rwkv_prompt(usht, lw, tm):
    nseq, _, t_len = usht.shape
    n = nseq * H_A
    zero_shift = jnp.zeros((nseq, SHIFT_W, 1), F32)
    outs = _prep(usht, zero_shift, lw["mu"], lw["w0"], lw["a0"], lw["lorat"], lw["k_k"], lw["k_a"], tm, False)
    seqs = _swap([a.reshape(n, HEAD_SIZE, t_len) for a in outs])
    tile = lambda a: jnp.tile(a.reshape(H_A, HEAD_SIZE).T, (1, nseq))
    y, pf = _scan(seqs, [tile(lw["r_k"]), tile(lw["lnx_g"]), tile(lw["lnx_b"])], None,
                  _pick((16, 8, 4, 2, 1), t_len))
    (yt,) = _swap([y])
    wkv = pf.reshape(HEAD_SIZE, HEAD_SIZE, nseq, H_A).transpose(2, 3, 1, 0)
    return yt.reshape(nseq, W_A, t_len), wkv


def _rwkv_sample(usht, shift0, state0, lw):
    nb = usht.shape[2]
    outs = _prep(usht, shift0.T[None], lw["mu"], lw["w0"], lw["a0"], lw["lorat"], lw["k_k"], lw["k_a"], nb, True)
    to_lanes = lambda a: a.reshape(H_A, HEAD_SIZE, nb).transpose(1, 0, 2).reshape(1, HEAD_SIZE, H_A * nb)
    tile = lambda a: jnp.repeat(a.reshape(H_A, HEAD_SIZE).T, nb, axis=1)
    p0 = state0.transpose(3, 2, 1, 0).reshape(HEAD_SIZE, HEAD_SIZE, H_A * nb)
    y, pf = _scan([to_lanes(a) for a in outs], [tile(lw["r_k"]), tile(lw["lnx_g"]), tile(lw["lnx_b"])], p0, 1)
    yt = y.reshape(HEAD_SIZE, H_A, nb).transpose(1, 0, 2).reshape(1, W_A, nb)
    wkv = pf.reshape(HEAD_SIZE, HEAD_SIZE, H_A, nb).transpose(3, 2, 1, 0)
    return yt, wkv


def kernel(x_prompt, x_sample, c_prompt, c_sample, state_wkv, state_shift, cache_k, cache_v, norm_g, w_ada, b_ada, w_in, mu_shift, w0, w_decay_up, a0, w_a_up, k_k, k_a, r_k, lnx_g, lnx_b, w_o_a, q_norm_g, k_norm_g, rel_bias, sinks, w_o_b, w_out):
    nbp, t_len, _ = x_prompt.shape
    nbs = x_sample.shape[0]
    depth = norm_g.shape[0]
    mp = nbp * t_len
    assert (nbp * H_A) % LANES == 0 and nbs % LANES == 0 and t_len % LANES == 0
    g512 = _block_ones(W_B, HEAD_DIM)
    g128 = _block_ones(KV_W, HEAD_DIM)

    gtab = rel_bias[_t5_bucket(2 * WINDOW - jnp.arange(BIAS_SPAN))].astype(F32).T
    bias_s = rel_bias[_t5_bucket(WINDOW - jnp.arange(WINDOW + 1))].astype(F32)
    pair = lambda a: jnp.concatenate([jnp.repeat(a[..., 0:Q_PER_KV, None], HEAD_DIM, axis=-1),
                                      jnp.repeat(a[..., Q_PER_KV:H_B, None], HEAD_DIM, axis=-1)], axis=-1)
    bias_rows = pair(bias_s[:WINDOW]).transpose(1, 0, 2)
    bias0 = pair(bias_s[WINDOW:]).transpose(1, 0, 2)

    c_all = jnp.concatenate([c_prompt, c_sample], axis=0)
    xp = x_prompt.reshape(mp, D_MODEL)
    xs = x_sample.reshape(nbs, D_MODEL)
    tm_p = _pick((256, 128), t_len)
    outs = {k: [] for k in ("wkv_p", "shift_p", "kw_p", "vw_p", "wkv_s", "shift_s", "kw_s", "vw_s")}
    for l in range(depth):
        wl = w_in[l]
        base = SHIFT_W
        z_a, q, kb, vb, z_b, gts = (wl[:, base:base + 512], wl[:, base + 512:base + 1024],
                                    wl[:, base + 1024:base + 1152], wl[:, base + 1152:base + 1280],
                                    wl[:, base + 1280:base + 1792], wl[:, base + 1792:])
        w_bf = jnp.concatenate([q[:, HEAD_PERM], kb, vb, z_b[:, HEAD_PERM], gts], axis=1).astype(BF16)
        wt_bf = jnp.concatenate([wl[:, :SHIFT_W], z_a], axis=1).T.astype(BF16)
        zeros = jnp.zeros((LORA, W_A), F32)
        lorat = jnp.concatenate([jnp.concatenate([w_decay_up[l], zeros], axis=1),
                                 jnp.concatenate([zeros, w_a_up[l]], axis=1)], axis=0).T.astype(BF16)
        lw = dict(mu=mu_shift[l], w0=w0[l], a0=a0[l], lorat=lorat, k_k=k_k[l], k_a=k_a[l], r_k=r_k[l],
                  lnx_g=lnx_g[l], lnx_b=lnx_b[l])
        woa_bf, wout_bf = w_o_a[l].astype(BF16), w_out[l].astype(BF16)
        wob_bf = w_o_b[l][HEAD_PERM, :].astype(BF16)
        q_gain = jnp.tile(q_norm_g[l], H_B).reshape(1, W_B)
        k_gain = jnp.tile(k_norm_g[l], KV_HEADS).reshape(1, KV_W)
        sink_col = jnp.repeat(sinks[l].reshape(KV_HEADS, Q_PER_KV), WINDOW, axis=1).reshape(
            KV_HEADS, Q_PER_KV * WINDOW, 1)
        sink_rows = pair(sinks[l].reshape(1, H_B)).transpose(1, 0, 2)

        mod = _ada(c_all, w_ada[l].astype(BF16), b_ada[l])
        shift, scale, gate = mod[:, :D_MODEL], mod[:, D_MODEL:2 * D_MODEL], mod[:, 2 * D_MODEL:]

        sp, scp, gp = (a[:nbp].reshape(nbp, 1, D_MODEL) for a in (shift, scale, gate))
        usht, zat, qkv, kv32, zb, gts_p = _in_proj(xp, norm_g[l], scp, sp, w_bf, wt_bf, q_gain, k_gain, g512, g128,
                                                   tm_p, t_len, False)
        yt, wkv = _rwkv_prompt(usht, lw, tm_p)
        yb = _attn(qkv, zb, nbp, gtab, sink_col)
        xp = _merge(yt, zat, yb, gts_p, xp, gp, woa_bf, wob_bf, wout_bf, tm_p, t_len, False)
        win = kv32.reshape(nbp, t_len, 2, KV_HEADS, HEAD_DIM)[:, t_len - WINDOW:]
        outs["wkv_p"].append(wkv)
        outs["shift_p"].append(usht[:, :, t_len - 1])
        outs["kw_p"].append(win[:, :, 0])
        outs["vw_p"].append(win[:, :, 1])

        ss, scs, gs = shift[nbp:], scale[nbp:], gate[nbp:]
        usht, zat, qkv, kv32, zb, gts_s = _in_proj(xs, norm_g[l], scs, ss, w_bf, wt_bf, q_gain, k_gain, g512, g128,
                                                   nbs, nbs, True)
        yt, wkv = _rwkv_sample(usht, state_shift[l], state_wkv[l], lw)
        yb, kwin, vwin = _attn_step(qkv, kv32, zb, cache_k[l].reshape(nbs, WINDOW, KV_W),
                                    cache_v[l].reshape(nbs, WINDOW, KV_W),
                                    bias_rows, bias0, sink_rows, g128, _pick((16, 8), nbs))
        xs = _merge(yt, zat, yb, gts_s, xs, gs, woa_bf, wob_bf, wout_bf, nbs, nbs, True)
        outs["wkv_s"].append(wkv)
        outs["shift_s"].append(usht[0].T)
        outs["kw_s"].append(kwin.reshape(nbs, WINDOW, KV_HEADS, HEAD_DIM))
        outs["vw_s"].append(vwin.reshape(nbs, WINDOW, KV_HEADS, HEAD_DIM))

    st = lambda k: jnp.stack(outs[k])
    return (xp.reshape(nbp, t_len, D_MODEL), xs.reshape(nbs, 1, D_MODEL),
            st("wkv_p"), st("shift_p"), st("kw_p"), st("vw_p"),
            st("wkv_s"), st("shift_s"), st("kw_s"), st("vw_s"))
```

```python
import functools
import math

import numpy as np
import jax
import jax.numpy as jnp
from jax import lax
from jax.experimental import pallas as pl
from jax.experimental.pallas import tpu as pltpu

F32 = jnp.float32
BF16 = jnp.bfloat16

D_MODEL = 1024
W_A = 512
HEAD_SIZE = 64
H_A = W_A // HEAD_SIZE
LORA = 64
SHIFT_W = 3 * W_A + 2 * LORA
W_B = 512
HEAD_DIM = 64
H_B = W_B // HEAD_DIM
KV_HEADS = 2
Q_PER_KV = H_B // KV_HEADS
KV_W = KV_HEADS * HEAD_DIM
WINDOW = 128
N_BUCKETS = 32
MAX_DISTANCE = 128
NORM_EPS = 1e-6
GN_EPS = 64e-5
NEG_INF = -1e30
QKV_W = W_B + 2 * KV_W
GATE_W = 2 * D_MODEL
ROW_COLS = QKV_W + W_B + GATE_W
T_COLS = SHIFT_W + W_A
LANES = 128
SUBLANES = 8
def _head_pairs(x, axis):
    s = x.shape
    x = x.reshape(s[:axis] + (KV_HEADS, Q_PER_KV, HEAD_DIM) + s[axis + 1:])
    return jnp.swapaxes(x, axis, axis + 1).reshape(s)


def _chan_major(x, axis, inverse=False):
    s = x.shape
    split = (HEAD_SIZE, H_A) if inverse else (H_A, HEAD_SIZE)
    x = x.reshape(s[:axis] + split + s[axis + 1:])
    return jnp.swapaxes(x, axis, axis + 1).reshape(s)


def _shift_major(x, axis, inverse=False):
    parts = [lax.slice_in_dim(x, i * W_A, (i + 1) * W_A, axis=axis) for i in range(3)]
    tail = lax.slice_in_dim(x, 3 * W_A, SHIFT_W, axis=axis)
    return jnp.concatenate([_chan_major(p, axis, inverse) for p in parts] + [tail], axis=axis)
ATTN_BLOCKS = 4
BIAS_SPAN = 3 * WINDOW
KEY_UNROLL = 32
SWAP_CHANNELS = 8


def _dot(a, b):
    return jnp.dot(a, b, preferred_element_type=F32)


def _dot_nt(a, b):
    return lax.dot_general(a, b, (((1,), (1,)), ((), ())), preferred_element_type=F32)


def _dot_tn(a, b):
    return lax.dot_general(a, b, (((0,), (0,)), ((), ())), preferred_element_type=F32)


def _split_dot(x, g):
    hi = x.astype(BF16)
    lo = (x - hi.astype(F32)).astype(BF16)
    return _dot(hi, g) + _dot(lo, g)


def _silu(x):
    return x * jax.nn.sigmoid(x)


def _pick(cands, n):
    for c in cands:
        if n % c == 0:
            return c
    return n


def _ada_kernel(c_ref, w_ref, b_ref, o_ref):
    c = c_ref[...]
    o_ref[...] = _dot(_silu(c).astype(BF16), w_ref[...]) + b_ref[...]


def _ada(c, w_bf, b):
    nb, d = c.shape
    n = w_bf.shape[1]
    tn = 1024
    return pl.pallas_call(
        _ada_kernel,
        grid=(n // tn,),
        in_specs=[pl.BlockSpec((nb, d), lambda j: (0, 0)),
                  pl.BlockSpec((d, tn), lambda j: (0, j)),
                  pl.BlockSpec((1, tn), lambda j: (0, j))],
        out_specs=pl.BlockSpec((nb, tn), lambda j: (0, j)),
        out_shape=jax.ShapeDtypeStruct((nb, n), F32),
        name="ada",
    )(c, w_bf, b.reshape(1, n))


def _in_proj_kernel(x_ref, g_ref, scale_ref, shift_ref, w_ref, wt_ref, qg_ref, kg_ref, g512_ref, g128_ref,
                    usht_ref, zat_ref, qkv_ref, kv32_ref, zb_ref, gate_ref):
    x = x_ref[...]
    ms = jnp.mean(x * x, axis=-1, keepdims=True)
    h = (x * lax.rsqrt(ms + NORM_EPS)) * g_ref[...]
    h = h * (1.0 + scale_ref[...]) + shift_ref[...]
    hb = h.astype(BF16)
    usht_ref[...] = _dot_nt(wt_ref[0:SHIFT_W, :], hb)
    zat_ref[...] = _dot_nt(wt_ref[SHIFT_W:T_COLS, :], hb)
    qkv = _dot(hb, w_ref[:, 0:QKV_W])
    q, k, v = qkv[:, 0:W_B], qkv[:, W_B:W_B + KV_W], qkv[:, W_B + KV_W:QKV_W]
    qn = q * lax.rsqrt(_split_dot(q * q, g512_ref[...]) * (1.0 / HEAD_DIM) + NORM_EPS) * qg_ref[...]
    kn = k * lax.rsqrt(_split_dot(k * k, g128_ref[...]) * (1.0 / HEAD_DIM) + NORM_EPS) * kg_ref[...]
    qkv_ref[:, 0:W_B] = (qn * (HEAD_DIM ** -0.5)).astype(BF16)
    qkv_ref[:, W_B:W_B + KV_W] = kn.astype(BF16)
    qkv_ref[:, W_B + KV_W:QKV_W] = v.astype(BF16)
    kv32_ref[:, 0:KV_W] = kn
    kv32_ref[:, KV_W:2 * KV_W] = v
    zb_ref[...] = _dot(hb, w_ref[:, QKV_W:QKV_W + W_B])
    gate_ref[...] = _dot(hb, w_ref[:, QKV_W + W_B:ROW_COLS])


def _mod_spec(per_row, tm, rows_per_seq):
    if per_row:
        return pl.BlockSpec((tm, D_MODEL), lambda i: (i, 0))
    return pl.BlockSpec((None, 1, D_MODEL), lambda i: ((i * tm) // rows_per_seq, 0, 0))


def _t_spec(cols, tm, rows_per_seq):
    per_seq = rows_per_seq // tm
    return pl.BlockSpec((None, cols, tm), lambda i: (i // per_seq, 0, i % per_seq))


def _in_proj(x, norm_g, scale, shift, w_bf, wt_bf, q_gain, k_gain, g512, g128, tm, rows_per_seq, per_row):
    m = x.shape[0]
    nseq = m // rows_per_seq
    const = lambda r, c: pl.BlockSpec((r, c), lambda i: (0, 0))
    widths = (QKV_W, 2 * KV_W, W_B, GATE_W)
    dtypes = (BF16, F32, F32, F32)
    return pl.pallas_call(
        _in_proj_kernel,
        grid=(m // tm,),
        in_specs=[pl.BlockSpec((tm, D_MODEL), lambda i: (i, 0)),
                  const(1, D_MODEL),
                  _mod_spec(per_row, tm, rows_per_seq),
                  _mod_spec(per_row, tm, rows_per_seq),
                  const(D_MODEL, ROW_COLS), const(T_COLS, D_MODEL),
                  const(1, W_B), const(1, KV_W), const(W_B, W_B), const(KV_W, KV_W)],
        out_specs=[_t_spec(SHIFT_W, tm, rows_per_seq), _t_spec(W_A, tm, rows_per_seq)]
                  + [pl.BlockSpec((tm, w), lambda i: (i, 0)) for w in widths],
        out_shape=[jax.ShapeDtypeStruct((nseq, SHIFT_W, rows_per_seq), F32),
                   jax.ShapeDtypeStruct((nseq, W_A, rows_per_seq), F32)]
                  + [jax.ShapeDtypeStruct((m, w), dt) for w, dt in zip(widths, dtypes)],
        name="in_proj",
    )(x, norm_g.reshape(1, D_MODEL), scale, shift, w_bf, wt_bf, q_gain, k_gain, g512, g128)


def _prep_kernel(u_ref, prev_ref, shift0_ref, mu_ref, w0_ref, a0_ref, lora_ref, kk_ref, ka_ref, rk_ref,
                 r_o, w_o, k_o, v_o, kk_o, b_o, coef_o, *, per_token_state):
    tm = u_ref.shape[1]
    heads = lambda x: x.reshape(HEAD_SIZE, H_A, LANES)
    lane = lax.broadcasted_iota(jnp.int32, (SHIFT_W, LANES), 1)
    row = lax.broadcasted_iota(jnp.int32, (2 * LORA, LANES), 0)
    if not per_token_state:
        before = jnp.where(pl.program_id(1) == 0, shift0_ref[...], prev_ref[...])
        rolled_before = pltpu.roll(before, 1, axis=1)
    for ci in range(tm // LANES):
        cols = slice(ci * LANES, (ci + 1) * LANES)
        u = u_ref[:, cols]
        if per_token_state:
            u_prev = shift0_ref[:, cols]
        else:
            rolled = pltpu.roll(u, 1, axis=1)
            u_prev = jnp.where(lane == 0, rolled_before, rolled)
            rolled_before = rolled
        xs = u + (u_prev - u) * mu_ref[...]
        r = xs[0:W_A]
        k = xs[W_A:2 * W_A]
        v = xs[2 * W_A:3 * W_A]
        tail = xs[3 * W_A:SHIFT_W]
        lora_in = jnp.where(row < LORA, jnp.tanh(tail), tail).astype(BF16)
        up = _dot(lora_ref[...], lora_in)
        neg = -(w0_ref[...] + up[0:W_A])
        softplus = jnp.maximum(neg, 0.0) + jnp.log(1.0 + jnp.exp(-jnp.abs(neg)))
        w_log = -softplus - 0.5
        decay = jnp.exp(-jnp.exp(w_log))
        a = jax.nn.sigmoid(a0_ref[...] + up[W_A:2 * W_A])
        kk = heads(k * kk_ref[...])
        norm = jnp.sqrt(jnp.sum(kk * kk, axis=0, keepdims=True))
        kk = kk / jnp.maximum(norm, 1e-12)
        k_mod = k * (1.0 + (a - 1.0) * ka_ref[...])
        r_o[:, :, cols] = heads(r)
        w_o[:, :, cols] = heads(decay)
        k_o[:, :, cols] = heads(k_mod)
        v_o[:, :, cols] = heads(v)
        kk_o[:, :, cols] = kk
        b_o[:, :, cols] = kk * heads(a)
        coef_o[:, cols] = jnp.sum(heads(r * k_mod * rk_ref[...]), axis=0)


def _prep(usht, shift0t, mu, w0, a0, lorat_bf, k_k, k_a, r_k, tm, per_token_state):
    nseq, _, t_len = usht.shape
    col = lambda a: jnp.broadcast_to(a.reshape(-1, 1), (a.size, LANES))
    cvec = lambda n: pl.BlockSpec((n, LANES), lambda s, j: (0, 0))
    blk = lambda cols: pl.BlockSpec((None, cols, tm), lambda s, j: (s, 0, j))
    if per_token_state:
        prev_spec = pl.BlockSpec((None, SHIFT_W, LANES), lambda s, j: (0, 0, 0))
        shift0_spec = blk(SHIFT_W)
    else:
        prev_spec = pl.BlockSpec((None, SHIFT_W, LANES),
                                 lambda s, j: (s, 0, jnp.maximum(j * (tm // LANES) - 1, 0)))
        shift0_spec = pl.BlockSpec((None, SHIFT_W, LANES), lambda s, j: (s, 0, 0))
    chan = pl.BlockSpec((HEAD_SIZE, H_A, tm), lambda s, j: (0, s, j))
    return pl.pallas_call(
        functools.partial(_prep_kernel, per_token_state=per_token_state),
        grid=(nseq, t_len // tm),
        in_specs=[blk(SHIFT_W), prev_spec, shift0_spec, cvec(SHIFT_W), cvec(W_A), cvec(W_A),
                  pl.BlockSpec((2 * W_A, 2 * LORA), lambda s, j: (0, 0)), cvec(W_A), cvec(W_A), cvec(W_A)],
        out_specs=[chan] * 6 + [pl.BlockSpec((H_A, tm), lambda s, j: (s, j))],
        out_shape=[jax.ShapeDtypeStruct((HEAD_SIZE, nseq * H_A, t_len), F32)] * 6
                  + [jax.ShapeDtypeStruct((nseq * H_A, t_len), F32)],
        name="rwkv_prep",
    )(usht, usht, shift0t, col(mu), col(w0), col(a0), lorat_bf, col(k_k), col(k_a), col(r_k))


def _swap_kernel(*refs):
    half = len(refs) // 2
    for x_ref, o_ref in zip(refs[:half], refs[half:]):
        for c in range(SWAP_CHANNELS):
            o_ref[c] = x_ref[c].T


def _swap(arrays):
    c_len, a_len, b_len = arrays[0].shape
    blk = (SWAP_CHANNELS, LANES, LANES)
    return pl.pallas_call(
        _swap_kernel,
        grid=(b_len // LANES, a_len // LANES, c_len // SWAP_CHANNELS),
        in_specs=[pl.BlockSpec(blk, lambda i, k, j: (j, k, i))] * len(arrays),
        out_specs=[pl.BlockSpec(blk, lambda i, k, j: (j, i, k))] * len(arrays),
        out_shape=[jax.ShapeDtypeStruct((c_len, b_len, a_len), F32)] * len(arrays),
        name="layout_swap",
    )(*arrays)


def _scan_kernel(*refs, tt, zero_init):
    if zero_init:
        (r_ref, w_ref, k_ref, v_ref, kk_ref, b_ref, coef_ref, lng_ref, lnb_ref,
         y_ref, pf_ref, p_ref, vs_ref, ys_ref) = refs
    else:
        (r_ref, w_ref, k_ref, v_ref, kk_ref, b_ref, coef_ref, lng_ref, lnb_ref, p0_ref,
         y_ref, pf_ref, p_ref, vs_ref, ys_ref) = refs
    ti = pl.program_id(1)
    vs_ref[...] = jnp.swapaxes(v_ref[...], 0, 1)

    @pl.when(ti == 0)
    def _():
        if zero_init:
            p_ref[...] = jnp.zeros(p_ref.shape, F32)
        else:
            p_ref[...] = p0_ref[...]

    def step(t, carry):
        zero = jnp.zeros((HEAD_SIZE, LANES), F32)
        row = pl.ds(t, 1)

        def reduce_keys(kb, part):
            part = list(part)
            for u in range(KEY_UNROLL):
                kc = kb * KEY_UNROLL + u
                part[u % 2] = part[u % 2] + p_ref[kc] * kk_ref[kc, row, :]
            return tuple(part)

        part = lax.fori_loop(0, HEAD_SIZE // KEY_UNROLL, reduce_keys, (zero, zero))
        sa = -(part[0] + part[1])
        vt = vs_ref[t]

        def update_keys(kb, out):
            for u in range(KEY_UNROLL):
                kc = kb * KEY_UNROLL + u
                p = p_ref[kc] * w_ref[kc, row, :] + sa * b_ref[kc, row, :] + vt * k_ref[kc, row, :]
                p_ref[kc] = p
                out = out + p * r_ref[kc, row, :]
            return out

        o = lax.fori_loop(0, HEAD_SIZE // KEY_UNROLL, update_keys, zero)
        mean = jnp.sum(o, axis=0, keepdims=True) * (1.0 / HEAD_SIZE)
        c = o - mean
        var = jnp.sum(c * c, axis=0, keepdims=True) * (1.0 / HEAD_SIZE)
        ys_ref[t] = c * lax.rsqrt(var + GN_EPS) * lng_ref[...] + lnb_ref[...] + coef_ref[row, :] * vt
        return carry

    lax.fori_loop(0, tt, step, 0)
    y_ref[...] = jnp.swapaxes(ys_ref[...], 0, 1)

    @pl.when(ti == pl.num_programs(1) - 1)
    def _():
        pf_ref[...] = p_ref[...]


def _scan(seqs, coef, tiles, p0, tt):
    _, t_len, n = seqs[0].shape
    seq_spec = pl.BlockSpec((HEAD_SIZE, tt, LANES), lambda j, i: (0, i, j))
    coef_spec = pl.BlockSpec((tt, LANES), lambda j, i: (i, j))
    tile_spec = pl.BlockSpec((HEAD_SIZE, LANES), lambda j, i: (0, j))
    state_spec = pl.BlockSpec((HEAD_SIZE, HEAD_SIZE, LANES), lambda j, i: (0, 0, j))
    zero_init = p0 is None
    args = list(seqs) + [coef] + list(tiles) + ([] if zero_init else [p0])
    return pl.pallas_call(
        functools.partial(_scan_kernel, tt=tt, zero_init=zero_init),
        grid=(n // LANES, t_len // tt),
        in_specs=[seq_spec] * 6 + [coef_spec] + [tile_spec] * 2 + ([] if zero_init else [state_spec]),
        out_specs=[seq_spec, state_spec],
        out_shape=[jax.ShapeDtypeStruct((HEAD_SIZE, t_len, n), F32),
                   jax.ShapeDtypeStruct((HEAD_SIZE, HEAD_SIZE, n), F32)],
        scratch_shapes=[pltpu.VMEM((HEAD_SIZE, HEAD_SIZE, LANES), F32),
                        pltpu.VMEM((tt, HEAD_SIZE, LANES), F32), pltpu.VMEM((tt, HEAD_SIZE, LANES), F32)],
        compiler_params=pltpu.CompilerParams(dimension_semantics=("arbitrary", "arbitrary")),
        name="rwkv_scan",
    )(*args)


def _attn_kernel(cur_ref, prev_ref, zb_ref, gtab_ref, sink_ref, yb_ref, bias_ref, *, blocks):
    s = pl.program_id(1)
    rows = Q_PER_KV * WINDOW

    @pl.when((pl.program_id(0) == 0) & (s == 0))
    def _():
        qi = lax.broadcasted_iota(jnp.int32, (WINDOW, 2 * WINDOW), 0)
        kj = lax.broadcasted_iota(jnp.int32, (WINDOW, 2 * WINDOW), 1)
        dist = qi + WINDOW - kj
        band = (dist >= 0) & (dist <= WINDOW)
        for g in range(KV_HEADS):
            for j in range(Q_PER_KV):
                h = g * Q_PER_KV + j
                row = jnp.broadcast_to(gtab_ref[h:h + 1, :], (WINDOW, BIAS_SPAN))
                toe = pltpu.roll(row, 0, 1, stride=1, stride_axis=0)[:, WINDOW:BIAS_SPAN]
                bias_ref[g, j * WINDOW:(j + 1) * WINDOW, :] = jnp.where(band, toe, NEG_INF)

    cur = cur_ref[...]
    prev = prev_ref[...]
    k_all = jnp.concatenate([prev[:, W_B:W_B + KV_W], cur[:, W_B:W_B + KV_W]], axis=0)
    v_all = jnp.concatenate([prev[:, W_B + KV_W:QKV_W], cur[:, W_B + KV_W:QKV_W]], axis=0)
    lane = lax.broadcasted_iota(jnp.int32, k_all.shape, 1)
    zero = jnp.zeros_like(k_all)
    k_g = [jnp.where(lane < HEAD_DIM, k_all, zero), jnp.where(lane >= HEAD_DIM, k_all, zero)]
    v_g = [jnp.where(lane < HEAD_DIM, v_all, zero), jnp.where(lane >= HEAD_DIM, v_all, zero)]
    ones = jnp.ones((2 * WINDOW, LANES), BF16)
    kj = lax.broadcasted_iota(jnp.int32, (rows, 2 * WINDOW), 1)
    for i in range(blocks):
        r0 = i * WINDOW
        q4 = jnp.concatenate([cur[r0:r0 + WINDOW, j * LANES:(j + 1) * LANES] for j in range(Q_PER_KV)], axis=0)
        acc = None
        for g in range(KV_HEADS):
            logits = _dot_nt(q4, k_g[g][r0:r0 + 2 * WINDOW]) + bias_ref[g]
            if i == 0:
                logits = jnp.where((kj >= WINDOW) | (s > 0), logits, NEG_INF)
            sink = sink_ref[g]
            m = jnp.maximum(jnp.max(logits, axis=-1, keepdims=True), sink)
            p = jnp.exp(logits - m).astype(BF16)
            den = _dot(p, ones) + jnp.exp(sink - m)
            o = _dot(p, v_g[g][r0:r0 + 2 * WINDOW]) / den
            acc = o if acc is None else acc + o
        for j in range(Q_PER_KV):
            zb = zb_ref[r0:r0 + WINDOW, j * LANES:(j + 1) * LANES]
            yb_ref[r0:r0 + WINDOW, j * LANES:(j + 1) * LANES] = (
                acc[j * WINDOW:(j + 1) * WINDOW] * _silu(zb)).astype(BF16)


def _attn(qkv, zb, nb_seq, gtab, sink_col):
    m = qkv.shape[0]
    nblk = m // (nb_seq * WINDOW)
    blocks = _pick((ATTN_BLOCKS, 2, 1), nblk)
    steps = nblk // blocks
    rows = Q_PER_KV * WINDOW
    return pl.pallas_call(
        functools.partial(_attn_kernel, blocks=blocks),
        grid=(nb_seq, steps),
        in_specs=[pl.BlockSpec((blocks * WINDOW, QKV_W), lambda b, s: (b * steps + s, 0)),
                  pl.BlockSpec((WINDOW, QKV_W), lambda b, s: (jnp.maximum((b * steps + s) * blocks - 1, 0), 0)),
                  pl.BlockSpec((blocks * WINDOW, W_B), lambda b, s: (b * steps + s, 0)),
                  pl.BlockSpec((H_B, BIAS_SPAN), lambda b, s: (0, 0)),
                  pl.BlockSpec((KV_HEADS, rows, 1), lambda b, s: (0, 0, 0))],
        out_specs=pl.BlockSpec((blocks * WINDOW, W_B), lambda b, s: (b * steps + s, 0)),
        out_shape=jax.ShapeDtypeStruct((m, W_B), BF16),
        scratch_shapes=[pltpu.VMEM((KV_HEADS, rows, 2 * WINDOW), F32)],
        compiler_params=pltpu.CompilerParams(dimension_semantics=("arbitrary", "arbitrary")),
        name="swa_prompt",
    )(qkv, qkv, zb, gtab, sink_col)


def _attn_step_kernel(q_ref, kn_ref, vn_ref, zb_ref, ck_ref, cv_ref, bias_ref, bias0_ref, sink_ref, g128_ref,
                      yb_ref, ko_ref, vo_ref):
    q = q_ref[...].astype(F32)
    kn = kn_ref[...]
    vn = vn_ref[...]
    ck = ck_ref[...]
    cv = cv_ref[...]
    bt = q.shape[0]
    for r in range(Q_PER_KV):
        qsel = q[:, r * LANES:(r + 1) * LANES]
        prod = (ck * qsel[:, None, :]).reshape(bt * WINDOW, KV_W)
        lg = _split_dot(prod, g128_ref[...]).reshape(bt, WINDOW, KV_W) + bias_ref[r]
        lgn = _split_dot(kn * qsel, g128_ref[...]) + bias0_ref[r]
        s = sink_ref[r]
        m = jnp.maximum(jnp.maximum(jnp.max(lg, axis=1), lgn), s)
        p = jnp.exp(lg - m[:, None, :])
        pn = jnp.exp(lgn - m)
        den = jnp.sum(p, axis=1) + pn + jnp.exp(s - m)
        o = (jnp.sum(p * cv, axis=1) + pn * vn) / den
        zb = zb_ref[:, r * LANES:(r + 1) * LANES]
        yb_ref[:, r * LANES:(r + 1) * LANES] = (o * _silu(zb)).astype(BF16)
    j = lax.broadcasted_iota(jnp.int32, ck.shape, 1)
    ko_ref[...] = jnp.where(j == WINDOW - 1, kn[:, None, :], pltpu.roll(ck, WINDOW - 1, axis=1))
    vo_ref[...] = jnp.where(j == WINDOW - 1, vn[:, None, :], pltpu.roll(cv, WINDOW - 1, axis=1))


def _attn_step(qkv, kv32, zb, cache_k, cache_v, bias_rows, bias0, sink_rows, g128, bt):
    nb = qkv.shape[0]
    cache_spec = pl.BlockSpec((bt, WINDOW, KV_W), lambda i: (i, 0, 0))
    return pl.pallas_call(
        _attn_step_kernel,
        grid=(nb // bt,),
        in_specs=[pl.BlockSpec((bt, W_B), lambda i: (i, 0)),
                  pl.BlockSpec((bt, KV_W), lambda i: (i, 0)),
                  pl.BlockSpec((bt, KV_W), lambda i: (i, 1)),
                  pl.BlockSpec((bt, W_B), lambda i: (i, 0)),
                  cache_spec, cache_spec,
                  pl.BlockSpec((Q_PER_KV, WINDOW, KV_W), lambda i: (0, 0, 0)),
                  pl.BlockSpec((Q_PER_KV, 1, KV_W), lambda i: (0, 0, 0)),
                  pl.BlockSpec((Q_PER_KV, 1, KV_W), lambda i: (0, 0, 0)),
                  pl.BlockSpec((KV_W, KV_W), lambda i: (0, 0))],
        out_specs=[pl.BlockSpec((bt, W_B), lambda i: (i, 0)), cache_spec, cache_spec],
        out_shape=[jax.ShapeDtypeStruct((nb, W_B), BF16),
                   jax.ShapeDtypeStruct(cache_k.shape, F32),
                   jax.ShapeDtypeStruct(cache_v.shape, F32)],
        name="swa_step",
    )(qkv, kv32, kv32, zb, cache_k, cache_v, bias_rows, bias0, sink_rows, g128)


def _merge_kernel(yt_ref, zat_ref, yb_ref, ga_ref, gb_ref, x_ref, gate_ref, woa_ref, wob_ref, wout_ref, out_ref):
    yt = yt_ref[...]
    yat = (yt.reshape(W_A, yt.shape[2]) * _silu(zat_ref[...])).astype(BF16)
    pa = _dot_tn(yat, woa_ref[...])
    pb = _dot(yb_ref[...], wob_ref[...])
    merged = jax.nn.sigmoid(ga_ref[...]) * pa + jax.nn.sigmoid(gb_ref[...]) * pb
    out_ref[...] = x_ref[...] + gate_ref[...] * _dot(merged.astype(BF16), wout_ref[...])


def _merge(yt, zat, yb, gates, x, gate, woa_bf, wob_bf, wout_bf, tm, rows_per_seq, per_row):
    m = x.shape[0]
    full = lambda c: pl.BlockSpec((tm, D_MODEL), lambda i: (i, c))
    per_seq = rows_per_seq // tm
    y_spec = pl.BlockSpec((HEAD_SIZE, H_A, tm), lambda i: (0, i // per_seq, i % per_seq))
    return pl.pallas_call(
        _merge_kernel,
        grid=(m // tm,),
        in_specs=[y_spec, _t_spec(W_A, tm, rows_per_seq),
                  pl.BlockSpec((tm, W_B), lambda i: (i, 0)), full(0), full(1), full(0),
                  _mod_spec(per_row, tm, rows_per_seq),
                  pl.BlockSpec((W_A, D_MODEL), lambda i: (0, 0)),
                  pl.BlockSpec((W_B, D_MODEL), lambda i: (0, 0)),
                  pl.BlockSpec((D_MODEL, D_MODEL), lambda i: (0, 0))],
        out_specs=full(0),
        out_shape=jax.ShapeDtypeStruct((m, D_MODEL), F32),
        name="merge_out",
    )(yt, zat, yb, gates, gates, x, gate, woa_bf, wob_bf, wout_bf)


def _t5_bucket(dist):
    max_exact = N_BUCKETS // 2
    d = jnp.maximum(dist, 0)
    log_ratio = jnp.log(jnp.maximum(d, 1).astype(F32) / max_exact) / math.log(MAX_DISTANCE / max_exact)
    large = jnp.minimum(max_exact + (log_ratio * (N_BUCKETS - max_exact)).astype(jnp.int32), N_BUCKETS - 1)
    return jnp.where(d < max_exact, d, large)


def _block_ones(n, blk):
    i = np.arange(n) // blk
    return jnp.asarray((i[:, None] == i[None, :]).astype(np.float32), dtype=BF16)


def _rwkv_prompt(usht, lw, tm):
    nseq, _, t_len = usht.shape
    zero_shift = jnp.zeros((nseq, SHIFT_W, LANES), F32)
    outs = _prep(usht, zero_shift, lw["mu"], lw["w0"], lw["a0"], lw["lorat"], lw["k_k"], lw["k_a"], lw["r_k"],
                 tm, False)
    seqs = _swap(outs[:6])
    tile = lambda a: jnp.tile(a.reshape(H_A, HEAD_SIZE).T, (1, nseq))
    y, pf = _scan(seqs, outs[6].T, [tile(lw["lnx_g"]), tile(lw["lnx_b"])], None, _pick((16, 8), t_len))
    (yt,) = _swap([y])
    wkv = pf.reshape(HEAD_SIZE, HEAD_SIZE, nseq, H_A).transpose(2, 3, 1, 0)
    return yt, wkv


def _rwkv_sample(usht, shift0, state0, lw):
    nb = usht.shape[2]
    n = H_A * nb
    outs = _prep(usht, _shift_major(shift0, 1).T[None], lw["mu"], lw["w0"], lw["a0"], lw["lorat"], lw["k_k"],
                 lw["k_a"], lw["r_k"], nb, True)
    tile = lambda a: jnp.repeat(a.reshape(H_A, HEAD_SIZE).T, nb, axis=1)
    p0 = state0.transpose(3, 2, 1, 0).reshape(HEAD_SIZE, HEAD_SIZE, n)
    y, pf = _scan([a.reshape(HEAD_SIZE, 1, n) for a in outs[:6]], outs[6].reshape(1, n),
                  [tile(lw["lnx_g"]), tile(lw["lnx_b"])], p0, 1)
    wkv = pf.reshape(HEAD_SIZE, HEAD_SIZE, H_A, nb).transpose(3, 2, 1, 0)
    return y.reshape(HEAD_SIZE, H_A, nb), wkv


def kernel(x_prompt, x_sample, c_prompt, c_sample, state_wkv, state_shift, cache_k, cache_v, norm_g, w_ada, b_ada, w_in, mu_shift, w0, w_decay_up, a0, w_a_up, k_k, k_a, r_k, lnx_g, lnx_b, w_o_a, q_norm_g, k_norm_g, rel_bias, sinks, w_o_b, w_out):
    nbp, t_len, _ = x_prompt.shape
    nbs = x_sample.shape[0]
    depth = norm_g.shape[0]
    mp = nbp * t_len
    assert (nbp * H_A) % LANES == 0 and nbs % LANES == 0 and t_len % LANES == 0
    g512 = _block_ones(W_B, HEAD_DIM)
    g128 = _block_ones(KV_W, HEAD_DIM)

    gtab = rel_bias[_t5_bucket(2 * WINDOW - jnp.arange(BIAS_SPAN))].astype(F32).T
    bias_s = rel_bias[_t5_bucket(WINDOW - jnp.arange(WINDOW + 1))].astype(F32)
    pair = lambda a: jnp.concatenate([jnp.repeat(a[..., 0:Q_PER_KV, None], HEAD_DIM, axis=-1),
                                      jnp.repeat(a[..., Q_PER_KV:H_B, None], HEAD_DIM, axis=-1)], axis=-1)
    bias_rows = pair(bias_s[:WINDOW]).transpose(1, 0, 2)
    bias0 = pair(bias_s[WINDOW:]).transpose(1, 0, 2)

    c_all = jnp.concatenate([c_prompt, c_sample], axis=0)
    xp = x_prompt.reshape(mp, D_MODEL)
    xs = x_sample.reshape(nbs, D_MODEL)
    tm_p = _pick((256, 128), t_len)
    tm_in = _pick((512, 256, 128), t_len)
    outs = {k: [] for k in ("wkv_p", "shift_p", "kw_p", "vw_p", "wkv_s", "shift_s", "kw_s", "vw_s")}
    for l in range(depth):
        wl = w_in[l]
        base = SHIFT_W
        z_a, q, kb, vb, z_b, gts = (wl[:, base:base + 512], wl[:, base + 512:base + 1024],
                                    wl[:, base + 1024:base + 1152], wl[:, base + 1152:base + 1280],
                                    wl[:, base + 1280:base + 1792], wl[:, base + 1792:])
        w_bf = jnp.concatenate([_head_pairs(q, 1), kb, vb, _head_pairs(z_b, 1), gts], axis=1).astype(BF16)
        wt_bf = jnp.concatenate([_shift_major(wl[:, :SHIFT_W], 1), _chan_major(z_a, 1)], axis=1).T.astype(BF16)
        zeros = jnp.zeros((LORA, W_A), F32)
        lorat = jnp.concatenate([jnp.concatenate([_chan_major(w_decay_up[l], 1), zeros], axis=1),
                                 jnp.concatenate([zeros, _chan_major(w_a_up[l], 1)], axis=1)],
                                axis=0).T.astype(BF16)
        cm = lambda a: _chan_major(a.reshape(-1), 0)
        lw = dict(mu=_shift_major(mu_shift[l], 0), w0=cm(w0[l]), a0=cm(a0[l]), lorat=lorat,
                  k_k=cm(k_k[l]), k_a=cm(k_a[l]), r_k=cm(r_k[l]), lnx_g=lnx_g[l], lnx_b=lnx_b[l])
        woa_bf, wout_bf = _chan_major(w_o_a[l], 0).astype(BF16), w_out[l].astype(BF16)
        wob_bf = _head_pairs(w_o_b[l], 0).astype(BF16)
        q_gain = jnp.tile(q_norm_g[l], H_B).reshape(1, W_B)
        k_gain = jnp.tile(k_norm_g[l], KV_HEADS).reshape(1, KV_W)
        sink_col = jnp.repeat(sinks[l].reshape(KV_HEADS, Q_PER_KV), WINDOW, axis=1).reshape(
            KV_HEADS, Q_PER_KV * WINDOW, 1)
        sink_rows = pair(sinks[l].reshape(1, H_B)).transpose(1, 0, 2)

        mod = _ada(c_all, w_ada[l].astype(BF16), b_ada[l])
        shift, scale, gate = mod[:, :D_MODEL], mod[:, D_MODEL:2 * D_MODEL], mod[:, 2 * D_MODEL:]

        sp, scp, gp = (a[:nbp].reshape(nbp, 1, D_MODEL) for a in (shift, scale, gate))
        usht, zat, qkv, kv32, zb, gts_p = _in_proj(xp, norm_g[l], scp, sp, w_bf, wt_bf, q_gain, k_gain, g512, g128,
                                                   tm_in, t_len, False)
        yt, wkv = _rwkv_prompt(usht, lw, tm_p)
        yb = _attn(qkv, zb, nbp, gtab, sink_col)
        xp = _merge(yt, zat, yb, gts_p, xp, gp, woa_bf, wob_bf, wout_bf, tm_p, t_len, False)
        win = kv32.reshape(nbp, t_len, 2, KV_HEADS, HEAD_DIM)[:, t_len - WINDOW:]
        outs["wkv_p"].append(wkv)
        outs["shift_p"].append(_shift_major(usht[:, :, t_len - 1], 1, inverse=True))
        outs["kw_p"].append(win[:, :, 0])
        outs["vw_p"].append(win[:, :, 1])

        ss, scs, gs = shift[nbp:], scale[nbp:], gate[nbp:]
        usht, zat, qkv, kv32, zb, gts_s = _in_proj(xs, norm_g[l], scs, ss, w_bf, wt_bf, q_gain, k_gain, g512, g128,
                                                   nbs, nbs, True)
        yt, wkv = _rwkv_sample(usht, state_shift[l], state_wkv[l], lw)
        yb, kwin, vwin = _attn_step(qkv, kv32, zb, cache_k[l].reshape(nbs, WINDOW, KV_W),
                                    cache_v[l].reshape(nbs, WINDOW, KV_W),
                                    bias_rows, bias0, sink_rows, g128, _pick((16, 8), nbs))
        xs = _merge(yt, zat, yb, gts_s, xs, gs, woa_bf, wob_bf, wout_bf, nbs, nbs, True)
        outs["wkv_s"].append(wkv)
        outs["shift_s"].append(_shift_major(usht[0].T, 1, inverse=True))
        outs["kw_s"].append(kwin.reshape(nbs, WINDOW, KV_HEADS, HEAD_DIM))
        outs["vw_s"].append(vwin.reshape(nbs, WINDOW, KV_HEADS, HEAD_DIM))

    st = lambda k: jnp.stack(outs[k])
    return (xp.reshape(nbp, t_len, D_MODEL), xs.reshape(nbs, 1, D_MODEL),
            st("wkv_p"), st("shift_p"), st("kw_p"), st("vw_p"),
            st("wkv_s"), st("shift_s"), st("kw_s"), st("vw_s"))
```

```python
import functools
import math

import numpy as np
import jax
import jax.numpy as jnp
from jax import lax
from jax.experimental import pallas as pl
from jax.experimental.pallas import tpu as pltpu

F32 = jnp.float32
BF16 = jnp.bfloat16

D_MODEL = 1024
W_A = 512
HEAD_SIZE = 64
H_A = W_A // HEAD_SIZE
LORA = 64
SHIFT_W = 3 * W_A + 2 * LORA
W_B = 512
HEAD_DIM = 64
H_B = W_B // HEAD_DIM
KV_HEADS = 2
Q_PER_KV = H_B // KV_HEADS
KV_W = KV_HEADS * HEAD_DIM
WINDOW = 128
N_BUCKETS = 32
MAX_DISTANCE = 128
NORM_EPS = 1e-6
GN_EPS = 64e-5
NEG_INF = -1e30
QKV_W = W_B + 2 * KV_W
GATE_W = 2 * D_MODEL
ROW_COLS = QKV_W + W_B + GATE_W
T_COLS = SHIFT_W + W_A
LANES = 128
SUBLANES = 8
def _head_pairs(x, axis):
    s = x.shape
    x = x.reshape(s[:axis] + (KV_HEADS, Q_PER_KV, HEAD_DIM) + s[axis + 1:])
    return jnp.swapaxes(x, axis, axis + 1).reshape(s)


def _chan_major(x, axis, inverse=False):
    s = x.shape
    split = (HEAD_SIZE, H_A) if inverse else (H_A, HEAD_SIZE)
    x = x.reshape(s[:axis] + split + s[axis + 1:])
    return jnp.swapaxes(x, axis, axis + 1).reshape(s)


def _shift_major(x, axis, inverse=False):
    parts = [lax.slice_in_dim(x, i * W_A, (i + 1) * W_A, axis=axis) for i in range(3)]
    tail = lax.slice_in_dim(x, 3 * W_A, SHIFT_W, axis=axis)
    return jnp.concatenate([_chan_major(p, axis, inverse) for p in parts] + [tail], axis=axis)
ATTN_BLOCKS = 4
BIAS_SPAN = 3 * WINDOW
KEY_UNROLL = 32
CHUNK = 64
CHUNKS_PER_STEP = 2
PAIR = 2 * HEAD_SIZE
NEUMANN_STEPS = CHUNK.bit_length() - 2


def _dot(a, b):
    return jnp.dot(a, b, preferred_element_type=F32)


def _dot_nt(a, b):
    return lax.dot_general(a, b, (((1,), (1,)), ((), ())), preferred_element_type=F32)


def _dot_tn(a, b):
    return lax.dot_general(a, b, (((0,), (0,)), ((), ())), preferred_element_type=F32)


def _split_dot(x, g):
    hi = x.astype(BF16)
    lo = (x - hi.astype(F32)).astype(BF16)
    return _dot(hi, g) + _dot(lo, g)


def _silu(x):
    return x * jax.nn.sigmoid(x)


def _pick(cands, n):
    for c in cands:
        if n % c == 0:
            return c
    return n


def _ada_kernel(c_ref, w_ref, b_ref, o_ref):
    c = c_ref[...]
    o_ref[...] = _dot(_silu(c).astype(BF16), w_ref[...]) + b_ref[...]


def _ada(c, w_bf, b):
    nb, d = c.shape
    n = w_bf.shape[1]
    tn = 1024
    return pl.pallas_call(
        _ada_kernel,
        grid=(n // tn,),
        in_specs=[pl.BlockSpec((nb, d), lambda j: (0, 0)),
                  pl.BlockSpec((d, tn), lambda j: (0, j)),
                  pl.BlockSpec((1, tn), lambda j: (0, j))],
        out_specs=pl.BlockSpec((nb, tn), lambda j: (0, j)),
        out_shape=jax.ShapeDtypeStruct((nb, n), F32),
        name="ada",
    )(c, w_bf, b.reshape(1, n))


def _in_proj_kernel(x_ref, g_ref, scale_ref, shift_ref, w_ref, wa_ref, qg_ref, kg_ref, g512_ref, g128_ref,
                    ush_ref, za_ref, qkv_ref, kv32_ref, zb_ref, gate_ref, *, transposed):
    x = x_ref[...]
    ms = jnp.mean(x * x, axis=-1, keepdims=True)
    h = (x * lax.rsqrt(ms + NORM_EPS)) * g_ref[...]
    h = h * (1.0 + scale_ref[...]) + shift_ref[...]
    hb = h.astype(BF16)
    if transposed:
        ush_ref[...] = _dot_nt(wa_ref[0:SHIFT_W, :], hb)
        za_ref[...] = _dot_nt(wa_ref[SHIFT_W:T_COLS, :], hb)
    else:
        ush_ref[...] = _dot(hb, wa_ref[:, 0:SHIFT_W])
        za_ref[...] = _dot(hb, wa_ref[:, SHIFT_W:T_COLS])
    qkv = _dot(hb, w_ref[:, 0:QKV_W])
    q, k, v = qkv[:, 0:W_B], qkv[:, W_B:W_B + KV_W], qkv[:, W_B + KV_W:QKV_W]
    qn = q * lax.rsqrt(_split_dot(q * q, g512_ref[...]) * (1.0 / HEAD_DIM) + NORM_EPS) * qg_ref[...]
    kn = k * lax.rsqrt(_split_dot(k * k, g128_ref[...]) * (1.0 / HEAD_DIM) + NORM_EPS) * kg_ref[...]
    qkv_ref[:, 0:W_B] = (qn * (HEAD_DIM ** -0.5)).astype(BF16)
    qkv_ref[:, W_B:W_B + KV_W] = kn.astype(BF16)
    qkv_ref[:, W_B + KV_W:QKV_W] = v.astype(BF16)
    kv32_ref[:, 0:KV_W] = kn
    kv32_ref[:, KV_W:2 * KV_W] = v
    zb_ref[...] = _dot(hb, w_ref[:, QKV_W:QKV_W + W_B])
    gate_ref[...] = _dot(hb, w_ref[:, QKV_W + W_B:ROW_COLS])


def _mod_spec(per_row, tm, rows_per_seq):
    if per_row:
        return pl.BlockSpec((tm, D_MODEL), lambda i: (i, 0))
    return pl.BlockSpec((None, 1, D_MODEL), lambda i: ((i * tm) // rows_per_seq, 0, 0))


def _t_spec(cols, tm, rows_per_seq):
    per_seq = rows_per_seq // tm
    return pl.BlockSpec((None, cols, tm), lambda i: (i // per_seq, 0, i % per_seq))


def _in_proj(x, norm_g, scale, shift, w_bf, wa_bf, q_gain, k_gain, g512, g128, tm, rows_per_seq, per_row,
             transposed):
    m = x.shape[0]
    nseq = m // rows_per_seq
    const = lambda r, c: pl.BlockSpec((r, c), lambda i: (0, 0))
    rows = lambda w: pl.BlockSpec((tm, w), lambda i: (i, 0))
    widths = (QKV_W, 2 * KV_W, W_B, GATE_W)
    dtypes = (BF16, F32, F32, F32)
    if transposed:
        a_specs = [_t_spec(SHIFT_W, tm, rows_per_seq), _t_spec(W_A, tm, rows_per_seq)]
        a_shapes = [jax.ShapeDtypeStruct((nseq, SHIFT_W, rows_per_seq), F32),
                    jax.ShapeDtypeStruct((nseq, W_A, rows_per_seq), F32)]
    else:
        a_specs = [rows(SHIFT_W), rows(W_A)]
        a_shapes = [jax.ShapeDtypeStruct((m, SHIFT_W), F32), jax.ShapeDtypeStruct((m, W_A), F32)]
    return pl.pallas_call(
        functools.partial(_in_proj_kernel, transposed=transposed),
        grid=(m // tm,),
        in_specs=[rows(D_MODEL),
                  const(1, D_MODEL),
                  _mod_spec(per_row, tm, rows_per_seq),
                  _mod_spec(per_row, tm, rows_per_seq),
                  const(D_MODEL, ROW_COLS), const(*wa_bf.shape),
                  const(1, W_B), const(1, KV_W), const(W_B, W_B), const(KV_W, KV_W)],
        out_specs=a_specs + [rows(w) for w in widths],
        out_shape=a_shapes + [jax.ShapeDtypeStruct((m, w), dt) for w, dt in zip(widths, dtypes)],
        name="in_proj",
    )(x, norm_g.reshape(1, D_MODEL), scale, shift, w_bf, wa_bf, q_gain, k_gain, g512, g128)


def _prep_kernel(u_ref, prev_ref, shift0_ref, mu_ref, w0_ref, a0_ref, lora_ref, kk_ref, ka_ref, rk_ref,
                 r_o, w_o, k_o, v_o, kk_o, b_o, coef_o, *, per_token_state):
    tm = u_ref.shape[1]
    heads = lambda x: x.reshape(HEAD_SIZE, H_A, LANES)
    lane = lax.broadcasted_iota(jnp.int32, (SHIFT_W, LANES), 1)
    row = lax.broadcasted_iota(jnp.int32, (2 * LORA, LANES), 0)
    if not per_token_state:
        before = jnp.where(pl.program_id(1) == 0, shift0_ref[...], prev_ref[...])
        rolled_before = pltpu.roll(before, 1, axis=1)
    for ci in range(tm // LANES):
        cols = slice(ci * LANES, (ci + 1) * LANES)
        u = u_ref[:, cols]
        if per_token_state:
            u_prev = shift0_ref[:, cols]
        else:
            rolled = pltpu.roll(u, 1, axis=1)
            u_prev = jnp.where(lane == 0, rolled_before, rolled)
            rolled_before = rolled
        xs = u + (u_prev - u) * mu_ref[...]
        r = xs[0:W_A]
        k = xs[W_A:2 * W_A]
        v = xs[2 * W_A:3 * W_A]
        tail = xs[3 * W_A:SHIFT_W]
        lora_in = jnp.where(row < LORA, jnp.tanh(tail), tail).astype(BF16)
        up = _dot(lora_ref[...], lora_in)
        neg = -(w0_ref[...] + up[0:W_A])
        softplus = jnp.maximum(neg, 0.0) + jnp.log(1.0 + jnp.exp(-jnp.abs(neg)))
        w_log = -softplus - 0.5
        decay = jnp.exp(-jnp.exp(w_log))
        a = jax.nn.sigmoid(a0_ref[...] + up[W_A:2 * W_A])
        kk = heads(k * kk_ref[...])
        norm = jnp.sqrt(jnp.sum(kk * kk, axis=0, keepdims=True))
        kk = kk / jnp.maximum(norm, 1e-12)
        k_mod = k * (1.0 + (a - 1.0) * ka_ref[...])
        r_o[:, :, cols] = heads(r)
        w_o[:, :, cols] = heads(decay)
        k_o[:, :, cols] = heads(k_mod)
        v_o[:, :, cols] = heads(v)
        kk_o[:, :, cols] = kk
        b_o[:, :, cols] = kk * heads(a)
        coef_o[:, cols] = jnp.sum(heads(r * k_mod * rk_ref[...]), axis=0)


def _prep(usht, shift0t, mu, w0, a0, lorat_bf, k_k, k_a, r_k, tm, per_token_state):
    nseq, _, t_len = usht.shape
    col = lambda a: jnp.broadcast_to(a.reshape(-1, 1), (a.size, LANES))
    cvec = lambda n: pl.BlockSpec((n, LANES), lambda s, j: (0, 0))
    blk = lambda cols: pl.BlockSpec((None, cols, tm), lambda s, j: (s, 0, j))
    if per_token_state:
        prev_spec = pl.BlockSpec((None, SHIFT_W, LANES), lambda s, j: (0, 0, 0))
        shift0_spec = blk(SHIFT_W)
    else:
        prev_spec = pl.BlockSpec((None, SHIFT_W, LANES),
                                 lambda s, j: (s, 0, jnp.maximum(j * (tm // LANES) - 1, 0)))
        shift0_spec = pl.BlockSpec((None, SHIFT_W, LANES), lambda s, j: (s, 0, 0))
    chan = pl.BlockSpec((HEAD_SIZE, H_A, tm), lambda s, j: (0, s, j))
    return pl.pallas_call(
        functools.partial(_prep_kernel, per_token_state=per_token_state),
        grid=(nseq, t_len // tm),
        in_specs=[blk(SHIFT_W), prev_spec, shift0_spec, cvec(SHIFT_W), cvec(W_A), cvec(W_A),
                  pl.BlockSpec((2 * W_A, 2 * LORA), lambda s, j: (0, 0)), cvec(W_A), cvec(W_A), cvec(W_A)],
        out_specs=[chan] * 6 + [pl.BlockSpec((H_A, tm), lambda s, j: (s, j))],
        out_shape=[jax.ShapeDtypeStruct((HEAD_SIZE, nseq * H_A, t_len), F32)] * 6
                  + [jax.ShapeDtypeStruct((nseq * H_A, t_len), F32)],
        name="rwkv_prep",
    )(usht, usht, shift0t, col(mu), col(w0), col(a0), lorat_bf, col(k_k), col(k_a), col(r_k))


def _scan_kernel(*refs, tt, zero_init):
    if zero_init:
        (r_ref, w_ref, k_ref, v_ref, kk_ref, b_ref, coef_ref, lng_ref, lnb_ref,
         y_ref, pf_ref, p_ref, vs_ref, ys_ref) = refs
    else:
        (r_ref, w_ref, k_ref, v_ref, kk_ref, b_ref, coef_ref, lng_ref, lnb_ref, p0_ref,
         y_ref, pf_ref, p_ref, vs_ref, ys_ref) = refs
    ti = pl.program_id(1)
    vs_ref[...] = jnp.swapaxes(v_ref[...], 0, 1)

    @pl.when(ti == 0)
    def _():
        if zero_init:
            p_ref[...] = jnp.zeros(p_ref.shape, F32)
        else:
            p_ref[...] = p0_ref[...]

    def step(t, carry):
        zero = jnp.zeros((HEAD_SIZE, LANES), F32)
        row = pl.ds(t, 1)

        def reduce_keys(kb, part):
            part = list(part)
            for u in range(KEY_UNROLL):
                kc = kb * KEY_UNROLL + u
                part[u % 2] = part[u % 2] + p_ref[kc] * kk_ref[kc, row, :]
            return tuple(part)

        part = lax.fori_loop(0, HEAD_SIZE // KEY_UNROLL, reduce_keys, (zero, zero))
        sa = -(part[0] + part[1])
        vt = vs_ref[t]

        def update_keys(kb, out):
            for u in range(KEY_UNROLL):
                kc = kb * KEY_UNROLL + u
                p = p_ref[kc] * w_ref[kc, row, :] + sa * b_ref[kc, row, :] + vt * k_ref[kc, row, :]
                p_ref[kc] = p
                out = out + p * r_ref[kc, row, :]
            return out

        o = lax.fori_loop(0, HEAD_SIZE // KEY_UNROLL, update_keys, zero)
        mean = jnp.sum(o, axis=0, keepdims=True) * (1.0 / HEAD_SIZE)
        c = o - mean
        var = jnp.sum(c * c, axis=0, keepdims=True) * (1.0 / HEAD_SIZE)
        ys_ref[t] = c * lax.rsqrt(var + GN_EPS) * lng_ref[...] + lnb_ref[...] + coef_ref[row, :] * vt
        return carry

    lax.fori_loop(0, tt, step, 0)
    y_ref[...] = jnp.swapaxes(ys_ref[...], 0, 1)

    @pl.when(ti == pl.num_programs(1) - 1)
    def _():
        pf_ref[...] = p_ref[...]


def _scan(seqs, coef, tiles, p0, tt):
    _, t_len, n = seqs[0].shape
    seq_spec = pl.BlockSpec((HEAD_SIZE, tt, LANES), lambda j, i: (0, i, j))
    coef_spec = pl.BlockSpec((tt, LANES), lambda j, i: (i, j))
    tile_spec = pl.BlockSpec((HEAD_SIZE, LANES), lambda j, i: (0, j))
    state_spec = pl.BlockSpec((HEAD_SIZE, HEAD_SIZE, LANES), lambda j, i: (0, 0, j))
    zero_init = p0 is None
    args = list(seqs) + [coef] + list(tiles) + ([] if zero_init else [p0])
    return pl.pallas_call(
        functools.partial(_scan_kernel, tt=tt, zero_init=zero_init),
        grid=(n // LANES, t_len // tt),
        in_specs=[seq_spec] * 6 + [coef_spec] + [tile_spec] * 2 + ([] if zero_init else [state_spec]),
        out_specs=[seq_spec, state_spec],
        out_shape=[jax.ShapeDtypeStruct((HEAD_SIZE, t_len, n), F32),
                   jax.ShapeDtypeStruct((HEAD_SIZE, HEAD_SIZE, n), F32)],
        scratch_shapes=[pltpu.VMEM((HEAD_SIZE, HEAD_SIZE, LANES), F32),
                        pltpu.VMEM((tt, HEAD_SIZE, LANES), F32), pltpu.VMEM((tt, HEAD_SIZE, LANES), F32)],
        compiler_params=pltpu.CompilerParams(dimension_semantics=("arbitrary", "arbitrary")),
        name="rwkv_scan",
    )(*args)


def _prep_rows_kernel(u_ref, prev_ref, mu_ref, w0_ref, a0_ref, lora_ref, kk_ref, ka_ref, rk_ref, g_ref,
                      r_o, lw_o, k_o, v_o, kk_o, b_o, bonus_o, *, tiles_per_seq):
    u = u_ref[...]
    first = (pl.program_id(0) % tiles_per_seq) == 0
    prev_row = jnp.where(first, 0.0, prev_ref[SUBLANES - 1:SUBLANES, :])
    row = lax.broadcasted_iota(jnp.int32, u.shape, 0)
    u_prev = jnp.where(row == 0, prev_row, pltpu.roll(u, 1, axis=0))
    xs = u + (u_prev - u) * mu_ref[...]
    r = xs[:, 0:W_A]
    k = xs[:, W_A:2 * W_A]
    v = xs[:, 2 * W_A:3 * W_A]
    tail = xs[:, 3 * W_A:SHIFT_W]
    lane = lax.broadcasted_iota(jnp.int32, tail.shape, 1)
    lora_in = jnp.where(lane < LORA, jnp.tanh(tail), tail).astype(BF16)
    up = _dot(lora_in, lora_ref[...])
    neg = -(w0_ref[...] + up[:, 0:W_A])
    softplus = jnp.maximum(neg, 0.0) + jnp.log(1.0 + jnp.exp(-jnp.abs(neg)))
    w_log = -softplus - 0.5
    a = jax.nn.sigmoid(a0_ref[...] + up[:, W_A:2 * W_A])
    kk = k * kk_ref[...]
    norm = jnp.sqrt(_split_dot(kk * kk, g_ref[...]))
    kk = kk / jnp.maximum(norm, 1e-12)
    k_mod = k * (1.0 + (a - 1.0) * ka_ref[...])
    r_o[...] = r
    lw_o[...] = -jnp.exp(w_log)
    k_o[...] = k_mod
    v_o[...] = v
    kk_o[...] = kk
    b_o[...] = kk * a
    bonus_o[...] = _split_dot(r * k_mod * rk_ref[...], g_ref[...]) * v


def _prep_rows(ush, mu, w0, a0, lora_bf, k_k, k_a, r_k, g512, tm, rows_per_seq):
    m = ush.shape[0]
    tiles_per_seq = rows_per_seq // tm
    row = lambda a: a.reshape(1, -1)
    vec = lambda n: pl.BlockSpec((1, n), lambda i: (0, 0))
    out = jax.ShapeDtypeStruct((m, W_A), F32)
    return pl.pallas_call(
        functools.partial(_prep_rows_kernel, tiles_per_seq=tiles_per_seq),
        grid=(m // tm,),
        in_specs=[pl.BlockSpec((tm, SHIFT_W), lambda i: (i, 0)),
                  pl.BlockSpec((SUBLANES, SHIFT_W), lambda i: (jnp.maximum(i * (tm // SUBLANES) - 1, 0), 0)),
                  vec(SHIFT_W), vec(W_A), vec(W_A),
                  pl.BlockSpec((2 * LORA, 2 * W_A), lambda i: (0, 0)),
                  vec(W_A), vec(W_A), vec(W_A),
                  pl.BlockSpec((W_A, W_A), lambda i: (0, 0))],
        out_specs=[pl.BlockSpec((tm, W_A), lambda i: (i, 0))] * 7,
        out_shape=[out] * 7,
        name="rwkv_prep_rows",
    )(ush, ush, row(mu), row(w0), row(a0), lora_bf, row(k_k), row(k_a), row(r_k), g512)


def _chunk_kernel(r_ref, lw_ref, k_ref, v_ref, kk_ref, b_ref, o_ref, pf_ref, p_ref, *, nchunk):
    npair = W_A // PAIR

    @pl.when(pl.program_id(1) == 0)
    def _():
        p_ref[...] = jnp.zeros(p_ref.shape, F32)

    ti = lax.broadcasted_iota(jnp.int32, (CHUNK, CHUNK), 0)
    tj = lax.broadcasted_iota(jnp.int32, (CHUNK, CHUNK), 1)
    strict = ti > tj
    incl = ti >= tj
    eye = jnp.where(ti == tj, 1.0, 0.0).astype(F32)
    tri = jnp.where(incl, 1.0, 0.0).astype(BF16)
    lane = lax.broadcasted_iota(jnp.int32, (CHUNK, PAIR), 1)
    head_a = lane < HEAD_SIZE
    pi = lax.broadcasted_iota(jnp.int32, (PAIR, PAIR), 0)
    pj = lax.broadcasted_iota(jnp.int32, (PAIR, PAIR), 1)
    same_head = (pi < HEAD_SIZE) == (pj < HEAD_SIZE)
    on_diag = pi == pj
    bf = lambda x: x.astype(BF16)
    cat = lambda a, b: jnp.concatenate([a, b], axis=1)

    items = [(pr, c) for pr in range(npair) for c in range(nchunk)]
    sels = (head_a, jnp.logical_not(head_a))
    pre = []
    for pr, c in items:
        r, lw, k, v, kk, b = (ref[c * CHUNK:(c + 1) * CHUNK, pr * PAIR:(pr + 1) * PAIR]
                              for ref in (r_ref, lw_ref, k_ref, v_ref, kk_ref, b_ref))
        g = _dot(tri, bf(lw))
        g = g + _dot(tri, bf(lw - bf(lw).astype(F32)))
        g_end = g[CHUNK - 1:CHUNK, :]
        e_neg = jnp.exp(-g)
        e_end = jnp.exp(g_end - g)
        kkt = kk * jnp.exp(g - lw)
        pre.append(dict(kkt=kkt, rt=r * jnp.exp(g), kkt_b=bf(kkt), bt_b=bf(b * e_neg), kt_b=bf(k * e_neg),
                        v_b=bf(v), bh_b=bf(b * e_end), kh_b=bf(k * e_end), gamma_end=jnp.exp(g_end)))
    tris = []
    for it in pre:
        for sel in sels:
            lhs = bf(jnp.concatenate([jnp.where(sel, it["kkt"], 0.0), jnp.where(sel, it["rt"], 0.0)], axis=0))
            xb = _dot_nt(lhs, it["bt_b"])
            xk = _dot_nt(lhs, it["kt_b"])
            tris.append((jnp.where(strict, xb[0:CHUNK], 0.0), jnp.where(strict, xk[0:CHUNK], 0.0),
                         jnp.where(incl, xb[CHUNK:2 * CHUNK], 0.0), jnp.where(incl, xk[CHUNK:2 * CHUNK], 0.0)))
    t_inv = [eye - t[0] for t in tris]
    power = [t[0] for t in tris]
    for _ in range(NEUMANN_STEPS):
        power = [_dot(bf(x), bf(x)) for x in power]
        t_inv = [t + _dot(bf(t), bf(x)) for t, x in zip(t_inv, power)]
    owner = [it for it in pre for _ in sels]
    lkv = [_dot(bf(t[1]), it["v_b"]) for t, it in zip(tris, owner)]
    tx = [_dot(bf(t), cat(it["kkt_b"], bf(x))) for t, it, x in zip(t_inv, owner, lkv)]
    mx = [_dot(bf(t[2]), bf(x)) for t, x in zip(tris, tx)]
    mv = [_dot(bf(t[3]), it["v_b"]) for t, it in zip(tris, owner)]
    pick = lambda xa, xb: jnp.where(head_a, xa, xb)
    affine = []
    for n, it in enumerate(pre):
        ia, ib = 2 * n, 2 * n + 1
        w_all = pick(tx[ia][:, 0:PAIR], tx[ib][:, 0:PAIR])
        u_loc = pick(tx[ia][:, PAIR:2 * PAIR], tx[ib][:, PAIR:2 * PAIR])
        q_eff = it["rt"] - pick(mx[ia][:, 0:PAIR], mx[ib][:, 0:PAIR])
        o_loc = pick(mv[ia], mv[ib]) - pick(mx[ia][:, PAIR:2 * PAIR], mx[ib][:, PAIR:2 * PAIR])
        bx = _dot_tn(it["bh_b"], cat(bf(w_all), bf(u_loc)))
        kv = _dot_tn(it["kh_b"], it["v_b"])
        a_eff = jnp.where(on_diag, it["gamma_end"], 0.0) - jnp.where(same_head, bx[:, 0:PAIR], 0.0)
        p_loc = jnp.where(same_head, kv - bx[:, PAIR:2 * PAIR], 0.0)
        affine.append((bf(a_eff), p_loc, bf(q_eff), o_loc))

    state = [p_ref[pr] for pr in range(npair)]
    for c in range(nchunk):
        for pr in range(npair):
            a_eff, p_loc, q_eff, o_loc = affine[pr * nchunk + c]
            p_b = bf(state[pr])
            o_ref[c * CHUNK:(c + 1) * CHUNK, pr * PAIR:(pr + 1) * PAIR] = _dot(q_eff, p_b) + o_loc
            state[pr] = _dot(a_eff, p_b) + p_loc
    for pr in range(npair):
        p_ref[pr] = state[pr]
        pf_ref[pr] = state[pr]


def _chunk_scan(seqs, nseq, nchunk):
    m = seqs[0].shape[0]
    tt = nchunk * CHUNK
    steps = m // (nseq * tt)
    npair = W_A // PAIR
    blk = pl.BlockSpec((tt, W_A), lambda s, i: (s * steps + i, 0))
    o, pf = pl.pallas_call(
        functools.partial(_chunk_kernel, nchunk=nchunk),
        grid=(nseq, steps),
        in_specs=[blk] * 6,
        out_specs=[blk, pl.BlockSpec((None, npair, PAIR, PAIR), lambda s, i: (s, 0, 0, 0))],
        out_shape=[jax.ShapeDtypeStruct((m, W_A), F32),
                   jax.ShapeDtypeStruct((nseq, npair, PAIR, PAIR), F32)],
        scratch_shapes=[pltpu.VMEM((npair, PAIR, PAIR), F32)],
        compiler_params=pltpu.CompilerParams(dimension_semantics=("arbitrary", "arbitrary")),
        name="rwkv_chunk_scan",
    )(*seqs)
    pf = pf.reshape(nseq, npair, 2, HEAD_SIZE, 2, HEAD_SIZE)
    heads = jnp.stack([pf[:, :, 0, :, 0, :], pf[:, :, 1, :, 1, :]], axis=2)
    return o, heads.reshape(nseq, H_A, HEAD_SIZE, HEAD_SIZE).transpose(0, 1, 3, 2)


def _attn_kernel(cur_ref, prev_ref, zb_ref, gtab_ref, sink_ref, yb_ref, bias_ref, *, blocks):
    s = pl.program_id(1)
    rows = Q_PER_KV * WINDOW

    @pl.when((pl.program_id(0) == 0) & (s == 0))
    def _():
        qi = lax.broadcasted_iota(jnp.int32, (WINDOW, 2 * WINDOW), 0)
        kj = lax.broadcasted_iota(jnp.int32, (WINDOW, 2 * WINDOW), 1)
        dist = qi + WINDOW - kj
        band = (dist >= 0) & (dist <= WINDOW)
        for g in range(KV_HEADS):
            for j in range(Q_PER_KV):
                h = g * Q_PER_KV + j
                row = jnp.broadcast_to(gtab_ref[h:h + 1, :], (WINDOW, BIAS_SPAN))
                toe = pltpu.roll(row, 0, 1, stride=1, stride_axis=0)[:, WINDOW:BIAS_SPAN]
                bias_ref[g, j * WINDOW:(j + 1) * WINDOW, :] = jnp.where(band, toe, NEG_INF)

    cur = cur_ref[...]
    prev = prev_ref[...]
    k_all = jnp.concatenate([prev[:, W_B:W_B + KV_W], cur[:, W_B:W_B + KV_W]], axis=0)
    v_all = jnp.concatenate([prev[:, W_B + KV_W:QKV_W], cur[:, W_B + KV_W:QKV_W]], axis=0)
    lane = lax.broadcasted_iota(jnp.int32, k_all.shape, 1)
    zero = jnp.zeros_like(k_all)
    k_g = [jnp.where(lane < HEAD_DIM, k_all, zero), jnp.where(lane >= HEAD_DIM, k_all, zero)]
    v_g = [jnp.where(lane < HEAD_DIM, v_all, zero), jnp.where(lane >= HEAD_DIM, v_all, zero)]
    ones = jnp.ones((2 * WINDOW, LANES), BF16)
    kj = lax.broadcasted_iota(jnp.int32, (rows, 2 * WINDOW), 1)
    for i in range(blocks):
        r0 = i * WINDOW
        q4 = jnp.concatenate([cur[r0:r0 + WINDOW, j * LANES:(j + 1) * LANES] for j in range(Q_PER_KV)], axis=0)
        acc = None
        for g in range(KV_HEADS):
            logits = _dot_nt(q4, k_g[g][r0:r0 + 2 * WINDOW]) + bias_ref[g]
            if i == 0:
                logits = jnp.where((kj >= WINDOW) | (s > 0), logits, NEG_INF)
            sink = sink_ref[g]
            m = jnp.maximum(jnp.max(logits, axis=-1, keepdims=True), sink)
            p = jnp.exp(logits - m).astype(BF16)
            den = _dot(p, ones) + jnp.exp(sink - m)
            o = _dot(p, v_g[g][r0:r0 + 2 * WINDOW]) / den
            acc = o if acc is None else acc + o
        for j in range(Q_PER_KV):
            zb = zb_ref[r0:r0 + WINDOW, j * LANES:(j + 1) * LANES]
            yb_ref[r0:r0 + WINDOW, j * LANES:(j + 1) * LANES] = (
                acc[j * WINDOW:(j + 1) * WINDOW] * _silu(zb)).astype(BF16)


def _attn(qkv, zb, nb_seq, gtab, sink_col):
    m = qkv.shape[0]
    nblk = m // (nb_seq * WINDOW)
    blocks = _pick((ATTN_BLOCKS, 2, 1), nblk)
    steps = nblk // blocks
    rows = Q_PER_KV * WINDOW
    return pl.pallas_call(
        functools.partial(_attn_kernel, blocks=blocks),
        grid=(nb_seq, steps),
        in_specs=[pl.BlockSpec((blocks * WINDOW, QKV_W), lambda b, s: (b * steps + s, 0)),
                  pl.BlockSpec((WINDOW, QKV_W), lambda b, s: (jnp.maximum((b * steps + s) * blocks - 1, 0), 0)),
                  pl.BlockSpec((blocks * WINDOW, W_B), lambda b, s: (b * steps + s, 0)),
                  pl.BlockSpec((H_B, BIAS_SPAN), lambda b, s: (0, 0)),
                  pl.BlockSpec((KV_HEADS, rows, 1), lambda b, s: (0, 0, 0))],
        out_specs=pl.BlockSpec((blocks * WINDOW, W_B), lambda b, s: (b * steps + s, 0)),
        out_shape=jax.ShapeDtypeStruct((m, W_B), BF16),
        scratch_shapes=[pltpu.VMEM((KV_HEADS, rows, 2 * WINDOW), F32)],
        compiler_params=pltpu.CompilerParams(dimension_semantics=("arbitrary", "arbitrary")),
        name="swa_prompt",
    )(qkv, qkv, zb, gtab, sink_col)


def _attn_step_kernel(q_ref, kn_ref, vn_ref, zb_ref, ck_ref, cv_ref, bias_ref, bias0_ref, sink_ref, g128_ref,
                      yb_ref, ko_ref, vo_ref):
    q = q_ref[...].astype(F32)
    kn = kn_ref[...]
    vn = vn_ref[...]
    ck = ck_ref[...]
    cv = cv_ref[...]
    bt = q.shape[0]
    for r in range(Q_PER_KV):
        qsel = q[:, r * LANES:(r + 1) * LANES]
        prod = (ck * qsel[:, None, :]).reshape(bt * WINDOW, KV_W)
        lg = _split_dot(prod, g128_ref[...]).reshape(bt, WINDOW, KV_W) + bias_ref[r]
        lgn = _split_dot(kn * qsel, g128_ref[...]) + bias0_ref[r]
        s = sink_ref[r]
        m = jnp.maximum(jnp.maximum(jnp.max(lg, axis=1), lgn), s)
        p = jnp.exp(lg - m[:, None, :])
        pn = jnp.exp(lgn - m)
        den = jnp.sum(p, axis=1) + pn + jnp.exp(s - m)
        o = (jnp.sum(p * cv, axis=1) + pn * vn) / den
        zb = zb_ref[:, r * LANES:(r + 1) * LANES]
        yb_ref[:, r * LANES:(r + 1) * LANES] = (o * _silu(zb)).astype(BF16)
    j = lax.broadcasted_iota(jnp.int32, ck.shape, 1)
    ko_ref[...] = jnp.where(j == WINDOW - 1, kn[:, None, :], pltpu.roll(ck, WINDOW - 1, axis=1))
    vo_ref[...] = jnp.where(j == WINDOW - 1, vn[:, None, :], pltpu.roll(cv, WINDOW - 1, axis=1))


def _attn_step(qkv, kv32, zb, cache_k, cache_v, bias_rows, bias0, sink_rows, g128, bt):
    nb = qkv.shape[0]
    cache_spec = pl.BlockSpec((bt, WINDOW, KV_W), lambda i: (i, 0, 0))
    return pl.pallas_call(
        _attn_step_kernel,
        grid=(nb // bt,),
        in_specs=[pl.BlockSpec((bt, W_B), lambda i: (i, 0)),
                  pl.BlockSpec((bt, KV_W), lambda i: (i, 0)),
                  pl.BlockSpec((bt, KV_W), lambda i: (i, 1)),
                  pl.BlockSpec((bt, W_B), lambda i: (i, 0)),
                  cache_spec, cache_spec,
                  pl.BlockSpec((Q_PER_KV, WINDOW, KV_W), lambda i: (0, 0, 0)),
                  pl.BlockSpec((Q_PER_KV, 1, KV_W), lambda i: (0, 0, 0)),
                  pl.BlockSpec((Q_PER_KV, 1, KV_W), lambda i: (0, 0, 0)),
                  pl.BlockSpec((KV_W, KV_W), lambda i: (0, 0))],
        out_specs=[pl.BlockSpec((bt, W_B), lambda i: (i, 0)), cache_spec, cache_spec],
        out_shape=[jax.ShapeDtypeStruct((nb, W_B), BF16),
                   jax.ShapeDtypeStruct(cache_k.shape, F32),
                   jax.ShapeDtypeStruct(cache_v.shape, F32)],
        name="swa_step",
    )(qkv, kv32, kv32, zb, cache_k, cache_v, bias_rows, bias0, sink_rows, g128)


def _merge_kernel(yt_ref, zat_ref, yb_ref, ga_ref, gb_ref, x_ref, gate_ref, woa_ref, wob_ref, wout_ref, out_ref):
    yt = yt_ref[...]
    yat = (yt.reshape(W_A, yt.shape[2]) * _silu(zat_ref[...])).astype(BF16)
    pa = _dot_tn(yat, woa_ref[...])
    pb = _dot(yb_ref[...], wob_ref[...])
    merged = jax.nn.sigmoid(ga_ref[...]) * pa + jax.nn.sigmoid(gb_ref[...]) * pb
    out_ref[...] = x_ref[...] + gate_ref[...] * _dot(merged.astype(BF16), wout_ref[...])


def _merge(yt, zat, yb, gates, x, gate, woa_bf, wob_bf, wout_bf, tm, rows_per_seq, per_row):
    m = x.shape[0]
    full = lambda c: pl.BlockSpec((tm, D_MODEL), lambda i: (i, c))
    per_seq = rows_per_seq // tm
    y_spec = pl.BlockSpec((HEAD_SIZE, H_A, tm), lambda i: (0, i // per_seq, i % per_seq))
    return pl.pallas_call(
        _merge_kernel,
        grid=(m // tm,),
        in_specs=[y_spec, _t_spec(W_A, tm, rows_per_seq),
                  pl.BlockSpec((tm, W_B), lambda i: (i, 0)), full(0), full(1), full(0),
                  _mod_spec(per_row, tm, rows_per_seq),
                  pl.BlockSpec((W_A, D_MODEL), lambda i: (0, 0)),
                  pl.BlockSpec((W_B, D_MODEL), lambda i: (0, 0)),
                  pl.BlockSpec((D_MODEL, D_MODEL), lambda i: (0, 0))],
        out_specs=full(0),
        out_shape=jax.ShapeDtypeStruct((m, D_MODEL), F32),
        name="merge_out",
    )(yt, zat, yb, gates, gates, x, gate, woa_bf, wob_bf, wout_bf)


def _merge_rows_kernel(o_ref, bonus_ref, za_ref, yb_ref, ga_ref, gb_ref, x_ref, gate_ref, lng_ref, lnb_ref,
                       g512_ref, woa_ref, wob_ref, wout_ref, out_ref):
    o = o_ref[...]
    mean = _split_dot(o, g512_ref[...]) * (1.0 / HEAD_SIZE)
    c = o - mean
    var = _split_dot(c * c, g512_ref[...]) * (1.0 / HEAD_SIZE)
    on = c * lax.rsqrt(var + GN_EPS) * lng_ref[...] + lnb_ref[...]
    ya = (on + bonus_ref[...]) * _silu(za_ref[...])
    pa = _dot(ya.astype(BF16), woa_ref[...])
    pb = _dot(yb_ref[...], wob_ref[...])
    merged = jax.nn.sigmoid(ga_ref[...]) * pa + jax.nn.sigmoid(gb_ref[...]) * pb
    out_ref[...] = x_ref[...] + gate_ref[...] * _dot(merged.astype(BF16), wout_ref[...])


def _merge_rows(o, bonus, za, yb, gates, x, gate, lnx_g, lnx_b, g512, woa_bf, wob_bf, wout_bf, tm, rows_per_seq):
    m = x.shape[0]
    half = pl.BlockSpec((tm, W_A), lambda i: (i, 0))
    full = lambda c: pl.BlockSpec((tm, D_MODEL), lambda i: (i, c))
    vec = pl.BlockSpec((1, W_A), lambda i: (0, 0))
    return pl.pallas_call(
        _merge_rows_kernel,
        grid=(m // tm,),
        in_specs=[half, half, half, half, full(0), full(1), full(0),
                  _mod_spec(False, tm, rows_per_seq), vec, vec,
                  pl.BlockSpec((W_A, W_A), lambda i: (0, 0)),
                  pl.BlockSpec((W_A, D_MODEL), lambda i: (0, 0)),
                  pl.BlockSpec((W_B, D_MODEL), lambda i: (0, 0)),
                  pl.BlockSpec((D_MODEL, D_MODEL), lambda i: (0, 0))],
        out_specs=full(0),
        out_shape=jax.ShapeDtypeStruct((m, D_MODEL), F32),
        name="merge_out_rows",
    )(o, bonus, za, yb, gates, gates, x, gate, lnx_g.reshape(1, W_A), lnx_b.reshape(1, W_A), g512,
      woa_bf, wob_bf, wout_bf)


def _t5_bucket(dist):
    max_exact = N_BUCKETS // 2
    d = jnp.maximum(dist, 0)
    log_ratio = jnp.log(jnp.maximum(d, 1).astype(F32) / max_exact) / math.log(MAX_DISTANCE / max_exact)
    large = jnp.minimum(max_exact + (log_ratio * (N_BUCKETS - max_exact)).astype(jnp.int32), N_BUCKETS - 1)
    return jnp.where(d < max_exact, d, large)


def _block_ones(n, blk):
    i = np.arange(n) // blk
    return jnp.asarray((i[:, None] == i[None, :]).astype(np.float32), dtype=BF16)


def _rwkv_prompt(ush, nseq, lw, g512, tm):
    t_len = ush.shape[0] // nseq
    outs = _prep_rows(ush, lw["mu"], lw["w0"], lw["a0"], lw["lora"], lw["k_k"], lw["k_a"], lw["r_k"], g512,
                      tm, t_len)
    o, wkv = _chunk_scan(outs[:6], nseq, _pick((CHUNKS_PER_STEP, 2, 1), t_len // CHUNK))
    return o, outs[6], wkv


def _rwkv_sample(usht, shift0, state0, lw):
    nb = usht.shape[2]
    n = H_A * nb
    outs = _prep(usht, _shift_major(shift0, 1).T[None], lw["mu"], lw["w0"], lw["a0"], lw["lorat"], lw["k_k"],
                 lw["k_a"], lw["r_k"], nb, True)
    tile = lambda a: jnp.repeat(a.reshape(H_A, HEAD_SIZE).T, nb, axis=1)
    p0 = state0.transpose(3, 2, 1, 0).reshape(HEAD_SIZE, HEAD_SIZE, n)
    y, pf = _scan([a.reshape(HEAD_SIZE, 1, n) for a in outs[:6]], outs[6].reshape(1, n),
                  [tile(lw["lnx_g"]), tile(lw["lnx_b"])], p0, 1)
    wkv = pf.reshape(HEAD_SIZE, HEAD_SIZE, H_A, nb).transpose(3, 2, 1, 0)
    return y.reshape(HEAD_SIZE, H_A, nb), wkv


def kernel(x_prompt, x_sample, c_prompt, c_sample, state_wkv, state_shift, cache_k, cache_v, norm_g, w_ada, b_ada, w_in, mu_shift, w0, w_decay_up, a0, w_a_up, k_k, k_a, r_k, lnx_g, lnx_b, w_o_a, q_norm_g, k_norm_g, rel_bias, sinks, w_o_b, w_out):
    nbp, t_len, _ = x_prompt.shape
    nbs = x_sample.shape[0]
    depth = norm_g.shape[0]
    mp = nbp * t_len
    assert (nbp * H_A) % LANES == 0 and nbs % LANES == 0 and t_len % LANES == 0
    g512 = _block_ones(W_B, HEAD_DIM)
    g128 = _block_ones(KV_W, HEAD_DIM)

    gtab = rel_bias[_t5_bucket(2 * WINDOW - jnp.arange(BIAS_SPAN))].astype(F32).T
    bias_s = rel_bias[_t5_bucket(WINDOW - jnp.arange(WINDOW + 1))].astype(F32)
    pair = lambda a: jnp.concatenate([jnp.repeat(a[..., 0:Q_PER_KV, None], HEAD_DIM, axis=-1),
                                      jnp.repeat(a[..., Q_PER_KV:H_B, None], HEAD_DIM, axis=-1)], axis=-1)
    bias_rows = pair(bias_s[:WINDOW]).transpose(1, 0, 2)
    bias0 = pair(bias_s[WINDOW:]).transpose(1, 0, 2)

    c_all = jnp.concatenate([c_prompt, c_sample], axis=0)
    xp = x_prompt.reshape(mp, D_MODEL)
    xs = x_sample.reshape(nbs, D_MODEL)
    tm_p = _pick((256, 128), t_len)
    tm_in = _pick((512, 256, 128), t_len)
    outs = {k: [] for k in ("wkv_p", "shift_p", "kw_p", "vw_p", "wkv_s", "shift_s", "kw_s", "vw_s")}
    for l in range(depth):
        wl = w_in[l]
        base = SHIFT_W
        z_a, q, kb, vb, z_b, gts = (wl[:, base:base + 512], wl[:, base + 512:base + 1024],
                                    wl[:, base + 1024:base + 1152], wl[:, base + 1152:base + 1280],
                                    wl[:, base + 1280:base + 1792], wl[:, base + 1792:])
        w_bf = jnp.concatenate([_head_pairs(q, 1), kb, vb, _head_pairs(z_b, 1), gts], axis=1).astype(BF16)
        wa_bf = jnp.concatenate([wl[:, :SHIFT_W], z_a], axis=1).astype(BF16)
        wt_bf = jnp.concatenate([_shift_major(wl[:, :SHIFT_W], 1), _chan_major(z_a, 1)], axis=1).T.astype(BF16)
        zeros = jnp.zeros((LORA, W_A), F32)
        lora = jnp.concatenate([jnp.concatenate([w_decay_up[l], zeros], axis=1),
                                jnp.concatenate([zeros, w_a_up[l]], axis=1)], axis=0).astype(BF16)
        lorat = jnp.concatenate([jnp.concatenate([_chan_major(w_decay_up[l], 1), zeros], axis=1),
                                 jnp.concatenate([zeros, _chan_major(w_a_up[l], 1)], axis=1)],
                                axis=0).T.astype(BF16)
        cm = lambda a: _chan_major(a.reshape(-1), 0)
        lw_p = dict(mu=mu_shift[l], w0=w0[l], a0=a0[l], lora=lora, k_k=k_k[l], k_a=k_a[l], r_k=r_k[l].reshape(-1))
        lw = dict(mu=_shift_major(mu_shift[l], 0), w0=cm(w0[l]), a0=cm(a0[l]), lorat=lorat,
                  k_k=cm(k_k[l]), k_a=cm(k_a[l]), r_k=cm(r_k[l]), lnx_g=lnx_g[l], lnx_b=lnx_b[l])
        woa_rows_bf = w_o_a[l].astype(BF16)
        woa_bf, wout_bf = _chan_major(w_o_a[l], 0).astype(BF16), w_out[l].astype(BF16)
        wob_bf = _head_pairs(w_o_b[l], 0).astype(BF16)
        q_gain = jnp.tile(q_norm_g[l], H_B).reshape(1, W_B)
        k_gain = jnp.tile(k_norm_g[l], KV_HEADS).reshape(1, KV_W)
        sink_col = jnp.repeat(sinks[l].reshape(KV_HEADS, Q_PER_KV), WINDOW, axis=1).reshape(
            KV_HEADS, Q_PER_KV * WINDOW, 1)
        sink_rows = pair(sinks[l].reshape(1, H_B)).transpose(1, 0, 2)

        mod = _ada(c_all, w_ada[l].astype(BF16), b_ada[l])
        shift, scale, gate = mod[:, :D_MODEL], mod[:, D_MODEL:2 * D_MODEL], mod[:, 2 * D_MODEL:]

        sp, scp, gp = (a[:nbp].reshape(nbp, 1, D_MODEL) for a in (shift, scale, gate))
        ush, za, qkv, kv32, zb, gts_p = _in_proj(xp, norm_g[l], scp, sp, w_bf, wa_bf, q_gain, k_gain, g512, g128,
                                                 tm_in, t_len, False, False)
        o, bonus, wkv = _rwkv_prompt(ush, nbp, lw_p, g512, tm_p)
        yb = _attn(qkv, zb, nbp, gtab, sink_col)
        xp = _merge_rows(o, bonus, za, yb, gts_p, xp, gp, lnx_g[l], lnx_b[l], g512, woa_rows_bf, wob_bf, wout_bf,
                         tm_p, t_len)
        win = kv32.reshape(nbp, t_len, 2, KV_HEADS, HEAD_DIM)[:, t_len - WINDOW:]
        outs["wkv_p"].append(wkv)
        outs["shift_p"].append(ush.reshape(nbp, t_len, SHIFT_W)[:, t_len - 1])
        outs["kw_p"].append(win[:, :, 0])
        outs["vw_p"].append(win[:, :, 1])

        ss, scs, gs = shift[nbp:], scale[nbp:], gate[nbp:]
        usht, zat, qkv, kv32, zb, gts_s = _in_proj(xs, norm_g[l], scs, ss, w_bf, wt_bf, q_gain, k_gain, g512, g128,
                                                   nbs, nbs, True, True)
        yt, wkv = _rwkv_sample(usht, state_shift[l], state_wkv[l], lw)
        yb, kwin, vwin = _attn_step(qkv, kv32, zb, cache_k[l].reshape(nbs, WINDOW, KV_W),
                                    cache_v[l].reshape(nbs, WINDOW, KV_W),
                                    bias_rows, bias0, sink_rows, g128, _pick((16, 8), nbs))
        xs = _merge(yt, zat, yb, gts_s, xs, gs, woa_bf, wob_bf, wout_bf, nbs, nbs, True)
        outs["wkv_s"].append(wkv)
        outs["shift_s"].append(_shift_major(usht[0].T, 1, inverse=True))
        outs["kw_s"].append(kwin.reshape(nbs, WINDOW, KV_HEADS, HEAD_DIM))
        outs["vw_s"].append(vwin.reshape(nbs, WINDOW, KV_HEADS, HEAD_DIM))

    st = lambda k: jnp.stack(outs[k])
    return (xp.reshape(nbp, t_len, D_MODEL), xs.reshape(nbs, 1, D_MODEL),
            st("wkv_p"), st("shift_p"), st("kw_p"), st("vw_p"),
            st("wkv_s"), st("shift_s"), st("kw_s"), st("vw_s"))
```

```python
import functools
import math

import numpy as np
import jax
import jax.numpy as jnp
from jax import lax
from jax.experimental import pallas as pl
from jax.experimental.pallas import tpu as pltpu

F32 = jnp.float32
BF16 = jnp.bfloat16

D_MODEL = 1024
W_A = 512
HEAD_SIZE = 64
H_A = W_A // HEAD_SIZE
LORA = 64
SHIFT_W = 3 * W_A + 2 * LORA
W_B = 512
HEAD_DIM = 64
H_B = W_B // HEAD_DIM
KV_HEADS = 2
Q_PER_KV = H_B // KV_HEADS
KV_W = KV_HEADS * HEAD_DIM
WINDOW = 128
N_BUCKETS = 32
MAX_DISTANCE = 128
NORM_EPS = 1e-6
GN_EPS = 64e-5
NEG_INF = -1e30
QKV_W = W_B + 2 * KV_W
GATE_W = 2 * D_MODEL
ROW_COLS = QKV_W + W_B + GATE_W
T_COLS = SHIFT_W + W_A
LANES = 128
SUBLANES = 8
def _head_pairs(x, axis):
    s = x.shape
    x = x.reshape(s[:axis] + (KV_HEADS, Q_PER_KV, HEAD_DIM) + s[axis + 1:])
    return jnp.swapaxes(x, axis, axis + 1).reshape(s)


def _chan_major(x, axis, inverse=False):
    s = x.shape
    split = (HEAD_SIZE, H_A) if inverse else (H_A, HEAD_SIZE)
    x = x.reshape(s[:axis] + split + s[axis + 1:])
    return jnp.swapaxes(x, axis, axis + 1).reshape(s)


def _shift_major(x, axis, inverse=False):
    parts = [lax.slice_in_dim(x, i * W_A, (i + 1) * W_A, axis=axis) for i in range(3)]
    tail = lax.slice_in_dim(x, 3 * W_A, SHIFT_W, axis=axis)
    return jnp.concatenate([_chan_major(p, axis, inverse) for p in parts] + [tail], axis=axis)
ATTN_BLOCKS = 4
BIAS_SPAN = 3 * WINDOW
KEY_UNROLL = 32
CHUNK = 64
CHUNKS_PER_STEP = 2
PAIR = 2 * HEAD_SIZE
NEUMANN_STEPS = CHUNK.bit_length() - 2


def _dot(a, b):
    return jnp.dot(a, b, preferred_element_type=F32)


def _dot_nt(a, b):
    return lax.dot_general(a, b, (((1,), (1,)), ((), ())), preferred_element_type=F32)


def _dot_tn(a, b):
    return lax.dot_general(a, b, (((0,), (0,)), ((), ())), preferred_element_type=F32)


def _head_sum(x, g):
    return _dot(x.astype(BF16), g)


def _split_dot(x, g):
    hi = x.astype(BF16)
    lo = (x - hi.astype(F32)).astype(BF16)
    return _dot(hi, g) + _dot(lo, g)


def _silu(x):
    x = x.astype(F32)
    return x * jax.nn.sigmoid(x)


def _sigmoid(x):
    return jax.nn.sigmoid(x.astype(F32))


def _pick(cands, n):
    for c in cands:
        if n % c == 0:
            return c
    return n


def _ada_kernel(c_ref, w_ref, b_ref, o_ref):
    c = c_ref[...]
    o_ref[...] = _dot(_silu(c).astype(BF16), w_ref[...]) + b_ref[...]


def _ada(c, w_bf, b):
    nb, d = c.shape
    n = w_bf.shape[1]
    tn = 1024
    return pl.pallas_call(
        _ada_kernel,
        grid=(n // tn,),
        in_specs=[pl.BlockSpec((nb, d), lambda j: (0, 0)),
                  pl.BlockSpec((d, tn), lambda j: (0, j)),
                  pl.BlockSpec((1, tn), lambda j: (0, j))],
        out_specs=pl.BlockSpec((nb, tn), lambda j: (0, j)),
        out_shape=jax.ShapeDtypeStruct((nb, n), F32),
        name="ada",
    )(c, w_bf, b.reshape(1, n))


def _in_proj_kernel(x_ref, g_ref, scale_ref, shift_ref, w_ref, wa_ref, qg_ref, kg_ref, g512_ref, g128_ref,
                    ush_ref, za_ref, qkv_ref, kv32_ref, zb_ref, gate_ref, *, transposed):
    x = x_ref[...]
    ms = jnp.mean(x * x, axis=-1, keepdims=True)
    h = (x * lax.rsqrt(ms + NORM_EPS)) * g_ref[...]
    h = h * (1.0 + scale_ref[...]) + shift_ref[...]
    hb = h.astype(BF16)
    if transposed:
        ush_ref[...] = _dot_nt(wa_ref[0:SHIFT_W, :], hb)
        za_ref[...] = _dot_nt(wa_ref[SHIFT_W:T_COLS, :], hb).astype(BF16)
    else:
        ush_ref[...] = _dot(hb, wa_ref[:, 0:SHIFT_W])
        za_ref[...] = _dot(hb, wa_ref[:, SHIFT_W:T_COLS]).astype(BF16)
    qkv = _dot(hb, w_ref[:, 0:QKV_W])
    q, k, v = qkv[:, 0:W_B], qkv[:, W_B:W_B + KV_W], qkv[:, W_B + KV_W:QKV_W]
    qn = q * lax.rsqrt(_head_sum(q * q, g512_ref[...]) * (1.0 / HEAD_DIM) + NORM_EPS) * qg_ref[...]
    kn = k * lax.rsqrt(_head_sum(k * k, g128_ref[...]) * (1.0 / HEAD_DIM) + NORM_EPS) * kg_ref[...]
    qkv_ref[:, 0:W_B] = (qn * (HEAD_DIM ** -0.5)).astype(BF16)
    qkv_ref[:, W_B:W_B + KV_W] = kn.astype(BF16)
    qkv_ref[:, W_B + KV_W:QKV_W] = v.astype(BF16)
    kv32_ref[:, 0:KV_W] = kn
    kv32_ref[:, KV_W:2 * KV_W] = v
    zb_ref[...] = _dot(hb, w_ref[:, QKV_W:QKV_W + W_B]).astype(BF16)
    gate_ref[...] = _dot(hb, w_ref[:, QKV_W + W_B:ROW_COLS]).astype(BF16)


def _mod_spec(per_row, tm, rows_per_seq):
    if per_row:
        return pl.BlockSpec((tm, D_MODEL), lambda i: (i, 0))
    return pl.BlockSpec((None, 1, D_MODEL), lambda i: ((i * tm) // rows_per_seq, 0, 0))


def _t_spec(cols, tm, rows_per_seq):
    per_seq = rows_per_seq // tm
    return pl.BlockSpec((None, cols, tm), lambda i: (i // per_seq, 0, i % per_seq))


def _in_proj(x, norm_g, scale, shift, w_bf, wa_bf, q_gain, k_gain, g512, g128, tm, rows_per_seq, per_row,
             transposed):
    m = x.shape[0]
    nseq = m // rows_per_seq
    const = lambda r, c: pl.BlockSpec((r, c), lambda i: (0, 0))
    rows = lambda w: pl.BlockSpec((tm, w), lambda i: (i, 0))
    widths = (QKV_W, 2 * KV_W, W_B, GATE_W)
    dtypes = (BF16, F32, BF16, BF16)
    if transposed:
        a_specs = [_t_spec(SHIFT_W, tm, rows_per_seq), _t_spec(W_A, tm, rows_per_seq)]
        a_shapes = [jax.ShapeDtypeStruct((nseq, SHIFT_W, rows_per_seq), F32),
                    jax.ShapeDtypeStruct((nseq, W_A, rows_per_seq), BF16)]
    else:
        a_specs = [rows(SHIFT_W), rows(W_A)]
        a_shapes = [jax.ShapeDtypeStruct((m, SHIFT_W), F32), jax.ShapeDtypeStruct((m, W_A), BF16)]
    return pl.pallas_call(
        functools.partial(_in_proj_kernel, transposed=transposed),
        grid=(m // tm,),
        in_specs=[rows(D_MODEL),
                  const(1, D_MODEL),
                  _mod_spec(per_row, tm, rows_per_seq),
                  _mod_spec(per_row, tm, rows_per_seq),
                  const(D_MODEL, ROW_COLS), const(*wa_bf.shape),
                  const(1, W_B), const(1, KV_W), const(W_B, W_B), const(KV_W, KV_W)],
        out_specs=a_specs + [rows(w) for w in widths],
        out_shape=a_shapes + [jax.ShapeDtypeStruct((m, w), dt) for w, dt in zip(widths, dtypes)],
        name="in_proj",
    )(x, norm_g.reshape(1, D_MODEL), scale, shift, w_bf, wa_bf, q_gain, k_gain, g512, g128)


def _prep_kernel(u_ref, prev_ref, shift0_ref, mu_ref, w0_ref, a0_ref, lora_ref, kk_ref, ka_ref, rk_ref,
                 r_o, w_o, k_o, v_o, kk_o, b_o, coef_o, *, per_token_state):
    tm = u_ref.shape[1]
    heads = lambda x: x.reshape(HEAD_SIZE, H_A, LANES)
    lane = lax.broadcasted_iota(jnp.int32, (SHIFT_W, LANES), 1)
    row = lax.broadcasted_iota(jnp.int32, (2 * LORA, LANES), 0)
    if not per_token_state:
        before = jnp.where(pl.program_id(1) == 0, shift0_ref[...], prev_ref[...])
        rolled_before = pltpu.roll(before, 1, axis=1)
    for ci in range(tm // LANES):
        cols = slice(ci * LANES, (ci + 1) * LANES)
        u = u_ref[:, cols]
        if per_token_state:
            u_prev = shift0_ref[:, cols]
        else:
            rolled = pltpu.roll(u, 1, axis=1)
            u_prev = jnp.where(lane == 0, rolled_before, rolled)
            rolled_before = rolled
        xs = u + (u_prev - u) * mu_ref[...]
        r = xs[0:W_A]
        k = xs[W_A:2 * W_A]
        v = xs[2 * W_A:3 * W_A]
        tail = xs[3 * W_A:SHIFT_W]
        lora_in = jnp.where(row < LORA, jnp.tanh(tail), tail).astype(BF16)
        up = _dot(lora_ref[...], lora_in)
        neg = -(w0_ref[...] + up[0:W_A])
        softplus = jnp.maximum(neg, 0.0) + jnp.log(1.0 + jnp.exp(-jnp.abs(neg)))
        w_log = -softplus - 0.5
        decay = jnp.exp(-jnp.exp(w_log))
        a = jax.nn.sigmoid(a0_ref[...] + up[W_A:2 * W_A])
        kk = heads(k * kk_ref[...])
        norm = jnp.sqrt(jnp.sum(kk * kk, axis=0, keepdims=True))
        kk = kk / jnp.maximum(norm, 1e-12)
        k_mod = k * (1.0 + (a - 1.0) * ka_ref[...])
        r_o[:, :, cols] = heads(r)
        w_o[:, :, cols] = heads(decay)
        k_o[:, :, cols] = heads(k_mod)
        v_o[:, :, cols] = heads(v)
        kk_o[:, :, cols] = kk
        b_o[:, :, cols] = kk * heads(a)
        coef_o[:, cols] = jnp.sum(heads(r * k_mod * rk_ref[...]), axis=0)


def _prep(usht, shift0t, mu, w0, a0, lorat_bf, k_k, k_a, r_k, tm, per_token_state):
    nseq, _, t_len = usht.shape
    col = lambda a: jnp.broadcast_to(a.reshape(-1, 1), (a.size, LANES))
    cvec = lambda n: pl.BlockSpec((n, LANES), lambda s, j: (0, 0))
    blk = lambda cols: pl.BlockSpec((None, cols, tm), lambda s, j: (s, 0, j))
    if per_token_state:
        prev_spec = pl.BlockSpec((None, SHIFT_W, LANES), lambda s, j: (0, 0, 0))
        shift0_spec = blk(SHIFT_W)
    else:
        prev_spec = pl.BlockSpec((None, SHIFT_W, LANES),
                                 lambda s, j: (s, 0, jnp.maximum(j * (tm // LANES) - 1, 0)))
        shift0_spec = pl.BlockSpec((None, SHIFT_W, LANES), lambda s, j: (s, 0, 0))
    chan = pl.BlockSpec((HEAD_SIZE, H_A, tm), lambda s, j: (0, s, j))
    return pl.pallas_call(
        functools.partial(_prep_kernel, per_token_state=per_token_state),
        grid=(nseq, t_len // tm),
        in_specs=[blk(SHIFT_W), prev_spec, shift0_spec, cvec(SHIFT_W), cvec(W_A), cvec(W_A),
                  pl.BlockSpec((2 * W_A, 2 * LORA), lambda s, j: (0, 0)), cvec(W_A), cvec(W_A), cvec(W_A)],
        out_specs=[chan] * 6 + [pl.BlockSpec((H_A, tm), lambda s, j: (s, j))],
        out_shape=[jax.ShapeDtypeStruct((HEAD_SIZE, nseq * H_A, t_len), F32)] * 6
                  + [jax.ShapeDtypeStruct((nseq * H_A, t_len), F32)],
        name="rwkv_prep",
    )(usht, usht, shift0t, col(mu), col(w0), col(a0), lorat_bf, col(k_k), col(k_a), col(r_k))


def _scan_kernel(*refs, tt, zero_init):
    if zero_init:
        (r_ref, w_ref, k_ref, v_ref, kk_ref, b_ref, coef_ref, lng_ref, lnb_ref,
         y_ref, pf_ref, p_ref, vs_ref, ys_ref) = refs
    else:
        (r_ref, w_ref, k_ref, v_ref, kk_ref, b_ref, coef_ref, lng_ref, lnb_ref, p0_ref,
         y_ref, pf_ref, p_ref, vs_ref, ys_ref) = refs
    ti = pl.program_id(1)
    vs_ref[...] = jnp.swapaxes(v_ref[...], 0, 1)

    @pl.when(ti == 0)
    def _():
        if zero_init:
            p_ref[...] = jnp.zeros(p_ref.shape, F32)
        else:
            p_ref[...] = p0_ref[...]

    def step(t, carry):
        zero = jnp.zeros((HEAD_SIZE, LANES), F32)
        row = pl.ds(t, 1)

        def reduce_keys(kb, part):
            part = list(part)
            for u in range(KEY_UNROLL):
                kc = kb * KEY_UNROLL + u
                part[u % 2] = part[u % 2] + p_ref[kc] * kk_ref[kc, row, :]
            return tuple(part)

        part = lax.fori_loop(0, HEAD_SIZE // KEY_UNROLL, reduce_keys, (zero, zero))
        sa = -(part[0] + part[1])
        vt = vs_ref[t]

        def update_keys(kb, out):
            for u in range(KEY_UNROLL):
                kc = kb * KEY_UNROLL + u
                p = p_ref[kc] * w_ref[kc, row, :] + sa * b_ref[kc, row, :] + vt * k_ref[kc, row, :]
                p_ref[kc] = p
                out = out + p * r_ref[kc, row, :]
            return out

        o = lax.fori_loop(0, HEAD_SIZE // KEY_UNROLL, update_keys, zero)
        mean = jnp.sum(o, axis=0, keepdims=True) * (1.0 / HEAD_SIZE)
        c = o - mean
        var = jnp.sum(c * c, axis=0, keepdims=True) * (1.0 / HEAD_SIZE)
        ys_ref[t] = c * lax.rsqrt(var + GN_EPS) * lng_ref[...] + lnb_ref[...] + coef_ref[row, :] * vt
        return carry

    lax.fori_loop(0, tt, step, 0)
    y_ref[...] = jnp.swapaxes(ys_ref[...], 0, 1)

    @pl.when(ti == pl.num_programs(1) - 1)
    def _():
        pf_ref[...] = p_ref[...]


def _scan(seqs, coef, tiles, p0, tt):
    _, t_len, n = seqs[0].shape
    seq_spec = pl.BlockSpec((HEAD_SIZE, tt, LANES), lambda j, i: (0, i, j))
    coef_spec = pl.BlockSpec((tt, LANES), lambda j, i: (i, j))
    tile_spec = pl.BlockSpec((HEAD_SIZE, LANES), lambda j, i: (0, j))
    state_spec = pl.BlockSpec((HEAD_SIZE, HEAD_SIZE, LANES), lambda j, i: (0, 0, j))
    zero_init = p0 is None
    args = list(seqs) + [coef] + list(tiles) + ([] if zero_init else [p0])
    return pl.pallas_call(
        functools.partial(_scan_kernel, tt=tt, zero_init=zero_init),
        grid=(n // LANES, t_len // tt),
        in_specs=[seq_spec] * 6 + [coef_spec] + [tile_spec] * 2 + ([] if zero_init else [state_spec]),
        out_specs=[seq_spec, state_spec],
        out_shape=[jax.ShapeDtypeStruct((HEAD_SIZE, t_len, n), F32),
                   jax.ShapeDtypeStruct((HEAD_SIZE, HEAD_SIZE, n), F32)],
        scratch_shapes=[pltpu.VMEM((HEAD_SIZE, HEAD_SIZE, LANES), F32),
                        pltpu.VMEM((tt, HEAD_SIZE, LANES), F32), pltpu.VMEM((tt, HEAD_SIZE, LANES), F32)],
        compiler_params=pltpu.CompilerParams(dimension_semantics=("arbitrary", "arbitrary")),
        name="rwkv_scan",
    )(*args)


def _prep_rows(u_ref, prev_ref, first, mu_ref, w0_ref, a0_ref, lora_ref, kk_ref, ka_ref, rk_ref, g_ref):
    u = u_ref[...]
    prev_row = jnp.where(first, 0.0, prev_ref[SUBLANES - 1:SUBLANES, :])
    row = lax.broadcasted_iota(jnp.int32, u.shape, 0)
    u_prev = jnp.where(row == 0, prev_row, pltpu.roll(u, 1, axis=0))
    xs = u + (u_prev - u) * mu_ref[...]
    r = xs[:, 0:W_A]
    k = xs[:, W_A:2 * W_A]
    v = xs[:, 2 * W_A:3 * W_A]
    tail = xs[:, 3 * W_A:SHIFT_W]
    lane = lax.broadcasted_iota(jnp.int32, tail.shape, 1)
    lora_in = jnp.where(lane < LORA, jnp.tanh(tail), tail).astype(BF16)
    up = _dot(lora_in, lora_ref[...])
    neg = -(w0_ref[...] + up[:, 0:W_A])
    softplus = jnp.maximum(neg, 0.0) + jnp.log(1.0 + jnp.exp(-jnp.abs(neg)))
    w_log = -softplus - 0.5
    a = jax.nn.sigmoid(a0_ref[...] + up[:, W_A:2 * W_A])
    kk = k * kk_ref[...]
    norm = jnp.sqrt(_head_sum(kk * kk, g_ref[...]))
    kk = kk / jnp.maximum(norm, 1e-12)
    k_mod = k * (1.0 + (a - 1.0) * ka_ref[...])
    lw = -jnp.exp(w_log)
    bonus = _head_sum(r * k_mod * rk_ref[...], g_ref[...]) * v
    return r, lw, k_mod, v, kk, kk * a, bonus


def _chunk_kernel(u_ref, prev_ref, mu_ref, w0_ref, a0_ref, lora_ref, kkp_ref, ka_ref, rk_ref, g_ref,
                  o_ref, bonus_ref, pf_ref, p_ref, *, nchunk):
    npair = W_A // PAIR
    first = pl.program_id(1) == 0

    @pl.when(first)
    def _():
        p_ref[...] = jnp.zeros(p_ref.shape, F32)

    seqs = _prep_rows(u_ref, prev_ref, first, mu_ref, w0_ref, a0_ref, lora_ref, kkp_ref, ka_ref, rk_ref, g_ref)
    bonus_ref[...] = seqs[6]

    ti = lax.broadcasted_iota(jnp.int32, (CHUNK, CHUNK), 0)
    tj = lax.broadcasted_iota(jnp.int32, (CHUNK, CHUNK), 1)
    strict = ti > tj
    incl = ti >= tj
    eye = jnp.where(ti == tj, 1.0, 0.0).astype(F32)
    tri = jnp.where(incl, 1.0, 0.0).astype(BF16)
    lane = lax.broadcasted_iota(jnp.int32, (CHUNK, PAIR), 1)
    head_a = lane < HEAD_SIZE
    pi = lax.broadcasted_iota(jnp.int32, (PAIR, PAIR), 0)
    pj = lax.broadcasted_iota(jnp.int32, (PAIR, PAIR), 1)
    same_head = (pi < HEAD_SIZE) == (pj < HEAD_SIZE)
    on_diag = pi == pj
    bf = lambda x: x.astype(BF16)
    cat = lambda a, b: jnp.concatenate([a, b], axis=1)

    items = [(pr, c) for pr in range(npair) for c in range(nchunk)]
    sels = (head_a, jnp.logical_not(head_a))
    pre = []
    for pr, c in items:
        r, lw, k, v, kk, b = (x[c * CHUNK:(c + 1) * CHUNK, pr * PAIR:(pr + 1) * PAIR] for x in seqs[:6])
        g = _dot(tri, bf(lw))
        g = g + _dot(tri, bf(lw - bf(lw).astype(F32)))
        g_end = g[CHUNK - 1:CHUNK, :]
        e_neg = jnp.exp(-g)
        e_end = jnp.exp(g_end - g)
        kkt = kk * jnp.exp(g - lw)
        pre.append(dict(kkt=kkt, rt=r * jnp.exp(g), kkt_b=bf(kkt), bt_b=bf(b * e_neg), kt_b=bf(k * e_neg),
                        v_b=bf(v), bh_b=bf(b * e_end), kh_b=bf(k * e_end), gamma_end=jnp.exp(g_end)))
    tris = []
    for it in pre:
        for sel in sels:
            lhs = bf(jnp.concatenate([jnp.where(sel, it["kkt"], 0.0), jnp.where(sel, it["rt"], 0.0)], axis=0))
            xb = _dot_nt(lhs, it["bt_b"])
            xk = _dot_nt(lhs, it["kt_b"])
            tris.append((jnp.where(strict, xb[0:CHUNK], 0.0), jnp.where(strict, xk[0:CHUNK], 0.0),
                         jnp.where(incl, xb[CHUNK:2 * CHUNK], 0.0), jnp.where(incl, xk[CHUNK:2 * CHUNK], 0.0)))
    t_inv = [eye - t[0] for t in tris]
    power = [t[0] for t in tris]
    for _ in range(NEUMANN_STEPS):
        power = [_dot(bf(x), bf(x)) for x in power]
        t_inv = [t + _dot(bf(t), bf(x)) for t, x in zip(t_inv, power)]
    owner = [it for it in pre for _ in sels]
    lkv = [_dot(bf(t[1]), it["v_b"]) for t, it in zip(tris, owner)]
    tx = [_dot(bf(t), cat(it["kkt_b"], bf(x))) for t, it, x in zip(t_inv, owner, lkv)]
    mx = [_dot(bf(t[2]), bf(x)) for t, x in zip(tris, tx)]
    mv = [_dot(bf(t[3]), it["v_b"]) for t, it in zip(tris, owner)]
    pick = lambda xa, xb: jnp.where(head_a, xa, xb)
    affine = []
    for n, it in enumerate(pre):
        ia, ib = 2 * n, 2 * n + 1
        w_all = pick(tx[ia][:, 0:PAIR], tx[ib][:, 0:PAIR])
        u_loc = pick(tx[ia][:, PAIR:2 * PAIR], tx[ib][:, PAIR:2 * PAIR])
        q_eff = it["rt"] - pick(mx[ia][:, 0:PAIR], mx[ib][:, 0:PAIR])
        o_loc = pick(mv[ia], mv[ib]) - pick(mx[ia][:, PAIR:2 * PAIR], mx[ib][:, PAIR:2 * PAIR])
        bx = _dot_tn(it["bh_b"], cat(bf(w_all), bf(u_loc)))
        kv = _dot_tn(it["kh_b"], it["v_b"])
        a_eff = jnp.where(on_diag, it["gamma_end"], 0.0) - jnp.where(same_head, bx[:, 0:PAIR], 0.0)
        p_loc = jnp.where(same_head, kv - bx[:, PAIR:2 * PAIR], 0.0)
        affine.append((bf(a_eff), p_loc, bf(q_eff), o_loc))

    state = [p_ref[pr] for pr in range(npair)]
    for c in range(nchunk):
        for pr in range(npair):
            a_eff, p_loc, q_eff, o_loc = affine[pr * nchunk + c]
            p_b = bf(state[pr])
            o_ref[c * CHUNK:(c + 1) * CHUNK, pr * PAIR:(pr + 1) * PAIR] = _dot(q_eff, p_b) + o_loc
            state[pr] = _dot(a_eff, p_b) + p_loc
    for pr in range(npair):
        p_ref[pr] = state[pr]

    @pl.when(pl.program_id(1) == pl.num_programs(1) - 1)
    def _():
        for pr in range(npair):
            pf_ref[2 * pr] = state[pr][0:HEAD_SIZE, 0:HEAD_SIZE].T
            pf_ref[2 * pr + 1] = state[pr][HEAD_SIZE:PAIR, HEAD_SIZE:PAIR].T


def _chunk_scan(ush, nseq, lw, g512, nchunk):
    m = ush.shape[0]
    tt = nchunk * CHUNK
    steps = m // (nseq * tt)
    row = lambda a: a.reshape(1, -1)
    vec = lambda n: pl.BlockSpec((1, n), lambda s, i: (0, 0))
    blk = lambda w: pl.BlockSpec((tt, w), lambda s, i: (s * steps + i, 0))
    return pl.pallas_call(
        functools.partial(_chunk_kernel, nchunk=nchunk),
        grid=(nseq, steps),
        in_specs=[blk(SHIFT_W),
                  pl.BlockSpec((SUBLANES, SHIFT_W),
                               lambda s, i: (jnp.maximum((s * steps + i) * (tt // SUBLANES) - 1, 0), 0)),
                  vec(SHIFT_W), vec(W_A), vec(W_A),
                  pl.BlockSpec((2 * LORA, 2 * W_A), lambda s, i: (0, 0)),
                  vec(W_A), vec(W_A), vec(W_A),
                  pl.BlockSpec((W_A, W_A), lambda s, i: (0, 0))],
        out_specs=[blk(W_A), blk(W_A),
                   pl.BlockSpec((None, H_A, HEAD_SIZE, HEAD_SIZE), lambda s, i: (s, 0, 0, 0))],
        out_shape=[jax.ShapeDtypeStruct((m, W_A), F32), jax.ShapeDtypeStruct((m, W_A), F32),
                   jax.ShapeDtypeStruct((nseq, H_A, HEAD_SIZE, HEAD_SIZE), F32)],
        scratch_shapes=[pltpu.VMEM((W_A // PAIR, PAIR, PAIR), F32)],
        compiler_params=pltpu.CompilerParams(dimension_semantics=("arbitrary", "arbitrary")),
        name="rwkv_chunk_scan",
    )(ush, ush, row(lw["mu"]), row(lw["w0"]), row(lw["a0"]), lw["lora"], row(lw["k_k"]), row(lw["k_a"]),
      row(lw["r_k"]), g512)


def _attn_kernel(cur_ref, prev_ref, zb_ref, gtab_ref, sink_ref, yb_ref, bias_ref, *, blocks):
    s = pl.program_id(1)
    rows = Q_PER_KV * WINDOW

    @pl.when((pl.program_id(0) == 0) & (s == 0))
    def _():
        qi = lax.broadcasted_iota(jnp.int32, (WINDOW, 2 * WINDOW), 0)
        kj = lax.broadcasted_iota(jnp.int32, (WINDOW, 2 * WINDOW), 1)
        dist = qi + WINDOW - kj
        band = (dist >= 0) & (dist <= WINDOW)
        for g in range(KV_HEADS):
            for j in range(Q_PER_KV):
                h = g * Q_PER_KV + j
                row = jnp.broadcast_to(gtab_ref[h:h + 1, :], (WINDOW, BIAS_SPAN))
                toe = pltpu.roll(row, 0, 1, stride=1, stride_axis=0)[:, WINDOW:BIAS_SPAN]
                bias_ref[g, j * WINDOW:(j + 1) * WINDOW, :] = jnp.where(band, toe, NEG_INF)

    cur = cur_ref[...]
    prev = prev_ref[...]
    k_all = jnp.concatenate([prev[:, W_B:W_B + KV_W], cur[:, W_B:W_B + KV_W]], axis=0)
    v_all = jnp.concatenate([prev[:, W_B + KV_W:QKV_W], cur[:, W_B + KV_W:QKV_W]], axis=0)
    lane = lax.broadcasted_iota(jnp.int32, k_all.shape, 1)
    zero = jnp.zeros_like(k_all)
    k_g = [jnp.where(lane < HEAD_DIM, k_all, zero), jnp.where(lane >= HEAD_DIM, k_all, zero)]
    v_g = [jnp.where(lane < HEAD_DIM, v_all, zero), jnp.where(lane >= HEAD_DIM, v_all, zero)]
    ones = jnp.ones((2 * WINDOW, LANES), BF16)
    kj = lax.broadcasted_iota(jnp.int32, (rows, 2 * WINDOW), 1)
    insts = [(i, g) for i in range(blocks) for g in range(KV_HEADS)]
    keys_of = lambda x, i: x[i * WINDOW:(i + 2) * WINDOW]
    q4 = [jnp.concatenate([cur[i * WINDOW:(i + 1) * WINDOW, j * LANES:(j + 1) * LANES] for j in range(Q_PER_KV)],
                          axis=0) for i in range(blocks)]
    logits = []
    for i, g in insts:
        lg = _dot_nt(q4[i], keys_of(k_g[g], i)) + bias_ref[g]
        if i == 0:
            lg = jnp.where((kj >= WINDOW) | (s > 0), lg, NEG_INF)
        logits.append(lg)
    tops = [jnp.maximum(jnp.max(lg, axis=-1, keepdims=True), sink_ref[g]) for (i, g), lg in zip(insts, logits)]
    probs = [jnp.exp(lg - m).astype(BF16) for lg, m in zip(logits, tops)]
    dens = [_dot(p, ones) + jnp.exp(sink_ref[g] - m) for (i, g), p, m in zip(insts, probs, tops)]
    outs = [_dot(p, keys_of(v_g[g], i)) / den for (i, g), p, den in zip(insts, probs, dens)]
    for i in range(blocks):
        acc = outs[KV_HEADS * i] + outs[KV_HEADS * i + 1]
        r0 = i * WINDOW
        for j in range(Q_PER_KV):
            zb = zb_ref[r0:r0 + WINDOW, j * LANES:(j + 1) * LANES]
            yb_ref[r0:r0 + WINDOW, j * LANES:(j + 1) * LANES] = (
                acc[j * WINDOW:(j + 1) * WINDOW] * _silu(zb)).astype(BF16)


def _attn(qkv, zb, nb_seq, gtab, sink_col):
    m = qkv.shape[0]
    nblk = m // (nb_seq * WINDOW)
    blocks = _pick((ATTN_BLOCKS, 2, 1), nblk)
    steps = nblk // blocks
    rows = Q_PER_KV * WINDOW
    return pl.pallas_call(
        functools.partial(_attn_kernel, blocks=blocks),
        grid=(nb_seq, steps),
        in_specs=[pl.BlockSpec((blocks * WINDOW, QKV_W), lambda b, s: (b * steps + s, 0)),
                  pl.BlockSpec((WINDOW, QKV_W), lambda b, s: (jnp.maximum((b * steps + s) * blocks - 1, 0), 0)),
                  pl.BlockSpec((blocks * WINDOW, W_B), lambda b, s: (b * steps + s, 0)),
                  pl.BlockSpec((H_B, BIAS_SPAN), lambda b, s: (0, 0)),
                  pl.BlockSpec((KV_HEADS, rows, 1), lambda b, s: (0, 0, 0))],
        out_specs=pl.BlockSpec((blocks * WINDOW, W_B), lambda b, s: (b * steps + s, 0)),
        out_shape=jax.ShapeDtypeStruct((m, W_B), BF16),
        scratch_shapes=[pltpu.VMEM((KV_HEADS, rows, 2 * WINDOW), F32)],
        compiler_params=pltpu.CompilerParams(dimension_semantics=("arbitrary", "arbitrary")),
        name="swa_prompt",
    )(qkv, qkv, zb, gtab, sink_col)


def _attn_step_kernel(q_ref, kn_ref, vn_ref, zb_ref, ck_ref, cv_ref, bias_ref, bias0_ref, sink_ref, g128_ref,
                      yb_ref, ko_ref, vo_ref):
    q = q_ref[...].astype(F32)
    kn = kn_ref[...]
    vn = vn_ref[...]
    ck = ck_ref[...]
    cv = cv_ref[...]
    bt = q.shape[0]
    for r in range(Q_PER_KV):
        qsel = q[:, r * LANES:(r + 1) * LANES]
        prod = (ck * qsel[:, None, :]).reshape(bt * WINDOW, KV_W)
        lg = _split_dot(prod, g128_ref[...]).reshape(bt, WINDOW, KV_W) + bias_ref[r]
        lgn = _split_dot(kn * qsel, g128_ref[...]) + bias0_ref[r]
        s = sink_ref[r]
        m = jnp.maximum(jnp.maximum(jnp.max(lg, axis=1), lgn), s)
        p = jnp.exp(lg - m[:, None, :])
        pn = jnp.exp(lgn - m)
        den = jnp.sum(p, axis=1) + pn + jnp.exp(s - m)
        o = (jnp.sum(p * cv, axis=1) + pn * vn) / den
        zb = zb_ref[:, r * LANES:(r + 1) * LANES]
        yb_ref[:, r * LANES:(r + 1) * LANES] = (o * _silu(zb)).astype(BF16)
    j = lax.broadcasted_iota(jnp.int32, ck.shape, 1)
    ko_ref[...] = jnp.where(j == WINDOW - 1, kn[:, None, :], pltpu.roll(ck, WINDOW - 1, axis=1))
    vo_ref[...] = jnp.where(j == WINDOW - 1, vn[:, None, :], pltpu.roll(cv, WINDOW - 1, axis=1))


def _attn_step(qkv, kv32, zb, cache_k, cache_v, bias_rows, bias0, sink_rows, g128, bt):
    nb = qkv.shape[0]
    cache_spec = pl.BlockSpec((bt, WINDOW, KV_W), lambda i: (i, 0, 0))
    return pl.pallas_call(
        _attn_step_kernel,
        grid=(nb // bt,),
        in_specs=[pl.BlockSpec((bt, W_B), lambda i: (i, 0)),
                  pl.BlockSpec((bt, KV_W), lambda i: (i, 0)),
                  pl.BlockSpec((bt, KV_W), lambda i: (i, 1)),
                  pl.BlockSpec((bt, W_B), lambda i: (i, 0)),
                  cache_spec, cache_spec,
                  pl.BlockSpec((Q_PER_KV, WINDOW, KV_W), lambda i: (0, 0, 0)),
                  pl.BlockSpec((Q_PER_KV, 1, KV_W), lambda i: (0, 0, 0)),
                  pl.BlockSpec((Q_PER_KV, 1, KV_W), lambda i: (0, 0, 0)),
                  pl.BlockSpec((KV_W, KV_W), lambda i: (0, 0))],
        out_specs=[pl.BlockSpec((bt, W_B), lambda i: (i, 0)), cache_spec, cache_spec],
        out_shape=[jax.ShapeDtypeStruct((nb, W_B), BF16),
                   jax.ShapeDtypeStruct(cache_k.shape, F32),
                   jax.ShapeDtypeStruct(cache_v.shape, F32)],
        name="swa_step",
    )(qkv, kv32, kv32, zb, cache_k, cache_v, bias_rows, bias0, sink_rows, g128)


def _merge_kernel(yt_ref, zat_ref, yb_ref, ga_ref, gb_ref, x_ref, gate_ref, woa_ref, wob_ref, wout_ref, out_ref):
    yt = yt_ref[...]
    yat = (yt.reshape(W_A, yt.shape[2]) * _silu(zat_ref[...])).astype(BF16)
    pa = _dot_tn(yat, woa_ref[...])
    pb = _dot(yb_ref[...], wob_ref[...])
    merged = _sigmoid(ga_ref[...]) * pa + _sigmoid(gb_ref[...]) * pb
    out_ref[...] = x_ref[...] + gate_ref[...] * _dot(merged.astype(BF16), wout_ref[...])


def _merge(yt, zat, yb, gates, x, gate, woa_bf, wob_bf, wout_bf, tm, rows_per_seq, per_row):
    m = x.shape[0]
    full = lambda c: pl.BlockSpec((tm, D_MODEL), lambda i: (i, c))
    per_seq = rows_per_seq // tm
    y_spec = pl.BlockSpec((HEAD_SIZE, H_A, tm), lambda i: (0, i // per_seq, i % per_seq))
    return pl.pallas_call(
        _merge_kernel,
        grid=(m // tm,),
        in_specs=[y_spec, _t_spec(W_A, tm, rows_per_seq),
                  pl.BlockSpec((tm, W_B), lambda i: (i, 0)), full(0), full(1), full(0),
                  _mod_spec(per_row, tm, rows_per_seq),
                  pl.BlockSpec((W_A, D_MODEL), lambda i: (0, 0)),
                  pl.BlockSpec((W_B, D_MODEL), lambda i: (0, 0)),
                  pl.BlockSpec((D_MODEL, D_MODEL), lambda i: (0, 0))],
        out_specs=full(0),
        out_shape=jax.ShapeDtypeStruct((m, D_MODEL), F32),
        name="merge_out",
    )(yt, zat, yb, gates, gates, x, gate, woa_bf, wob_bf, wout_bf)


def _merge_rows_kernel(o_ref, bonus_ref, za_ref, yb_ref, ga_ref, gb_ref, x_ref, gate_ref, lng_ref, lnb_ref,
                       g512_ref, woa_ref, wob_ref, wout_ref, out_ref):
    o = o_ref[...]
    mean = _head_sum(o, g512_ref[...]) * (1.0 / HEAD_SIZE)
    c = o - mean
    var = _head_sum(c * c, g512_ref[...]) * (1.0 / HEAD_SIZE)
    on = c * lax.rsqrt(var + GN_EPS) * lng_ref[...] + lnb_ref[...]
    ya = (on + bonus_ref[...]) * _silu(za_ref[...])
    pa = _dot(ya.astype(BF16), woa_ref[...])
    pb = _dot(yb_ref[...], wob_ref[...])
    merged = _sigmoid(ga_ref[...]) * pa + _sigmoid(gb_ref[...]) * pb
    out_ref[...] = x_ref[...] + gate_ref[...] * _dot(merged.astype(BF16), wout_ref[...])


def _merge_rows(o, bonus, za, yb, gates, x, gate, lnx_g, lnx_b, g512, woa_bf, wob_bf, wout_bf, tm, rows_per_seq):
    m = x.shape[0]
    half = pl.BlockSpec((tm, W_A), lambda i: (i, 0))
    full = lambda c: pl.BlockSpec((tm, D_MODEL), lambda i: (i, c))
    vec = pl.BlockSpec((1, W_A), lambda i: (0, 0))
    return pl.pallas_call(
        _merge_rows_kernel,
        grid=(m // tm,),
        in_specs=[half, half, half, half, full(0), full(1), full(0),
                  _mod_spec(False, tm, rows_per_seq), vec, vec,
                  pl.BlockSpec((W_A, W_A), lambda i: (0, 0)),
                  pl.BlockSpec((W_A, D_MODEL), lambda i: (0, 0)),
                  pl.BlockSpec((W_B, D_MODEL), lambda i: (0, 0)),
                  pl.BlockSpec((D_MODEL, D_MODEL), lambda i: (0, 0))],
        out_specs=full(0),
        out_shape=jax.ShapeDtypeStruct((m, D_MODEL), F32),
        name="merge_out_rows",
    )(o, bonus, za, yb, gates, gates, x, gate, lnx_g.reshape(1, W_A), lnx_b.reshape(1, W_A), g512,
      woa_bf, wob_bf, wout_bf)


def _t5_bucket(dist):
    max_exact = N_BUCKETS // 2
    d = jnp.maximum(dist, 0)
    log_ratio = jnp.log(jnp.maximum(d, 1).astype(F32) / max_exact) / math.log(MAX_DISTANCE / max_exact)
    large = jnp.minimum(max_exact + (log_ratio * (N_BUCKETS - max_exact)).astype(jnp.int32), N_BUCKETS - 1)
    return jnp.where(d < max_exact, d, large)


def _block_ones(n, blk):
    i = np.arange(n) // blk
    return jnp.asarray((i[:, None] == i[None, :]).astype(np.float32), dtype=BF16)


def _rwkv_sample(usht, shift0, state0, lw):
    nb = usht.shape[2]
    n = H_A * nb
    outs = _prep(usht, _shift_major(shift0, 1).T[None], lw["mu"], lw["w0"], lw["a0"], lw["lorat"], lw["k_k"],
                 lw["k_a"], lw["r_k"], nb, True)
    tile = lambda a: jnp.repeat(a.reshape(H_A, HEAD_SIZE).T, nb, axis=1)
    p0 = state0.transpose(3, 2, 1, 0).reshape(HEAD_SIZE, HEAD_SIZE, n)
    y, pf = _scan([a.reshape(HEAD_SIZE, 1, n) for a in outs[:6]], outs[6].reshape(1, n),
                  [tile(lw["lnx_g"]), tile(lw["lnx_b"])], p0, 1)
    wkv = pf.reshape(HEAD_SIZE, HEAD_SIZE, H_A, nb).transpose(3, 2, 1, 0)
    return y.reshape(HEAD_SIZE, H_A, nb), wkv


def kernel(x_prompt, x_sample, c_prompt, c_sample, state_wkv, state_shift, cache_k, cache_v, norm_g, w_ada, b_ada, w_in, mu_shift, w0, w_decay_up, a0, w_a_up, k_k, k_a, r_k, lnx_g, lnx_b, w_o_a, q_norm_g, k_norm_g, rel_bias, sinks, w_o_b, w_out):
    nbp, t_len, _ = x_prompt.shape
    nbs = x_sample.shape[0]
    depth = norm_g.shape[0]
    mp = nbp * t_len
    assert (nbp * H_A) % LANES == 0 and nbs % LANES == 0 and t_len % LANES == 0
    g512 = _block_ones(W_B, HEAD_DIM)
    g128 = _block_ones(KV_W, HEAD_DIM)

    gtab = rel_bias[_t5_bucket(2 * WINDOW - jnp.arange(BIAS_SPAN))].astype(F32).T
    bias_s = rel_bias[_t5_bucket(WINDOW - jnp.arange(WINDOW + 1))].astype(F32)
    pair = lambda a: jnp.concatenate([jnp.repeat(a[..., 0:Q_PER_KV, None], HEAD_DIM, axis=-1),
                                      jnp.repeat(a[..., Q_PER_KV:H_B, None], HEAD_DIM, axis=-1)], axis=-1)
    bias_rows = pair(bias_s[:WINDOW]).transpose(1, 0, 2)
    bias0 = pair(bias_s[WINDOW:]).transpose(1, 0, 2)

    c_all = jnp.concatenate([c_prompt, c_sample], axis=0)
    xp = x_prompt.reshape(mp, D_MODEL)
    xs = x_sample.reshape(nbs, D_MODEL)
    tm_p = _pick((256, 128), t_len)
    tm_in = _pick((512, 256, 128), t_len)
    outs = {k: [] for k in ("wkv_p", "shift_p", "kw_p", "vw_p", "wkv_s", "shift_s", "kw_s", "vw_s")}
    for l in range(depth):
        wl = w_in[l]
        base = SHIFT_W
        z_a, q, kb, vb, z_b, gts = (wl[:, base:base + 512], wl[:, base + 512:base + 1024],
                                    wl[:, base + 1024:base + 1152], wl[:, base + 1152:base + 1280],
                                    wl[:, base + 1280:base + 1792], wl[:, base + 1792:])
        w_bf = jnp.concatenate([_head_pairs(q, 1), kb, vb, _head_pairs(z_b, 1), gts], axis=1).astype(BF16)
        wa_bf = jnp.concatenate([wl[:, :SHIFT_W], z_a], axis=1).astype(BF16)
        wt_bf = jnp.concatenate([_shift_major(wl[:, :SHIFT_W], 1), _chan_major(z_a, 1)], axis=1).T.astype(BF16)
        zeros = jnp.zeros((LORA, W_A), F32)
        lora = jnp.concatenate([jnp.concatenate([w_decay_up[l], zeros], axis=1),
                                jnp.concatenate([zeros, w_a_up[l]], axis=1)], axis=0).astype(BF16)
        lorat = jnp.concatenate([jnp.concatenate([_chan_major(w_decay_up[l], 1), zeros], axis=1),
                                 jnp.concatenate([zeros, _chan_major(w_a_up[l], 1)], axis=1)],
                                axis=0).T.astype(BF16)
        cm = lambda a: _chan_major(a.reshape(-1), 0)
        lw_p = dict(mu=mu_shift[l], w0=w0[l], a0=a0[l], lora=lora, k_k=k_k[l], k_a=k_a[l], r_k=r_k[l].reshape(-1))
        lw = dict(mu=_shift_major(mu_shift[l], 0), w0=cm(w0[l]), a0=cm(a0[l]), lorat=lorat,
                  k_k=cm(k_k[l]), k_a=cm(k_a[l]), r_k=cm(r_k[l]), lnx_g=lnx_g[l], lnx_b=lnx_b[l])
        woa_rows_bf = w_o_a[l].astype(BF16)
        woa_bf, wout_bf = _chan_major(w_o_a[l], 0).astype(BF16), w_out[l].astype(BF16)
        wob_bf = _head_pairs(w_o_b[l], 0).astype(BF16)
        q_gain = jnp.tile(q_norm_g[l], H_B).reshape(1, W_B)
        k_gain = jnp.tile(k_norm_g[l], KV_HEADS).reshape(1, KV_W)
        sink_col = jnp.repeat(sinks[l].reshape(KV_HEADS, Q_PER_KV), WINDOW, axis=1).reshape(
            KV_HEADS, Q_PER_KV * WINDOW, 1)
        sink_rows = pair(sinks[l].reshape(1, H_B)).transpose(1, 0, 2)

        mod = _ada(c_all, w_ada[l].astype(BF16), b_ada[l])
        shift, scale, gate = mod[:, :D_MODEL], mod[:, D_MODEL:2 * D_MODEL], mod[:, 2 * D_MODEL:]

        sp, scp, gp = (a[:nbp].reshape(nbp, 1, D_MODEL) for a in (shift, scale, gate))
        ush, za, qkv, kv32, zb, gts_p = _in_proj(xp, norm_g[l], scp, sp, w_bf, wa_bf, q_gain, k_gain, g512, g128,
                                                 tm_in, t_len, False, False)
        o, bonus, wkv = _chunk_scan(ush, nbp, lw_p, g512, _pick((CHUNKS_PER_STEP, 1), t_len // CHUNK))
        yb = _attn(qkv, zb, nbp, gtab, sink_col)
        xp = _merge_rows(o, bonus, za, yb, gts_p, xp, gp, lnx_g[l], lnx_b[l], g512, woa_rows_bf, wob_bf, wout_bf,
                         tm_p, t_len)
        win = kv32.reshape(nbp, t_len, 2, KV_HEADS, HEAD_DIM)[:, t_len - WINDOW:]
        outs["wkv_p"].append(wkv)
        outs["shift_p"].append(ush.reshape(nbp, t_len, SHIFT_W)[:, t_len - 1])
        outs["kw_p"].append(win[:, :, 0])
        outs["vw_p"].append(win[:, :, 1])

        ss, scs, gs = shift[nbp:], scale[nbp:], gate[nbp:]
        usht, zat, qkv, kv32, zb, gts_s = _in_proj(xs, norm_g[l], scs, ss, w_bf, wt_bf, q_gain, k_gain, g512, g128,
                                                   nbs, nbs, True, True)
        yt, wkv = _rwkv_sample(usht, state_shift[l], state_wkv[l], lw)
        yb, kwin, vwin = _attn_step(qkv, kv32, zb, cache_k[l].reshape(nbs, WINDOW, KV_W),
                                    cache_v[l].reshape(nbs, WINDOW, KV_W),
                                    bias_rows, bias0, sink_rows, g128, _pick((16, 8), nbs))
        xs = _merge(yt, zat, yb, gts_s, xs, gs, woa_bf, wob_bf, wout_bf, nbs, nbs, True)
        outs["wkv_s"].append(wkv)
        outs["shift_s"].append(_shift_major(usht[0].T, 1, inverse=True))
        outs["kw_s"].append(kwin.reshape(nbs, WINDOW, KV_HEADS, HEAD_DIM))
        outs["vw_s"].append(vwin.reshape(nbs, WINDOW, KV_HEADS, HEAD_DIM))

    st = lambda k: jnp.stack(outs[k])
    return (xp.reshape(nbp, t_len, D_MODEL), xs.reshape(nbs, 1, D_MODEL),
            st("wkv_p"), st("shift_p"), st("kw_p"), st("vw_p"),
            st("wkv_s"), st("shift_s"), st("kw_s"), st("vw_s"))
```

```python
import functools
import math

import numpy as np
import jax
import jax.numpy as jnp
from jax import lax
from jax.experimental import pallas as pl
from jax.experimental.pallas import tpu as pltpu

F32 = jnp.float32
BF16 = jnp.bfloat16

D_MODEL = 1024
W_A = 512
HEAD_SIZE = 64
H_A = W_A // HEAD_SIZE
LORA = 64
SHIFT_W = 3 * W_A + 2 * LORA
W_B = 512
HEAD_DIM = 64
H_B = W_B // HEAD_DIM
KV_HEADS = 2
Q_PER_KV = H_B // KV_HEADS
KV_W = KV_HEADS * HEAD_DIM
WINDOW = 128
N_BUCKETS = 32
MAX_DISTANCE = 128
NORM_EPS = 1e-6
GN_EPS = 64e-5
NEG_INF = -1e30
QKV_W = W_B + 2 * KV_W
GATE_W = 2 * D_MODEL
ROW_COLS = QKV_W + W_B + GATE_W
T_COLS = SHIFT_W + W_A
LANES = 128
SUBLANES = 8
def _head_pairs(x, axis):
    s = x.shape
    x = x.reshape(s[:axis] + (KV_HEADS, Q_PER_KV, HEAD_DIM) + s[axis + 1:])
    return jnp.swapaxes(x, axis, axis + 1).reshape(s)


def _chan_major(x, axis, inverse=False):
    s = x.shape
    split = (HEAD_SIZE, H_A) if inverse else (H_A, HEAD_SIZE)
    x = x.reshape(s[:axis] + split + s[axis + 1:])
    return jnp.swapaxes(x, axis, axis + 1).reshape(s)


def _shift_major(x, axis, inverse=False):
    parts = [lax.slice_in_dim(x, i * W_A, (i + 1) * W_A, axis=axis) for i in range(3)]
    tail = lax.slice_in_dim(x, 3 * W_A, SHIFT_W, axis=axis)
    return jnp.concatenate([_chan_major(p, axis, inverse) for p in parts] + [tail], axis=axis)
ATTN_BLOCKS = 4
BIAS_SPAN = 3 * WINDOW
KEY_UNROLL = 32
CHUNK = 64
CHUNKS_PER_STEP = 8
PAIR = 2 * HEAD_SIZE
QUAD = 2 * PAIR
NEUMANN_STEPS = CHUNK.bit_length() - 2


def _dot(a, b):
    return jnp.dot(a, b, preferred_element_type=F32)


def _dot_nt(a, b):
    return lax.dot_general(a, b, (((1,), (1,)), ((), ())), preferred_element_type=F32)


def _dot_tn(a, b):
    return lax.dot_general(a, b, (((0,), (0,)), ((), ())), preferred_element_type=F32)


def _head_sum(x, g):
    return _dot(x.astype(BF16), g)


def _split_dot(x, g):
    hi = x.astype(BF16)
    lo = (x - hi.astype(F32)).astype(BF16)
    return _dot(hi, g) + _dot(lo, g)


def _silu(x):
    x = x.astype(F32)
    return x * jax.nn.sigmoid(x)


def _sigmoid(x):
    return jax.nn.sigmoid(x.astype(F32))


def _pick(cands, n):
    for c in cands:
        if n % c == 0:
            return c
    return n


def _ada_kernel(c_ref, w_ref, b_ref, o_ref):
    c = c_ref[...]
    o_ref[...] = _dot(_silu(c).astype(BF16), w_ref[...]) + b_ref[...]


def _ada(c, w_bf, b):
    nb, d = c.shape
    n = w_bf.shape[1]
    tn = 1024
    return pl.pallas_call(
        _ada_kernel,
        grid=(n // tn,),
        in_specs=[pl.BlockSpec((nb, d), lambda j: (0, 0)),
                  pl.BlockSpec((d, tn), lambda j: (0, j)),
                  pl.BlockSpec((1, tn), lambda j: (0, j))],
        out_specs=pl.BlockSpec((nb, tn), lambda j: (0, j)),
        out_shape=jax.ShapeDtypeStruct((nb, n), F32),
        name="ada",
    )(c, w_bf, b.reshape(1, n))


def _in_proj_kernel(x_ref, g_ref, scale_ref, shift_ref, w_ref, wa_ref, qg_ref, kg_ref, g512_ref, g128_ref,
                    ush_ref, za_ref, qkv_ref, kv32_ref, zb_ref, gate_ref, *, transposed):
    x = x_ref[...]
    ms = jnp.mean(x * x, axis=-1, keepdims=True)
    h = (x * lax.rsqrt(ms + NORM_EPS)) * g_ref[...]
    h = h * (1.0 + scale_ref[...]) + shift_ref[...]
    hb = h.astype(BF16)
    if transposed:
        ush_ref[...] = _dot_nt(wa_ref[0:SHIFT_W, :], hb)
        za_ref[...] = _dot_nt(wa_ref[SHIFT_W:T_COLS, :], hb).astype(BF16)
    else:
        ush_ref[...] = _dot(hb, wa_ref[:, 0:SHIFT_W])
        za_ref[...] = _dot(hb, wa_ref[:, SHIFT_W:T_COLS]).astype(BF16)
    qkv = _dot(hb, w_ref[:, 0:QKV_W])
    q, k, v = qkv[:, 0:W_B], qkv[:, W_B:W_B + KV_W], qkv[:, W_B + KV_W:QKV_W]
    qn = q * lax.rsqrt(_head_sum(q * q, g512_ref[...]) * (1.0 / HEAD_DIM) + NORM_EPS) * qg_ref[...]
    kn = k * lax.rsqrt(_head_sum(k * k, g128_ref[...]) * (1.0 / HEAD_DIM) + NORM_EPS) * kg_ref[...]
    qkv_ref[:, 0:W_B] = (qn * (HEAD_DIM ** -0.5)).astype(BF16)
    qkv_ref[:, W_B:W_B + KV_W] = kn.astype(BF16)
    qkv_ref[:, W_B + KV_W:QKV_W] = v.astype(BF16)
    kv32_ref[:, 0:KV_W] = kn
    kv32_ref[:, KV_W:2 * KV_W] = v
    zb_ref[...] = _dot(hb, w_ref[:, QKV_W:QKV_W + W_B]).astype(BF16)
    gate_ref[...] = _dot(hb, w_ref[:, QKV_W + W_B:ROW_COLS]).astype(BF16)


def _mod_spec(per_row, tm, rows_per_seq):
    if per_row:
        return pl.BlockSpec((tm, D_MODEL), lambda i: (i, 0))
    return pl.BlockSpec((None, 1, D_MODEL), lambda i: ((i * tm) // rows_per_seq, 0, 0))


def _t_spec(cols, tm, rows_per_seq):
    per_seq = rows_per_seq // tm
    return pl.BlockSpec((None, cols, tm), lambda i: (i // per_seq, 0, i % per_seq))


def _in_proj(x, norm_g, scale, shift, w_bf, wa_bf, q_gain, k_gain, g512, g128, tm, rows_per_seq, per_row,
             transposed):
    m = x.shape[0]
    nseq = m // rows_per_seq
    const = lambda r, c: pl.BlockSpec((r, c), lambda i: (0, 0))
    rows = lambda w: pl.BlockSpec((tm, w), lambda i: (i, 0))
    widths = (QKV_W, 2 * KV_W, W_B, GATE_W)
    dtypes = (BF16, F32, BF16, BF16)
    if transposed:
        a_specs = [_t_spec(SHIFT_W, tm, rows_per_seq), _t_spec(W_A, tm, rows_per_seq)]
        a_shapes = [jax.ShapeDtypeStruct((nseq, SHIFT_W, rows_per_seq), F32),
                    jax.ShapeDtypeStruct((nseq, W_A, rows_per_seq), BF16)]
    else:
        a_specs = [rows(SHIFT_W), rows(W_A)]
        a_shapes = [jax.ShapeDtypeStruct((m, SHIFT_W), F32), jax.ShapeDtypeStruct((m, W_A), BF16)]
    return pl.pallas_call(
        functools.partial(_in_proj_kernel, transposed=transposed),
        grid=(m // tm,),
        in_specs=[rows(D_MODEL),
                  const(1, D_MODEL),
                  _mod_spec(per_row, tm, rows_per_seq),
                  _mod_spec(per_row, tm, rows_per_seq),
                  const(D_MODEL, ROW_COLS), const(*wa_bf.shape),
                  const(1, W_B), const(1, KV_W), const(W_B, W_B), const(KV_W, KV_W)],
        out_specs=a_specs + [rows(w) for w in widths],
        out_shape=a_shapes + [jax.ShapeDtypeStruct((m, w), dt) for w, dt in zip(widths, dtypes)],
        name="in_proj",
    )(x, norm_g.reshape(1, D_MODEL), scale, shift, w_bf, wa_bf, q_gain, k_gain, g512, g128)


def _prep_kernel(u_ref, prev_ref, shift0_ref, mu_ref, w0_ref, a0_ref, lora_ref, kk_ref, ka_ref, rk_ref,
                 r_o, w_o, k_o, v_o, kk_o, b_o, coef_o, *, per_token_state):
    tm = u_ref.shape[1]
    heads = lambda x: x.reshape(HEAD_SIZE, H_A, LANES)
    lane = lax.broadcasted_iota(jnp.int32, (SHIFT_W, LANES), 1)
    row = lax.broadcasted_iota(jnp.int32, (2 * LORA, LANES), 0)
    if not per_token_state:
        before = jnp.where(pl.program_id(1) == 0, shift0_ref[...], prev_ref[...])
        rolled_before = pltpu.roll(before, 1, axis=1)
    for ci in range(tm // LANES):
        cols = slice(ci * LANES, (ci + 1) * LANES)
        u = u_ref[:, cols]
        if per_token_state:
            u_prev = shift0_ref[:, cols]
        else:
            rolled = pltpu.roll(u, 1, axis=1)
            u_prev = jnp.where(lane == 0, rolled_before, rolled)
            rolled_before = rolled
        xs = u + (u_prev - u) * mu_ref[...]
        r = xs[0:W_A]
        k = xs[W_A:2 * W_A]
        v = xs[2 * W_A:3 * W_A]
        tail = xs[3 * W_A:SHIFT_W]
        lora_in = jnp.where(row < LORA, jnp.tanh(tail), tail).astype(BF16)
        up = _dot(lora_ref[...], lora_in)
        neg = -(w0_ref[...] + up[0:W_A])
        softplus = jnp.maximum(neg, 0.0) + jnp.log(1.0 + jnp.exp(-jnp.abs(neg)))
        w_log = -softplus - 0.5
        decay = jnp.exp(-jnp.exp(w_log))
        a = jax.nn.sigmoid(a0_ref[...] + up[W_A:2 * W_A])
        kk = heads(k * kk_ref[...])
        norm = jnp.sqrt(jnp.sum(kk * kk, axis=0, keepdims=True))
        kk = kk / jnp.maximum(norm, 1e-12)
        k_mod = k * (1.0 + (a - 1.0) * ka_ref[...])
        r_o[:, :, cols] = heads(r)
        w_o[:, :, cols] = heads(decay)
        k_o[:, :, cols] = heads(k_mod)
        v_o[:, :, cols] = heads(v)
        kk_o[:, :, cols] = kk
        b_o[:, :, cols] = kk * heads(a)
        coef_o[:, cols] = jnp.sum(heads(r * k_mod * rk_ref[...]), axis=0)


def _prep(usht, shift0t, mu, w0, a0, lorat_bf, k_k, k_a, r_k, tm, per_token_state):
    nseq, _, t_len = usht.shape
    col = lambda a: jnp.broadcast_to(a.reshape(-1, 1), (a.size, LANES))
    cvec = lambda n: pl.BlockSpec((n, LANES), lambda s, j: (0, 0))
    blk = lambda cols: pl.BlockSpec((None, cols, tm), lambda s, j: (s, 0, j))
    if per_token_state:
        prev_spec = pl.BlockSpec((None, SHIFT_W, LANES), lambda s, j: (0, 0, 0))
        shift0_spec = blk(SHIFT_W)
    else:
        prev_spec = pl.BlockSpec((None, SHIFT_W, LANES),
                                 lambda s, j: (s, 0, jnp.maximum(j * (tm // LANES) - 1, 0)))
        shift0_spec = pl.BlockSpec((None, SHIFT_W, LANES), lambda s, j: (s, 0, 0))
    chan = pl.BlockSpec((HEAD_SIZE, H_A, tm), lambda s, j: (0, s, j))
    return pl.pallas_call(
        functools.partial(_prep_kernel, per_token_state=per_token_state),
        grid=(nseq, t_len // tm),
        in_specs=[blk(SHIFT_W), prev_spec, shift0_spec, cvec(SHIFT_W), cvec(W_A), cvec(W_A),
                  pl.BlockSpec((2 * W_A, 2 * LORA), lambda s, j: (0, 0)), cvec(W_A), cvec(W_A), cvec(W_A)],
        out_specs=[chan] * 6 + [pl.BlockSpec((H_A, tm), lambda s, j: (s, j))],
        out_shape=[jax.ShapeDtypeStruct((HEAD_SIZE, nseq * H_A, t_len), F32)] * 6
                  + [jax.ShapeDtypeStruct((nseq * H_A, t_len), F32)],
        name="rwkv_prep",
    )(usht, usht, shift0t, col(mu), col(w0), col(a0), lorat_bf, col(k_k), col(k_a), col(r_k))


def _scan_kernel(*refs, tt, zero_init):
    if zero_init:
        (r_ref, w_ref, k_ref, v_ref, kk_ref, b_ref, coef_ref, lng_ref, lnb_ref,
         y_ref, pf_ref, p_ref, vs_ref, ys_ref) = refs
    else:
        (r_ref, w_ref, k_ref, v_ref, kk_ref, b_ref, coef_ref, lng_ref, lnb_ref, p0_ref,
         y_ref, pf_ref, p_ref, vs_ref, ys_ref) = refs
    ti = pl.program_id(1)
    vs_ref[...] = jnp.swapaxes(v_ref[...], 0, 1)

    @pl.when(ti == 0)
    def _():
        if zero_init:
            p_ref[...] = jnp.zeros(p_ref.shape, F32)
        else:
            p_ref[...] = p0_ref[...]

    def step(t, carry):
        zero = jnp.zeros((HEAD_SIZE, LANES), F32)
        row = pl.ds(t, 1)

        def reduce_keys(kb, part):
            part = list(part)
            for u in range(KEY_UNROLL):
                kc = kb * KEY_UNROLL + u
                part[u % 2] = part[u % 2] + p_ref[kc] * kk_ref[kc, row, :]
            return tuple(part)

        part = lax.fori_loop(0, HEAD_SIZE // KEY_UNROLL, reduce_keys, (zero, zero))
        sa = -(part[0] + part[1])
        vt = vs_ref[t]

        def update_keys(kb, out):
            for u in range(KEY_UNROLL):
                kc = kb * KEY_UNROLL + u
                p = p_ref[kc] * w_ref[kc, row, :] + sa * b_ref[kc, row, :] + vt * k_ref[kc, row, :]
                p_ref[kc] = p
                out = out + p * r_ref[kc, row, :]
            return out

        o = lax.fori_loop(0, HEAD_SIZE // KEY_UNROLL, update_keys, zero)
        mean = jnp.sum(o, axis=0, keepdims=True) * (1.0 / HEAD_SIZE)
        c = o - mean
        var = jnp.sum(c * c, axis=0, keepdims=True) * (1.0 / HEAD_SIZE)
        ys_ref[t] = c * lax.rsqrt(var + GN_EPS) * lng_ref[...] + lnb_ref[...] + coef_ref[row, :] * vt
        return carry

    lax.fori_loop(0, tt, step, 0)
    y_ref[...] = jnp.swapaxes(ys_ref[...], 0, 1)

    @pl.when(ti == pl.num_programs(1) - 1)
    def _():
        pf_ref[...] = p_ref[...]


def _scan(seqs, coef, tiles, p0, tt):
    _, t_len, n = seqs[0].shape
    seq_spec = pl.BlockSpec((HEAD_SIZE, tt, LANES), lambda j, i: (0, i, j))
    coef_spec = pl.BlockSpec((tt, LANES), lambda j, i: (i, j))
    tile_spec = pl.BlockSpec((HEAD_SIZE, LANES), lambda j, i: (0, j))
    state_spec = pl.BlockSpec((HEAD_SIZE, HEAD_SIZE, LANES), lambda j, i: (0, 0, j))
    zero_init = p0 is None
    args = list(seqs) + [coef] + list(tiles) + ([] if zero_init else [p0])
    return pl.pallas_call(
        functools.partial(_scan_kernel, tt=tt, zero_init=zero_init),
        grid=(n // LANES, t_len // tt),
        in_specs=[seq_spec] * 6 + [coef_spec] + [tile_spec] * 2 + ([] if zero_init else [state_spec]),
        out_specs=[seq_spec, state_spec],
        out_shape=[jax.ShapeDtypeStruct((HEAD_SIZE, t_len, n), F32),
                   jax.ShapeDtypeStruct((HEAD_SIZE, HEAD_SIZE, n), F32)],
        scratch_shapes=[pltpu.VMEM((HEAD_SIZE, HEAD_SIZE, LANES), F32),
                        pltpu.VMEM((tt, HEAD_SIZE, LANES), F32), pltpu.VMEM((tt, HEAD_SIZE, LANES), F32)],
        compiler_params=pltpu.CompilerParams(dimension_semantics=("arbitrary", "arbitrary")),
        name="rwkv_scan",
    )(*args)


def _prep_rows(u_ref, prev_ref, first, mu_ref, w0_ref, a0_ref, lora_ref, kk_ref, ka_ref, rk_ref, g_ref):
    u = u_ref[...]
    prev_row = jnp.where(first, 0.0, prev_ref[SUBLANES - 1:SUBLANES, :])
    row = lax.broadcasted_iota(jnp.int32, u.shape, 0)
    u_prev = jnp.where(row == 0, prev_row, pltpu.roll(u, 1, axis=0))
    xs = u + (u_prev - u) * mu_ref[...]
    r = xs[:, 0:W_A]
    k = xs[:, W_A:2 * W_A]
    v = xs[:, 2 * W_A:3 * W_A]
    tail = xs[:, 3 * W_A:SHIFT_W]
    lane = lax.broadcasted_iota(jnp.int32, tail.shape, 1)
    lora_in = jnp.where(lane < LORA, jnp.tanh(tail), tail).astype(BF16)
    up = _dot(lora_in, lora_ref[...])
    neg = -(w0_ref[...] + up[:, 0:W_A])
    softplus = jnp.maximum(neg, 0.0) + jnp.log(1.0 + jnp.exp(-jnp.abs(neg)))
    w_log = -softplus - 0.5
    a = jax.nn.sigmoid(a0_ref[...] + up[:, W_A:2 * W_A])
    kk = k * kk_ref[...]
    norm = jnp.sqrt(_head_sum(kk * kk, g_ref[...]))
    kk = kk / jnp.maximum(norm, 1e-12)
    k_mod = k * (1.0 + (a - 1.0) * ka_ref[...])
    lw = -jnp.exp(w_log)
    bonus = _head_sum(r * k_mod * rk_ref[...], g_ref[...]) * v
    return r, lw, k_mod, v, kk, kk * a, bonus


def _chunk_kernel(u_ref, prev_ref, mu_ref, w0_ref, a0_ref, lora_ref, kkp_ref, ka_ref, rk_ref, g_ref,
                  o_ref, bonus_ref, pf_ref, p_ref, *, nchunk):
    npair = W_A // PAIR
    first = pl.program_id(1) == 0

    @pl.when(first)
    def _():
        p_ref[...] = jnp.zeros(p_ref.shape, F32)

    seqs = _prep_rows(u_ref, prev_ref, first, mu_ref, w0_ref, a0_ref, lora_ref, kkp_ref, ka_ref, rk_ref, g_ref)
    bonus_ref[...] = seqs[6]

    tq = lax.broadcasted_iota(jnp.int32, (CHUNK, QUAD), 0)
    lq = lax.broadcasted_iota(jnp.int32, (CHUNK, QUAD), 1)
    head_of = lq // HEAD_SIZE
    col = lq % HEAD_SIZE
    strict = tq > col
    incl = tq >= col
    eye = jnp.where(tq == col, 1.0, 0.0).astype(F32)
    ti = lax.broadcasted_iota(jnp.int32, (CHUNK, CHUNK), 0)
    tj = lax.broadcasted_iota(jnp.int32, (CHUNK, CHUNK), 1)
    tri = jnp.where(ti >= tj, 1.0, 0.0).astype(BF16)
    pi = lax.broadcasted_iota(jnp.int32, (PAIR, PAIR), 0)
    pj = lax.broadcasted_iota(jnp.int32, (PAIR, PAIR), 1)
    same_head = (pi < HEAD_SIZE) == (pj < HEAD_SIZE)
    on_diag = pi == pj
    bf = lambda x: x.astype(BF16)
    cat = lambda a, b: jnp.concatenate([a, b], axis=1)

    def spread(x):
        return jnp.concatenate([jnp.where(head_of == j, x, jnp.zeros_like(x)) for j in range(QUAD // HEAD_SIZE)],
                               axis=0)

    items = [(qd, c) for qd in range(W_A // QUAD) for c in range(nchunk)]
    pre = []
    for qd, c in items:
        r, lw, k, v, kk, b = (x[c * CHUNK:(c + 1) * CHUNK, qd * QUAD:(qd + 1) * QUAD] for x in seqs[:6])
        g = _dot(tri, bf(lw))
        g = g + _dot(tri, bf(lw - bf(lw).astype(F32)))
        g_end = g[CHUNK - 1:CHUNK, :]
        e_neg = jnp.exp(-g)
        e_end = jnp.exp(g_end - g)
        kkt = kk * jnp.exp(g - lw)
        rt = r * jnp.exp(g)
        pre.append(dict(rt=rt, lhs=bf(jnp.concatenate([kkt, rt], axis=0)), kkt_b=bf(kkt), bt_b=bf(b * e_neg),
                        kt_b=bf(k * e_neg), v_b=bf(v), bh_b=bf(b * e_end), kh_b=bf(k * e_end),
                        gamma_end=jnp.exp(g_end)))
    xb = [_dot_nt(it["lhs"], spread(it["bt_b"])) for it in pre]
    xk = [_dot_nt(it["lhs"], spread(it["kt_b"])) for it in pre]
    l_b = [jnp.where(strict, x[0:CHUNK], 0.0) for x in xb]
    l_k = [jnp.where(strict, x[0:CHUNK], 0.0) for x in xk]
    m_b = [jnp.where(incl, x[CHUNK:2 * CHUNK], 0.0) for x in xb]
    m_k = [jnp.where(incl, x[CHUNK:2 * CHUNK], 0.0) for x in xk]
    t_inv = [eye - x for x in l_b]
    power = l_b
    for _ in range(NEUMANN_STEPS):
        power = [_dot(bf(x), spread(bf(x))) for x in power]
        t_inv = [t + _dot(bf(t), spread(bf(x))) for t, x in zip(t_inv, power)]
    v_s = [spread(it["v_b"]) for it in pre]
    lkv = [_dot(bf(x), vs) for x, vs in zip(l_k, v_s)]
    tx = [_dot(bf(t), cat(spread(it["kkt_b"]), spread(bf(x)))) for t, it, x in zip(t_inv, pre, lkv)]
    mx = [_dot(bf(m), cat(spread(bf(x[:, 0:QUAD])), spread(bf(x[:, QUAD:2 * QUAD])))) for m, x in zip(m_b, tx)]
    mv = [_dot(bf(m), vs) for m, vs in zip(m_k, v_s)]
    affine = {}
    for (qd, c), it, x, y, z in zip(items, pre, tx, mx, mv):
        q_eff = it["rt"] - y[:, 0:QUAD]
        o_loc = z - y[:, QUAD:2 * QUAD]
        for half in range(QUAD // PAIR):
            lanes = slice(half * PAIR, (half + 1) * PAIR)
            wu = cat(bf(x[:, lanes]), bf(x[:, QUAD + half * PAIR:QUAD + (half + 1) * PAIR]))
            bx = _dot_tn(it["bh_b"][:, lanes], wu)
            kv = _dot_tn(it["kh_b"][:, lanes], it["v_b"][:, lanes])
            a_eff = jnp.where(on_diag, it["gamma_end"][:, lanes], 0.0) - jnp.where(same_head, bx[:, 0:PAIR], 0.0)
            p_loc = jnp.where(same_head, kv - bx[:, PAIR:2 * PAIR], 0.0)
            affine[(qd * (QUAD // PAIR) + half, c)] = (bf(a_eff), p_loc, bf(q_eff[:, lanes]), o_loc[:, lanes])

    state = [p_ref[pr] for pr in range(npair)]
    for c in range(nchunk):
        for pr in range(npair):
            a_eff, p_loc, q_eff, o_loc = affine[(pr, c)]
            p_b = bf(state[pr])
            o_ref[c * CHUNK:(c + 1) * CHUNK, pr * PAIR:(pr + 1) * PAIR] = _dot(q_eff, p_b) + o_loc
            state[pr] = _dot(a_eff, p_b) + p_loc
    for pr in range(npair):
        p_ref[pr] = state[pr]

    @pl.when(pl.program_id(1) == pl.num_programs(1) - 1)
    def _():
        for pr in range(npair):
            pf_ref[2 * pr] = state[pr][0:HEAD_SIZE, 0:HEAD_SIZE].T
            pf_ref[2 * pr + 1] = state[pr][HEAD_SIZE:PAIR, HEAD_SIZE:PAIR].T


def _chunk_scan(ush, nseq, lw, g512, nchunk):
    m = ush.shape[0]
    tt = nchunk * CHUNK
    steps = m // (nseq * tt)
    row = lambda a: a.reshape(1, -1)
    vec = lambda n: pl.BlockSpec((1, n), lambda s, i: (0, 0))
    blk = lambda w: pl.BlockSpec((tt, w), lambda s, i: (s * steps + i, 0))
    return pl.pallas_call(
        functools.partial(_chunk_kernel, nchunk=nchunk),
        grid=(nseq, steps),
        in_specs=[blk(SHIFT_W),
                  pl.BlockSpec((SUBLANES, SHIFT_W),
                               lambda s, i: (jnp.maximum((s * steps + i) * (tt // SUBLANES) - 1, 0), 0)),
                  vec(SHIFT_W), vec(W_A), vec(W_A),
                  pl.BlockSpec((2 * LORA, 2 * W_A), lambda s, i: (0, 0)),
                  vec(W_A), vec(W_A), vec(W_A),
                  pl.BlockSpec((W_A, W_A), lambda s, i: (0, 0))],
        out_specs=[blk(W_A), blk(W_A),
                   pl.BlockSpec((None, H_A, HEAD_SIZE, HEAD_SIZE), lambda s, i: (s, 0, 0, 0))],
        out_shape=[jax.ShapeDtypeStruct((m, W_A), F32), jax.ShapeDtypeStruct((m, W_A), F32),
                   jax.ShapeDtypeStruct((nseq, H_A, HEAD_SIZE, HEAD_SIZE), F32)],
        scratch_shapes=[pltpu.VMEM((W_A // PAIR, PAIR, PAIR), F32)],
        compiler_params=pltpu.CompilerParams(dimension_semantics=("arbitrary", "arbitrary")),
        name="rwkv_chunk_scan",
    )(ush, ush, row(lw["mu"]), row(lw["w0"]), row(lw["a0"]), lw["lora"], row(lw["k_k"]), row(lw["k_a"]),
      row(lw["r_k"]), g512)


def _attn_kernel(cur_ref, prev_ref, zb_ref, gtab_ref, sink_ref, yb_ref, bias_ref, *, blocks):
    s = pl.program_id(1)
    rows = Q_PER_KV * WINDOW

    @pl.when((pl.program_id(0) == 0) & (s == 0))
    def _():
        qi = lax.broadcasted_iota(jnp.int32, (WINDOW, 2 * WINDOW), 0)
        kj = lax.broadcasted_iota(jnp.int32, (WINDOW, 2 * WINDOW), 1)
        dist = qi + WINDOW - kj
        band = (dist >= 0) & (dist <= WINDOW)
        for g in range(KV_HEADS):
            for j in range(Q_PER_KV):
                h = g * Q_PER_KV + j
                row = jnp.broadcast_to(gtab_ref[h:h + 1, :], (WINDOW, BIAS_SPAN))
                toe = pltpu.roll(row, 0, 1, stride=1, stride_axis=0)[:, WINDOW:BIAS_SPAN]
                bias_ref[g, j * WINDOW:(j + 1) * WINDOW, :] = jnp.where(band, toe, NEG_INF)

    cur = cur_ref[...]
    prev = prev_ref[...]
    k_all = jnp.concatenate([prev[:, W_B:W_B + KV_W], cur[:, W_B:W_B + KV_W]], axis=0)
    v_all = jnp.concatenate([prev[:, W_B + KV_W:QKV_W], cur[:, W_B + KV_W:QKV_W]], axis=0)
    lane = lax.broadcasted_iota(jnp.int32, k_all.shape, 1)
    zero = jnp.zeros_like(k_all)
    k_g = [jnp.where(lane < HEAD_DIM, k_all, zero), jnp.where(lane >= HEAD_DIM, k_all, zero)]
    v_g = [jnp.where(lane < HEAD_DIM, v_all, zero), jnp.where(lane >= HEAD_DIM, v_all, zero)]
    ones = jnp.ones((2 * WINDOW, LANES), BF16)
    kj = lax.broadcasted_iota(jnp.int32, (rows, 2 * WINDOW), 1)
    insts = [(i, g) for i in range(blocks) for g in range(KV_HEADS)]
    keys_of = lambda x, i: x[i * WINDOW:(i + 2) * WINDOW]
    q4 = [jnp.concatenate([cur[i * WINDOW:(i + 1) * WINDOW, j * LANES:(j + 1) * LANES] for j in range(Q_PER_KV)],
                          axis=0) for i in range(blocks)]
    logits = []
    for i, g in insts:
        lg = _dot_nt(q4[i], keys_of(k_g[g], i)) + bias_ref[g]
        if i == 0:
            lg = jnp.where((kj >= WINDOW) | (s > 0), lg, NEG_INF)
        logits.append(lg)
    tops = [jnp.maximum(jnp.max(lg, axis=-1, keepdims=True), sink_ref[g]) for (i, g), lg in zip(insts, logits)]
    probs = [jnp.exp(lg - m).astype(BF16) for lg, m in zip(logits, tops)]
    dens = [_dot(p, ones) + jnp.exp(sink_ref[g] - m) for (i, g), p, m in zip(insts, probs, tops)]
    outs = [_dot(p, keys_of(v_g[g], i)) / den for (i, g), p, den in zip(insts, probs, dens)]
    for i in range(blocks):
        acc = outs[KV_HEADS * i] + outs[KV_HEADS * i + 1]
        r0 = i * WINDOW
        for j in range(Q_PER_KV):
            zb = zb_ref[r0:r0 + WINDOW, j * LANES:(j + 1) * LANES]
            yb_ref[r0:r0 + WINDOW, j * LANES:(j + 1) * LANES] = (
                acc[j * WINDOW:(j + 1) * WINDOW] * _silu(zb)).astype(BF16)


def _attn(qkv, zb, nb_seq, gtab, sink_col):
    m = qkv.shape[0]
    nblk = m // (nb_seq * WINDOW)
    blocks = _pick((ATTN_BLOCKS, 2, 1), nblk)
    steps = nblk // blocks
    rows = Q_PER_KV * WINDOW
    return pl.pallas_call(
        functools.partial(_attn_kernel, blocks=blocks),
        grid=(nb_seq, steps),
        in_specs=[pl.BlockSpec((blocks * WINDOW, QKV_W), lambda b, s: (b * steps + s, 0)),
                  pl.BlockSpec((WINDOW, QKV_W), lambda b, s: (jnp.maximum((b * steps + s) * blocks - 1, 0), 0)),
                  pl.BlockSpec((blocks * WINDOW, W_B), lambda b, s: (b * steps + s, 0)),
                  pl.BlockSpec((H_B, BIAS_SPAN), lambda b, s: (0, 0)),
                  pl.BlockSpec((KV_HEADS, rows, 1), lambda b, s: (0, 0, 0))],
        out_specs=pl.BlockSpec((blocks * WINDOW, W_B), lambda b, s: (b * steps + s, 0)),
        out_shape=jax.ShapeDtypeStruct((m, W_B), BF16),
        scratch_shapes=[pltpu.VMEM((KV_HEADS, rows, 2 * WINDOW), F32)],
        compiler_params=pltpu.CompilerParams(dimension_semantics=("arbitrary", "arbitrary")),
        name="swa_prompt",
    )(qkv, qkv, zb, gtab, sink_col)


def _attn_step_kernel(q_ref, kn_ref, vn_ref, zb_ref, ck_ref, cv_ref, bias_ref, bias0_ref, sink_ref, g128_ref,
                      yb_ref, ko_ref, vo_ref):
    q = q_ref[...].astype(F32)
    kn = kn_ref[...]
    vn = vn_ref[...]
    ck = ck_ref[...]
    cv = cv_ref[...]
    bt = q.shape[0]
    for r in range(Q_PER_KV):
        qsel = q[:, r * LANES:(r + 1) * LANES]
        prod = (ck * qsel[:, None, :]).reshape(bt * WINDOW, KV_W)
        lg = _split_dot(prod, g128_ref[...]).reshape(bt, WINDOW, KV_W) + bias_ref[r]
        lgn = _split_dot(kn * qsel, g128_ref[...]) + bias0_ref[r]
        s = sink_ref[r]
        m = jnp.maximum(jnp.maximum(jnp.max(lg, axis=1), lgn), s)
        p = jnp.exp(lg - m[:, None, :])
        pn = jnp.exp(lgn - m)
        den = jnp.sum(p, axis=1) + pn + jnp.exp(s - m)
        o = (jnp.sum(p * cv, axis=1) + pn * vn) / den
        zb = zb_ref[:, r * LANES:(r + 1) * LANES]
        yb_ref[:, r * LANES:(r + 1) * LANES] = (o * _silu(zb)).astype(BF16)
    j = lax.broadcasted_iota(jnp.int32, ck.shape, 1)
    ko_ref[...] = jnp.where(j == WINDOW - 1, kn[:, None, :], pltpu.roll(ck, WINDOW - 1, axis=1))
    vo_ref[...] = jnp.where(j == WINDOW - 1, vn[:, None, :], pltpu.roll(cv, WINDOW - 1, axis=1))


def _attn_step(qkv, kv32, zb, cache_k, cache_v, bias_rows, bias0, sink_rows, g128, bt):
    nb = qkv.shape[0]
    cache_spec = pl.BlockSpec((bt, WINDOW, KV_W), lambda i: (i, 0, 0))
    return pl.pallas_call(
        _attn_step_kernel,
        grid=(nb // bt,),
        in_specs=[pl.BlockSpec((bt, W_B), lambda i: (i, 0)),
                  pl.BlockSpec((bt, KV_W), lambda i: (i, 0)),
                  pl.BlockSpec((bt, KV_W), lambda i: (i, 1)),
                  pl.BlockSpec((bt, W_B), lambda i: (i, 0)),
                  cache_spec, cache_spec,
                  pl.BlockSpec((Q_PER_KV, WINDOW, KV_W), lambda i: (0, 0, 0)),
                  pl.BlockSpec((Q_PER_KV, 1, KV_W), lambda i: (0, 0, 0)),
                  pl.BlockSpec((Q_PER_KV, 1, KV_W), lambda i: (0, 0, 0)),
                  pl.BlockSpec((KV_W, KV_W), lambda i: (0, 0))],
        out_specs=[pl.BlockSpec((bt, W_B), lambda i: (i, 0)), cache_spec, cache_spec],
        out_shape=[jax.ShapeDtypeStruct((nb, W_B), BF16),
                   jax.ShapeDtypeStruct(cache_k.shape, F32),
                   jax.ShapeDtypeStruct(cache_v.shape, F32)],
        name="swa_step",
    )(qkv, kv32, kv32, zb, cache_k, cache_v, bias_rows, bias0, sink_rows, g128)


def _merge_kernel(yt_ref, zat_ref, yb_ref, ga_ref, gb_ref, x_ref, gate_ref, woa_ref, wob_ref, wout_ref, out_ref):
    yt = yt_ref[...]
    yat = (yt.reshape(W_A, yt.shape[2]) * _silu(zat_ref[...])).astype(BF16)
    pa = _dot_tn(yat, woa_ref[...])
    pb = _dot(yb_ref[...], wob_ref[...])
    merged = _sigmoid(ga_ref[...]) * pa + _sigmoid(gb_ref[...]) * pb
    out_ref[...] = x_ref[...] + gate_ref[...] * _dot(merged.astype(BF16), wout_ref[...])


def _merge(yt, zat, yb, gates, x, gate, woa_bf, wob_bf, wout_bf, tm, rows_per_seq, per_row):
    m = x.shape[0]
    full = lambda c: pl.BlockSpec((tm, D_MODEL), lambda i: (i, c))
    per_seq = rows_per_seq // tm
    y_spec = pl.BlockSpec((HEAD_SIZE, H_A, tm), lambda i: (0, i // per_seq, i % per_seq))
    return pl.pallas_call(
        _merge_kernel,
        grid=(m // tm,),
        in_specs=[y_spec, _t_spec(W_A, tm, rows_per_seq),
                  pl.BlockSpec((tm, W_B), lambda i: (i, 0)), full(0), full(1), full(0),
                  _mod_spec(per_row, tm, rows_per_seq),
                  pl.BlockSpec((W_A, D_MODEL), lambda i: (0, 0)),
                  pl.BlockSpec((W_B, D_MODEL), lambda i: (0, 0)),
                  pl.BlockSpec((D_MODEL, D_MODEL), lambda i: (0, 0))],
        out_specs=full(0),
        out_shape=jax.ShapeDtypeStruct((m, D_MODEL), F32),
        name="merge_out",
    )(yt, zat, yb, gates, gates, x, gate, woa_bf, wob_bf, wout_bf)


def _merge_rows_kernel(o_ref, bonus_ref, za_ref, yb_ref, ga_ref, gb_ref, x_ref, gate_ref, lng_ref, lnb_ref,
                       g512_ref, woa_ref, wob_ref, wout_ref, out_ref):
    o = o_ref[...]
    mean = _head_sum(o, g512_ref[...]) * (1.0 / HEAD_SIZE)
    c = o - mean
    var = _head_sum(c * c, g512_ref[...]) * (1.0 / HEAD_SIZE)
    on = c * lax.rsqrt(var + GN_EPS) * lng_ref[...] + lnb_ref[...]
    ya = (on + bonus_ref[...]) * _silu(za_ref[...])
    pa = _dot(ya.astype(BF16), woa_ref[...])
    pb = _dot(yb_ref[...], wob_ref[...])
    merged = _sigmoid(ga_ref[...]) * pa + _sigmoid(gb_ref[...]) * pb
    out_ref[...] = x_ref[...] + gate_ref[...] * _dot(merged.astype(BF16), wout_ref[...])


def _merge_rows(o, bonus, za, yb, gates, x, gate, lnx_g, lnx_b, g512, woa_bf, wob_bf, wout_bf, tm, rows_per_seq):
    m = x.shape[0]
    half = pl.BlockSpec((tm, W_A), lambda i: (i, 0))
    full = lambda c: pl.BlockSpec((tm, D_MODEL), lambda i: (i, c))
    vec = pl.BlockSpec((1, W_A), lambda i: (0, 0))
    return pl.pallas_call(
        _merge_rows_kernel,
        grid=(m // tm,),
        in_specs=[half, half, half, half, full(0), full(1), full(0),
                  _mod_spec(False, tm, rows_per_seq), vec, vec,
                  pl.BlockSpec((W_A, W_A), lambda i: (0, 0)),
                  pl.BlockSpec((W_A, D_MODEL), lambda i: (0, 0)),
                  pl.BlockSpec((W_B, D_MODEL), lambda i: (0, 0)),
                  pl.BlockSpec((D_MODEL, D_MODEL), lambda i: (0, 0))],
        out_specs=full(0),
        out_shape=jax.ShapeDtypeStruct((m, D_MODEL), F32),
        name="merge_out_rows",
    )(o, bonus, za, yb, gates, gates, x, gate, lnx_g.reshape(1, W_A), lnx_b.reshape(1, W_A), g512,
      woa_bf, wob_bf, wout_bf)


def _t5_bucket(dist):
    max_exact = N_BUCKETS // 2
    d = jnp.maximum(dist, 0)
    log_ratio = jnp.log(jnp.maximum(d, 1).astype(F32) / max_exact) / math.log(MAX_DISTANCE / max_exact)
    large = jnp.minimum(max_exact + (log_ratio * (N_BUCKETS - max_exact)).astype(jnp.int32), N_BUCKETS - 1)
    return jnp.where(d < max_exact, d, large)


def _block_ones(n, blk):
    i = np.arange(n) // blk
    return jnp.asarray((i[:, None] == i[None, :]).astype(np.float32), dtype=BF16)


def _rwkv_sample(usht, shift0, state0, lw):
    nb = usht.shape[2]
    n = H_A * nb
    outs = _prep(usht, _shift_major(shift0, 1).T[None], lw["mu"], lw["w0"], lw["a0"], lw["lorat"], lw["k_k"],
                 lw["k_a"], lw["r_k"], nb, True)
    tile = lambda a: jnp.repeat(a.reshape(H_A, HEAD_SIZE).T, nb, axis=1)
    p0 = state0.transpose(3, 2, 1, 0).reshape(HEAD_SIZE, HEAD_SIZE, n)
    y, pf = _scan([a.reshape(HEAD_SIZE, 1, n) for a in outs[:6]], outs[6].reshape(1, n),
                  [tile(lw["lnx_g"]), tile(lw["lnx_b"])], p0, 1)
    wkv = pf.reshape(HEAD_SIZE, HEAD_SIZE, H_A, nb).transpose(3, 2, 1, 0)
    return y.reshape(HEAD_SIZE, H_A, nb), wkv


def kernel(x_prompt, x_sample, c_prompt, c_sample, state_wkv, state_shift, cache_k, cache_v, norm_g, w_ada, b_ada, w_in, mu_shift, w0, w_decay_up, a0, w_a_up, k_k, k_a, r_k, lnx_g, lnx_b, w_o_a, q_norm_g, k_norm_g, rel_bias, sinks, w_o_b, w_out):
    nbp, t_len, _ = x_prompt.shape
    nbs = x_sample.shape[0]
    depth = norm_g.shape[0]
    mp = nbp * t_len
    assert (nbp * H_A) % LANES == 0 and nbs % LANES == 0 and t_len % LANES == 0
    g512 = _block_ones(W_B, HEAD_DIM)
    g128 = _block_ones(KV_W, HEAD_DIM)

    gtab = rel_bias[_t5_bucket(2 * WINDOW - jnp.arange(BIAS_SPAN))].astype(F32).T
    bias_s = rel_bias[_t5_bucket(WINDOW - jnp.arange(WINDOW + 1))].astype(F32)
    pair = lambda a: jnp.concatenate([jnp.repeat(a[..., 0:Q_PER_KV, None], HEAD_DIM, axis=-1),
                                      jnp.repeat(a[..., Q_PER_KV:H_B, None], HEAD_DIM, axis=-1)], axis=-1)
    bias_rows = pair(bias_s[:WINDOW]).transpose(1, 0, 2)
    bias0 = pair(bias_s[WINDOW:]).transpose(1, 0, 2)

    c_all = jnp.concatenate([c_prompt, c_sample], axis=0)
    xp = x_prompt.reshape(mp, D_MODEL)
    xs = x_sample.reshape(nbs, D_MODEL)
    tm_p = _pick((512, 256, 128), t_len)
    tm_in = _pick((512, 256, 128), t_len)
    outs = {k: [] for k in ("wkv_p", "shift_p", "kw_p", "vw_p", "wkv_s", "shift_s", "kw_s", "vw_s")}
    for l in range(depth):
        wl = w_in[l].astype(BF16)
        base = SHIFT_W
        z_a, q, kb, vb, z_b, gts = (wl[:, base:base + 512], wl[:, base + 512:base + 1024],
                                    wl[:, base + 1024:base + 1152], wl[:, base + 1152:base + 1280],
                                    wl[:, base + 1280:base + 1792], wl[:, base + 1792:])
        w_bf = jnp.concatenate([_head_pairs(q, 1), kb, vb, _head_pairs(z_b, 1), gts], axis=1)
        wa_bf = jnp.concatenate([wl[:, :SHIFT_W], z_a], axis=1)
        wt_bf = jnp.concatenate([_shift_major(wl[:, :SHIFT_W], 1), _chan_major(z_a, 1)], axis=1).T
        zeros = jnp.zeros((LORA, W_A), F32)
        lora = jnp.concatenate([jnp.concatenate([w_decay_up[l], zeros], axis=1),
                                jnp.concatenate([zeros, w_a_up[l]], axis=1)], axis=0).astype(BF16)
        lorat = jnp.concatenate([jnp.concatenate([_chan_major(w_decay_up[l], 1), zeros], axis=1),
                                 jnp.concatenate([zeros, _chan_major(w_a_up[l], 1)], axis=1)],
                                axis=0).T.astype(BF16)
        cm = lambda a: _chan_major(a.reshape(-1), 0)
        lw_p = dict(mu=mu_shift[l], w0=w0[l], a0=a0[l], lora=lora, k_k=k_k[l], k_a=k_a[l], r_k=r_k[l].reshape(-1))
        lw = dict(mu=_shift_major(mu_shift[l], 0), w0=cm(w0[l]), a0=cm(a0[l]), lorat=lorat,
                  k_k=cm(k_k[l]), k_a=cm(k_a[l]), r_k=cm(r_k[l]), lnx_g=lnx_g[l], lnx_b=lnx_b[l])
        woa_rows_bf = w_o_a[l].astype(BF16)
        woa_bf, wout_bf = _chan_major(woa_rows_bf, 0), w_out[l].astype(BF16)
        wob_bf = _head_pairs(w_o_b[l].astype(BF16), 0)
        q_gain = jnp.tile(q_norm_g[l], H_B).reshape(1, W_B)
        k_gain = jnp.tile(k_norm_g[l], KV_HEADS).reshape(1, KV_W)
        sink_col = jnp.repeat(sinks[l].reshape(KV_HEADS, Q_PER_KV), WINDOW, axis=1).reshape(
            KV_HEADS, Q_PER_KV * WINDOW, 1)
        sink_rows = pair(sinks[l].reshape(1, H_B)).transpose(1, 0, 2)

        mod = _ada(c_all, w_ada[l].astype(BF16), b_ada[l])
        shift, scale, gate = mod[:, :D_MODEL], mod[:, D_MODEL:2 * D_MODEL], mod[:, 2 * D_MODEL:]

        sp, scp, gp = (a[:nbp].reshape(nbp, 1, D_MODEL) for a in (shift, scale, gate))
        ush, za, qkv, kv32, zb, gts_p = _in_proj(xp, norm_g[l], scp, sp, w_bf, wa_bf, q_gain, k_gain, g512, g128,
                                                 tm_in, t_len, False, False)
        o, bonus, wkv = _chunk_scan(ush, nbp, lw_p, g512, _pick((CHUNKS_PER_STEP, 1), t_len // CHUNK))
        yb = _attn(qkv, zb, nbp, gtab, sink_col)
        xp = _merge_rows(o, bonus, za, yb, gts_p, xp, gp, lnx_g[l], lnx_b[l], g512, woa_rows_bf, wob_bf, wout_bf,
                         tm_p, t_len)
        win = kv32.reshape(nbp, t_len, 2, KV_HEADS, HEAD_DIM)[:, t_len - WINDOW:]
        outs["wkv_p"].append(wkv)
        outs["shift_p"].append(ush.reshape(nbp, t_len, SHIFT_W)[:, t_len - 1])
        outs["kw_p"].append(win[:, :, 0])
        outs["vw_p"].append(win[:, :, 1])

        ss, scs, gs = shift[nbp:], scale[nbp:], gate[nbp:]
        usht, zat, qkv, kv32, zb, gts_s = _in_proj(xs, norm_g[l], scs, ss, w_bf, wt_bf, q_gain, k_gain, g512, g128,
                                                   nbs, nbs, True, True)
        yt, wkv = _rwkv_sample(usht, state_shift[l], state_wkv[l], lw)
        yb, kwin, vwin = _attn_step(qkv, kv32, zb, cache_k[l].reshape(nbs, WINDOW, KV_W),
                                    cache_v[l].reshape(nbs, WINDOW, KV_W),
                                    bias_rows, bias0, sink_rows, g128, _pick((16, 8), nbs))
        xs = _merge(yt, zat, yb, gts_s, xs, gs, woa_bf, wob_bf, wout_bf, nbs, nbs, True)
        outs["wkv_s"].append(wkv)
        outs["shift_s"].append(_shift_major(usht[0].T, 1, inverse=True))
        outs["kw_s"].append(kwin.reshape(nbs, WINDOW, KV_HEADS, HEAD_DIM))
        outs["vw_s"].append(vwin.reshape(nbs, WINDOW, KV_HEADS, HEAD_DIM))

    st = lambda k: jnp.stack(outs[k])
    return (xp.reshape(nbp, t_len, D_MODEL), xs.reshape(nbs, 1, D_MODEL),
            st("wkv_p"), st("shift_p"), st("kw_p"), st("vw_p"),
            st("wkv_s"), st("shift_s"), st("kw_s"), st("vw_s"))
```

```python
import functools
import math

import numpy as np
import jax
import jax.numpy as jnp
from jax import lax
from jax.experimental import pallas as pl
from jax.experimental.pallas import tpu as pltpu

F32 = jnp.float32
BF16 = jnp.bfloat16

D_MODEL = 1024
W_A = 512
HEAD_SIZE = 64
H_A = W_A // HEAD_SIZE
LORA = 64
SHIFT_W = 3 * W_A + 2 * LORA
W_B = 512
HEAD_DIM = 64
H_B = W_B // HEAD_DIM
KV_HEADS = 2
Q_PER_KV = H_B // KV_HEADS
KV_W = KV_HEADS * HEAD_DIM
WINDOW = 128
N_BUCKETS = 32
MAX_DISTANCE = 128
NORM_EPS = 1e-6
GN_EPS = 64e-5
NEG_INF = -1e30
QKV_W = W_B + 2 * KV_W
GATE_W = 2 * D_MODEL
ROW_COLS = QKV_W + W_B + GATE_W
T_COLS = SHIFT_W + W_A
LANES = 128
SUBLANES = 8
def _head_pairs(x, axis):
    s = x.shape
    x = x.reshape(s[:axis] + (KV_HEADS, Q_PER_KV, HEAD_DIM) + s[axis + 1:])
    return jnp.swapaxes(x, axis, axis + 1).reshape(s)


def _chan_major(x, axis, inverse=False):
    s = x.shape
    split = (HEAD_SIZE, H_A) if inverse else (H_A, HEAD_SIZE)
    x = x.reshape(s[:axis] + split + s[axis + 1:])
    return jnp.swapaxes(x, axis, axis + 1).reshape(s)


def _shift_major(x, axis, inverse=False):
    parts = [lax.slice_in_dim(x, i * W_A, (i + 1) * W_A, axis=axis) for i in range(3)]
    tail = lax.slice_in_dim(x, 3 * W_A, SHIFT_W, axis=axis)
    return jnp.concatenate([_chan_major(p, axis, inverse) for p in parts] + [tail], axis=axis)
ATTN_BLOCKS = 4
BIAS_SPAN = 3 * WINDOW
CHUNK = 64
CHUNKS_PER_STEP = 4
PAIR = 2 * HEAD_SIZE
NEUMANN_STEPS = CHUNK.bit_length() - 2


def _dot(a, b):
    return jnp.dot(a, b, preferred_element_type=F32)


def _dot_nt(a, b):
    return lax.dot_general(a, b, (((1,), (1,)), ((), ())), preferred_element_type=F32)


def _dot_tn(a, b):
    return lax.dot_general(a, b, (((0,), (0,)), ((), ())), preferred_element_type=F32)


def _head_sum(x, g):
    return _dot(x.astype(BF16), g)


def _split_dot(x, g):
    hi = x.astype(BF16)
    lo = (x - hi.astype(F32)).astype(BF16)
    return _dot(hi, g) + _dot(lo, g)


def _silu(x):
    x = x.astype(F32)
    return x * jax.nn.sigmoid(x)


def _sigmoid(x):
    return jax.nn.sigmoid(x.astype(F32))


def _pick(cands, n):
    for c in cands:
        if n % c == 0:
            return c
    return n


def _ada_kernel(c_ref, w_ref, b_ref, o_ref):
    c = c_ref[...]
    o_ref[...] = _dot(_silu(c).astype(BF16), w_ref[...]) + b_ref[...]


def _ada(c, w_bf, b):
    nb, d = c.shape
    n = w_bf.shape[1]
    tn = 1024
    return pl.pallas_call(
        _ada_kernel,
        grid=(n // tn,),
        in_specs=[pl.BlockSpec((nb, d), lambda j: (0, 0)),
                  pl.BlockSpec((d, tn), lambda j: (0, j)),
                  pl.BlockSpec((1, tn), lambda j: (0, j))],
        out_specs=pl.BlockSpec((nb, tn), lambda j: (0, j)),
        out_shape=jax.ShapeDtypeStruct((nb, n), F32),
        name="ada",
    )(c, w_bf, b.reshape(1, n))


def _in_proj_kernel(x_ref, g_ref, scale_ref, shift_ref, w_ref, wa_ref, qg_ref, kg_ref, g512_ref, g128_ref,
                    ush_ref, za_ref, qkv_ref, kv32_ref, zb_ref, gate_ref, *, transposed):
    x = x_ref[...]
    ms = jnp.mean(x * x, axis=-1, keepdims=True)
    h = (x * lax.rsqrt(ms + NORM_EPS)) * g_ref[...]
    h = h * (1.0 + scale_ref[...]) + shift_ref[...]
    hb = h.astype(BF16)
    if transposed:
        ush_ref[...] = _dot_nt(wa_ref[0:SHIFT_W, :], hb)
        za_ref[...] = _dot_nt(wa_ref[SHIFT_W:T_COLS, :], hb).astype(BF16)
    else:
        ush_ref[...] = _dot(hb, wa_ref[:, 0:SHIFT_W])
        za_ref[...] = _dot(hb, wa_ref[:, SHIFT_W:T_COLS]).astype(BF16)
    qkv = _dot(hb, w_ref[:, 0:QKV_W])
    q, k, v = qkv[:, 0:W_B], qkv[:, W_B:W_B + KV_W], qkv[:, W_B + KV_W:QKV_W]
    qn = q * lax.rsqrt(_head_sum(q * q, g512_ref[...]) * (1.0 / HEAD_DIM) + NORM_EPS) * qg_ref[...]
    kn = k * lax.rsqrt(_head_sum(k * k, g128_ref[...]) * (1.0 / HEAD_DIM) + NORM_EPS) * kg_ref[...]
    qkv_ref[:, 0:W_B] = (qn * (HEAD_DIM ** -0.5)).astype(BF16)
    qkv_ref[:, W_B:W_B + KV_W] = kn.astype(BF16)
    qkv_ref[:, W_B + KV_W:QKV_W] = v.astype(BF16)
    kv32_ref[:, 0:KV_W] = kn
    kv32_ref[:, KV_W:2 * KV_W] = v
    zb_ref[...] = _dot(hb, w_ref[:, QKV_W:QKV_W + W_B]).astype(BF16)
    gate_ref[...] = _dot(hb, w_ref[:, QKV_W + W_B:ROW_COLS]).astype(BF16)


def _mod_spec(per_row, tm, rows_per_seq):
    if per_row:
        return pl.BlockSpec((tm, D_MODEL), lambda i: (i, 0))
    return pl.BlockSpec((None, 1, D_MODEL), lambda i: ((i * tm) // rows_per_seq, 0, 0))


def _t_spec(cols, tm, rows_per_seq):
    per_seq = rows_per_seq // tm
    return pl.BlockSpec((None, cols, tm), lambda i: (i // per_seq, 0, i % per_seq))


def _in_proj(x, norm_g, scale, shift, w_bf, wa_bf, q_gain, k_gain, g512, g128, tm, rows_per_seq, per_row,
             transposed):
    m = x.shape[0]
    nseq = m // rows_per_seq
    const = lambda r, c: pl.BlockSpec((r, c), lambda i: (0, 0))
    rows = lambda w: pl.BlockSpec((tm, w), lambda i: (i, 0))
    widths = (QKV_W, 2 * KV_W, W_B, GATE_W)
    dtypes = (BF16, F32, BF16, BF16)
    if transposed:
        a_specs = [_t_spec(SHIFT_W, tm, rows_per_seq), _t_spec(W_A, tm, rows_per_seq)]
        a_shapes = [jax.ShapeDtypeStruct((nseq, SHIFT_W, rows_per_seq), F32),
                    jax.ShapeDtypeStruct((nseq, W_A, rows_per_seq), BF16)]
    else:
        a_specs = [rows(SHIFT_W), rows(W_A)]
        a_shapes = [jax.ShapeDtypeStruct((m, SHIFT_W), F32), jax.ShapeDtypeStruct((m, W_A), BF16)]
    return pl.pallas_call(
        functools.partial(_in_proj_kernel, transposed=transposed),
        grid=(m // tm,),
        in_specs=[rows(D_MODEL),
                  const(1, D_MODEL),
                  _mod_spec(per_row, tm, rows_per_seq),
                  _mod_spec(per_row, tm, rows_per_seq),
                  const(D_MODEL, ROW_COLS), const(*wa_bf.shape),
                  const(1, W_B), const(1, KV_W), const(W_B, W_B), const(KV_W, KV_W)],
        out_specs=a_specs + [rows(w) for w in widths],
        out_shape=a_shapes + [jax.ShapeDtypeStruct((m, w), dt) for w, dt in zip(widths, dtypes)],
        name="in_proj",
    )(x, norm_g.reshape(1, D_MODEL), scale, shift, w_bf, wa_bf, q_gain, k_gain, g512, g128)


def _prep_kernel(u_ref, prev_ref, shift0_ref, mu_ref, w0_ref, a0_ref, lora_ref, kk_ref, ka_ref, rk_ref,
                 r_o, w_o, k_o, v_o, kk_o, b_o, coef_o, *, per_token_state):
    tm = u_ref.shape[1]
    heads = lambda x: x.reshape(HEAD_SIZE, H_A, LANES)
    lane = lax.broadcasted_iota(jnp.int32, (SHIFT_W, LANES), 1)
    row = lax.broadcasted_iota(jnp.int32, (2 * LORA, LANES), 0)
    if not per_token_state:
        before = jnp.where(pl.program_id(1) == 0, shift0_ref[...], prev_ref[...])
        rolled_before = pltpu.roll(before, 1, axis=1)
    for ci in range(tm // LANES):
        cols = slice(ci * LANES, (ci + 1) * LANES)
        u = u_ref[:, cols]
        if per_token_state:
            u_prev = shift0_ref[:, cols]
        else:
            rolled = pltpu.roll(u, 1, axis=1)
            u_prev = jnp.where(lane == 0, rolled_before, rolled)
            rolled_before = rolled
        xs = u + (u_prev - u) * mu_ref[...]
        r = xs[0:W_A]
        k = xs[W_A:2 * W_A]
        v = xs[2 * W_A:3 * W_A]
        tail = xs[3 * W_A:SHIFT_W]
        lora_in = jnp.where(row < LORA, jnp.tanh(tail), tail).astype(BF16)
        up = _dot(lora_ref[...], lora_in)
        neg = -(w0_ref[...] + up[0:W_A])
        softplus = jnp.maximum(neg, 0.0) + jnp.log(1.0 + jnp.exp(-jnp.abs(neg)))
        w_log = -softplus - 0.5
        decay = jnp.exp(-jnp.exp(w_log))
        a = jax.nn.sigmoid(a0_ref[...] + up[W_A:2 * W_A])
        kk = heads(k * kk_ref[...])
        norm = jnp.sqrt(jnp.sum(kk * kk, axis=0, keepdims=True))
        kk = kk / jnp.maximum(norm, 1e-12)
        k_mod = k * (1.0 + (a - 1.0) * ka_ref[...])
        r_o[:, :, cols] = heads(r)
        w_o[:, :, cols] = heads(decay)
        k_o[:, :, cols] = heads(k_mod)
        v_o[:, :, cols] = heads(v)
        kk_o[:, :, cols] = kk
        b_o[:, :, cols] = kk * heads(a)
        coef_o[:, cols] = jnp.sum(heads(r * k_mod * rk_ref[...]), axis=0)


def _prep(usht, shift0t, mu, w0, a0, lorat_bf, k_k, k_a, r_k, tm, per_token_state):
    nseq, _, t_len = usht.shape
    col = lambda a: jnp.broadcast_to(a.reshape(-1, 1), (a.size, LANES))
    cvec = lambda n: pl.BlockSpec((n, LANES), lambda s, j: (0, 0))
    blk = lambda cols: pl.BlockSpec((None, cols, tm), lambda s, j: (s, 0, j))
    if per_token_state:
        prev_spec = pl.BlockSpec((None, SHIFT_W, LANES), lambda s, j: (0, 0, 0))
        shift0_spec = blk(SHIFT_W)
    else:
        prev_spec = pl.BlockSpec((None, SHIFT_W, LANES),
                                 lambda s, j: (s, 0, jnp.maximum(j * (tm // LANES) - 1, 0)))
        shift0_spec = pl.BlockSpec((None, SHIFT_W, LANES), lambda s, j: (s, 0, 0))
    chan = pl.BlockSpec((HEAD_SIZE, H_A, tm), lambda s, j: (0, s, j))
    return pl.pallas_call(
        functools.partial(_prep_kernel, per_token_state=per_token_state),
        grid=(nseq, t_len // tm),
        in_specs=[blk(SHIFT_W), prev_spec, shift0_spec, cvec(SHIFT_W), cvec(W_A), cvec(W_A),
                  pl.BlockSpec((2 * W_A, 2 * LORA), lambda s, j: (0, 0)), cvec(W_A), cvec(W_A), cvec(W_A)],
        out_specs=[chan] * 6 + [pl.BlockSpec((H_A, tm), lambda s, j: (s, j))],
        out_shape=[jax.ShapeDtypeStruct((HEAD_SIZE, nseq * H_A, t_len), F32)] * 6
                  + [jax.ShapeDtypeStruct((nseq * H_A, t_len), F32)],
        name="rwkv_prep",
    )(usht, usht, shift0t, col(mu), col(w0), col(a0), lorat_bf, col(k_k), col(k_a), col(r_k))


def _step_kernel(r_ref, w_ref, k_ref, v_ref, kk_ref, b_ref, coef_ref, lng_ref, lnb_ref, s_ref,
                 y_ref, so_ref, o_ref):
    kk, w, b, km, r = kk_ref[...], w_ref[...], b_ref[...], k_ref[...], r_ref[...]
    for vi in range(HEAD_SIZE):
        s = s_ref[vi]
        sa = -jnp.sum(s * kk, axis=0, keepdims=True)
        s = s * w + sa * b + v_ref[pl.ds(vi, 1), :] * km
        so_ref[vi] = s
        o_ref[pl.ds(vi, 1), :] = jnp.sum(s * r, axis=0, keepdims=True)
    o = o_ref[...]
    mean = jnp.sum(o, axis=0, keepdims=True) * (1.0 / HEAD_SIZE)
    c = o - mean
    var = jnp.sum(c * c, axis=0, keepdims=True) * (1.0 / HEAD_SIZE)
    coef = coef_ref[pl.ds(pl.program_id(0), 1), :]
    y_ref[...] = c * lax.rsqrt(var + GN_EPS) * lng_ref[...] + lnb_ref[...] + coef * v_ref[...]


def _step(seqs, coef, lng, lnb, state):
    nb = coef.shape[1]
    seq_spec = pl.BlockSpec((None, HEAD_SIZE, nb), lambda h: (h, 0, 0))
    state_spec = pl.BlockSpec((None, HEAD_SIZE, HEAD_SIZE, nb), lambda h: (h, 0, 0, 0))
    return pl.pallas_call(
        _step_kernel,
        grid=(H_A,),
        in_specs=[seq_spec] * 6 + [pl.BlockSpec((H_A, nb), lambda h: (0, 0)), seq_spec, seq_spec, state_spec],
        out_specs=[seq_spec, state_spec],
        out_shape=[jax.ShapeDtypeStruct((H_A, HEAD_SIZE, nb), F32),
                   jax.ShapeDtypeStruct((H_A, HEAD_SIZE, HEAD_SIZE, nb), F32)],
        scratch_shapes=[pltpu.VMEM((HEAD_SIZE, nb), F32)],
        name="rwkv_step",
    )(*seqs, coef, lng, lnb, state)


def _prep_rows(u_ref, prev_ref, first, mu_ref, w0_ref, a0_ref, lora_ref, kk_ref, ka_ref, rk_ref, g_ref):
    u = u_ref[...]
    prev_row = jnp.where(first, 0.0, prev_ref[SUBLANES - 1:SUBLANES, :])
    row = lax.broadcasted_iota(jnp.int32, u.shape, 0)
    u_prev = jnp.where(row == 0, prev_row, pltpu.roll(u, 1, axis=0))
    xs = u + (u_prev - u) * mu_ref[...]
    r = xs[:, 0:W_A]
    k = xs[:, W_A:2 * W_A]
    v = xs[:, 2 * W_A:3 * W_A]
    tail = xs[:, 3 * W_A:SHIFT_W]
    lane = lax.broadcasted_iota(jnp.int32, tail.shape, 1)
    lora_in = jnp.where(lane < LORA, jnp.tanh(tail), tail).astype(BF16)
    up = _dot(lora_in, lora_ref[...])
    neg = -(w0_ref[...] + up[:, 0:W_A])
    softplus = jnp.maximum(neg, 0.0) + jnp.log(1.0 + jnp.exp(-jnp.abs(neg)))
    w_log = -softplus - 0.5
    a = jax.nn.sigmoid(a0_ref[...] + up[:, W_A:2 * W_A])
    kk = k * kk_ref[...]
    norm = jnp.sqrt(_head_sum(kk * kk, g_ref[...]))
    kk = kk / jnp.maximum(norm, 1e-12)
    k_mod = k * (1.0 + (a - 1.0) * ka_ref[...])
    lw = -jnp.exp(w_log)
    bonus = _head_sum(r * k_mod * rk_ref[...], g_ref[...]) * v
    return r, lw, k_mod, v, kk, kk * a, bonus


def _chunk_kernel(u_ref, prev_ref, mu_ref, w0_ref, a0_ref, lora_ref, kkp_ref, ka_ref, rk_ref, g_ref,
                  o_ref, bonus_ref, pf_ref, p_ref, *, nchunk):
    npair = W_A // PAIR
    first = pl.program_id(1) == 0

    @pl.when(first)
    def _():
        p_ref[...] = jnp.zeros(p_ref.shape, F32)

    seqs = _prep_rows(u_ref, prev_ref, first, mu_ref, w0_ref, a0_ref, lora_ref, kkp_ref, ka_ref, rk_ref, g_ref)
    bonus_ref[...] = seqs[6]

    ti = lax.broadcasted_iota(jnp.int32, (CHUNK, CHUNK), 0)
    tj = lax.broadcasted_iota(jnp.int32, (CHUNK, CHUNK), 1)
    strict = ti > tj
    incl = ti >= tj
    eye = jnp.where(ti == tj, 1.0, 0.0).astype(F32)
    tri = jnp.where(incl, 1.0, 0.0).astype(BF16)
    lane = lax.broadcasted_iota(jnp.int32, (CHUNK, PAIR), 1)
    head_a = lane < HEAD_SIZE
    pi = lax.broadcasted_iota(jnp.int32, (PAIR, PAIR), 0)
    pj = lax.broadcasted_iota(jnp.int32, (PAIR, PAIR), 1)
    same_head = (pi < HEAD_SIZE) == (pj < HEAD_SIZE)
    on_diag = pi == pj
    bf = lambda x: x.astype(BF16)
    cat = lambda a, b: jnp.concatenate([a, b], axis=1)

    items = [(pr, c) for pr in range(npair) for c in range(nchunk)]
    sels = (head_a, jnp.logical_not(head_a))
    pre = []
    for pr, c in items:
        r, lw, k, v, kk, b = (x[c * CHUNK:(c + 1) * CHUNK, pr * PAIR:(pr + 1) * PAIR] for x in seqs[:6])
        g = _dot(tri, bf(lw))
        g = g + _dot(tri, bf(lw - bf(lw).astype(F32)))
        g_end = g[CHUNK - 1:CHUNK, :]
        e_neg = jnp.exp(-g)
        e_end = jnp.exp(g_end - g)
        kkt = kk * jnp.exp(g - lw)
        pre.append(dict(kkt=kkt, rt=r * jnp.exp(g), kkt_b=bf(kkt), bt_b=bf(b * e_neg), kt_b=bf(k * e_neg),
                        v_b=bf(v), bh_b=bf(b * e_end), kh_b=bf(k * e_end), gamma_end=jnp.exp(g_end)))
    tris = []
    for it in pre:
        for sel in sels:
            lhs = bf(jnp.concatenate([jnp.where(sel, it["kkt"], 0.0), jnp.where(sel, it["rt"], 0.0)], axis=0))
            xb = _dot_nt(lhs, it["bt_b"])
            xk = _dot_nt(lhs, it["kt_b"])
            tris.append((jnp.where(strict, xb[0:CHUNK], 0.0), jnp.where(strict, xk[0:CHUNK], 0.0),
                         jnp.where(incl, xb[CHUNK:2 * CHUNK], 0.0), jnp.where(incl, xk[CHUNK:2 * CHUNK], 0.0)))
    t_inv = [eye - t[0] for t in tris]
    power = [t[0] for t in tris]
    for _ in range(NEUMANN_STEPS):
        power = [_dot(bf(x), bf(x)) for x in power]
        t_inv = [t + _dot(bf(t), bf(x)) for t, x in zip(t_inv, power)]
    owner = [it for it in pre for _ in sels]
    lkv = [_dot(bf(t[1]), it["v_b"]) for t, it in zip(tris, owner)]
    tx = [_dot(bf(t), cat(it["kkt_b"], bf(x))) for t, it, x in zip(t_inv, owner, lkv)]
    mx = [_dot(bf(t[2]), bf(x)) for t, x in zip(tris, tx)]
    mv = [_dot(bf(t[3]), it["v_b"]) for t, it in zip(tris, owner)]
    pick = lambda xa, xb: jnp.where(head_a, xa, xb)
    affine = {}
    for n, ((pr, c), it) in enumerate(zip(items, pre)):
        ia, ib = 2 * n, 2 * n + 1
        w_all = pick(tx[ia][:, 0:PAIR], tx[ib][:, 0:PAIR])
        u_loc = pick(tx[ia][:, PAIR:2 * PAIR], tx[ib][:, PAIR:2 * PAIR])
        q_eff = it["rt"] - pick(mx[ia][:, 0:PAIR], mx[ib][:, 0:PAIR])
        o_loc = pick(mv[ia], mv[ib]) - pick(mx[ia][:, PAIR:2 * PAIR], mx[ib][:, PAIR:2 * PAIR])
        bx = _dot_tn(it["bh_b"], cat(bf(w_all), bf(u_loc)))
        kv = _dot_tn(it["kh_b"], it["v_b"])
        a_eff = jnp.where(on_diag, it["gamma_end"], 0.0) - jnp.where(same_head, bx[:, 0:PAIR], 0.0)
        p_loc = jnp.where(same_head, kv - bx[:, PAIR:2 * PAIR], 0.0)
        affine[(pr, c)] = (bf(a_eff), p_loc, bf(q_eff), o_loc)

    state = [p_ref[pr] for pr in range(npair)]
    for c in range(nchunk):
        for pr in range(npair):
            a_eff, p_loc, q_eff, o_loc = affine[(pr, c)]
            p_b = bf(state[pr])
            o_ref[c * CHUNK:(c + 1) * CHUNK, pr * PAIR:(pr + 1) * PAIR] = _dot(q_eff, p_b) + o_loc
            state[pr] = _dot(a_eff, p_b) + p_loc
    for pr in range(npair):
        p_ref[pr] = state[pr]

    @pl.when(pl.program_id(1) == pl.num_programs(1) - 1)
    def _():
        for pr in range(npair):
            pf_ref[2 * pr] = state[pr][0:HEAD_SIZE, 0:HEAD_SIZE].T
            pf_ref[2 * pr + 1] = state[pr][HEAD_SIZE:PAIR, HEAD_SIZE:PAIR].T


def _chunk_scan(ush, nseq, lw, g512, nchunk):
    m = ush.shape[0]
    tt = nchunk * CHUNK
    steps = m // (nseq * tt)
    row = lambda a: a.reshape(1, -1)
    vec = lambda n: pl.BlockSpec((1, n), lambda s, i: (0, 0))
    blk = lambda w: pl.BlockSpec((tt, w), lambda s, i: (s * steps + i, 0))
    return pl.pallas_call(
        functools.partial(_chunk_kernel, nchunk=nchunk),
        grid=(nseq, steps),
        in_specs=[blk(SHIFT_W),
                  pl.BlockSpec((SUBLANES, SHIFT_W),
                               lambda s, i: (jnp.maximum((s * steps + i) * (tt // SUBLANES) - 1, 0), 0)),
                  vec(SHIFT_W), vec(W_A), vec(W_A),
                  pl.BlockSpec((2 * LORA, 2 * W_A), lambda s, i: (0, 0)),
                  vec(W_A), vec(W_A), vec(W_A),
                  pl.BlockSpec((W_A, W_A), lambda s, i: (0, 0))],
        out_specs=[blk(W_A), blk(W_A),
                   pl.BlockSpec((None, H_A, HEAD_SIZE, HEAD_SIZE), lambda s, i: (s, 0, 0, 0))],
        out_shape=[jax.ShapeDtypeStruct((m, W_A), F32), jax.ShapeDtypeStruct((m, W_A), F32),
                   jax.ShapeDtypeStruct((nseq, H_A, HEAD_SIZE, HEAD_SIZE), F32)],
        scratch_shapes=[pltpu.VMEM((W_A // PAIR, PAIR, PAIR), F32)],
        compiler_params=pltpu.CompilerParams(dimension_semantics=("arbitrary", "arbitrary")),
        name="rwkv_chunk_scan",
    )(ush, ush, row(lw["mu"]), row(lw["w0"]), row(lw["a0"]), lw["lora"], row(lw["k_k"]), row(lw["k_a"]),
      row(lw["r_k"]), g512)


def _attn_kernel(cur_ref, prev_ref, zb_ref, gtab_ref, sink_ref, yb_ref, bias_ref, *, blocks):
    s = pl.program_id(1)
    rows = Q_PER_KV * WINDOW

    @pl.when((pl.program_id(0) == 0) & (s == 0))
    def _():
        qi = lax.broadcasted_iota(jnp.int32, (WINDOW, 2 * WINDOW), 0)
        kj = lax.broadcasted_iota(jnp.int32, (WINDOW, 2 * WINDOW), 1)
        dist = qi + WINDOW - kj
        band = (dist >= 0) & (dist <= WINDOW)
        for g in range(KV_HEADS):
            for j in range(Q_PER_KV):
                h = g * Q_PER_KV + j
                row = jnp.broadcast_to(gtab_ref[h:h + 1, :], (WINDOW, BIAS_SPAN))
                toe = pltpu.roll(row, 0, 1, stride=1, stride_axis=0)[:, WINDOW:BIAS_SPAN]
                bias_ref[g, j * WINDOW:(j + 1) * WINDOW, :] = jnp.where(band, toe, NEG_INF)

    cur = cur_ref[...]
    prev = prev_ref[...]
    k_all = jnp.concatenate([prev[:, W_B:W_B + KV_W], cur[:, W_B:W_B + KV_W]], axis=0)
    v_all = jnp.concatenate([prev[:, W_B + KV_W:QKV_W], cur[:, W_B + KV_W:QKV_W]], axis=0)
    lane = lax.broadcasted_iota(jnp.int32, k_all.shape, 1)
    zero = jnp.zeros_like(k_all)
    k_g = [jnp.where(lane < HEAD_DIM, k_all, zero), jnp.where(lane >= HEAD_DIM, k_all, zero)]
    v_g = [jnp.where(lane < HEAD_DIM, v_all, zero), jnp.where(lane >= HEAD_DIM, v_all, zero)]
    ones = jnp.ones((2 * WINDOW, LANES), BF16)
    kj = lax.broadcasted_iota(jnp.int32, (rows, 2 * WINDOW), 1)
    insts = [(i, g) for i in range(blocks) for g in range(KV_HEADS)]
    keys_of = lambda x, i: x[i * WINDOW:(i + 2) * WINDOW]
    q4 = [jnp.concatenate([cur[i * WINDOW:(i + 1) * WINDOW, j * LANES:(j + 1) * LANES] for j in range(Q_PER_KV)],
                          axis=0) for i in range(blocks)]
    logits = []
    for i, g in insts:
        lg = _dot_nt(q4[i], keys_of(k_g[g], i)) + bias_ref[g]
        if i == 0:
            lg = jnp.where((kj >= WINDOW) | (s > 0), lg, NEG_INF)
        logits.append(lg)
    tops = [jnp.maximum(jnp.max(lg, axis=-1, keepdims=True), sink_ref[g]) for (i, g), lg in zip(insts, logits)]
    probs = [jnp.exp(lg - m).astype(BF16) for lg, m in zip(logits, tops)]
    dens = [_dot(p, ones) + jnp.exp(sink_ref[g] - m) for (i, g), p, m in zip(insts, probs, tops)]
    outs = [_dot(p, keys_of(v_g[g], i)) / den for (i, g), p, den in zip(insts, probs, dens)]
    for i in range(blocks):
        acc = outs[KV_HEADS * i] + outs[KV_HEADS * i + 1]
        r0 = i * WINDOW
        for j in range(Q_PER_KV):
            zb = zb_ref[r0:r0 + WINDOW, j * LANES:(j + 1) * LANES]
            yb_ref[r0:r0 + WINDOW, j * LANES:(j + 1) * LANES] = (
                acc[j * WINDOW:(j + 1) * WINDOW] * _silu(zb)).astype(BF16)


def _attn(qkv, zb, nb_seq, gtab, sink_col):
    m = qkv.shape[0]
    nblk = m // (nb_seq * WINDOW)
    blocks = _pick((ATTN_BLOCKS, 2, 1), nblk)
    steps = nblk // blocks
    rows = Q_PER_KV * WINDOW
    return pl.pallas_call(
        functools.partial(_attn_kernel, blocks=blocks),
        grid=(nb_seq, steps),
        in_specs=[pl.BlockSpec((blocks * WINDOW, QKV_W), lambda b, s: (b * steps + s, 0)),
                  pl.BlockSpec((WINDOW, QKV_W), lambda b, s: (jnp.maximum((b * steps + s) * blocks - 1, 0), 0)),
                  pl.BlockSpec((blocks * WINDOW, W_B), lambda b, s: (b * steps + s, 0)),
                  pl.BlockSpec((H_B, BIAS_SPAN), lambda b, s: (0, 0)),
                  pl.BlockSpec((KV_HEADS, rows, 1), lambda b, s: (0, 0, 0))],
        out_specs=pl.BlockSpec((blocks * WINDOW, W_B), lambda b, s: (b * steps + s, 0)),
        out_shape=jax.ShapeDtypeStruct((m, W_B), BF16),
        scratch_shapes=[pltpu.VMEM((KV_HEADS, rows, 2 * WINDOW), F32)],
        compiler_params=pltpu.CompilerParams(dimension_semantics=("arbitrary", "arbitrary")),
        name="swa_prompt",
    )(qkv, qkv, zb, gtab, sink_col)


def _attn_step_kernel(q_ref, kn_ref, vn_ref, zb_ref, ck_ref, cv_ref, bias_ref, bias0_ref, sink_ref, g128_ref,
                      yb_ref, ko_ref, vo_ref):
    q = q_ref[...].astype(F32)
    kn = kn_ref[...]
    vn = vn_ref[...]
    ck = ck_ref[...]
    cv = cv_ref[...]
    bt = q.shape[0]
    for r in range(Q_PER_KV):
        qsel = q[:, r * LANES:(r + 1) * LANES]
        prod = (ck * qsel[:, None, :]).reshape(bt * WINDOW, KV_W)
        lg = _split_dot(prod, g128_ref[...]).reshape(bt, WINDOW, KV_W) + bias_ref[r]
        lgn = _split_dot(kn * qsel, g128_ref[...]) + bias0_ref[r]
        s = sink_ref[r]
        m = jnp.maximum(jnp.maximum(jnp.max(lg, axis=1), lgn), s)
        p = jnp.exp(lg - m[:, None, :])
        pn = jnp.exp(lgn - m)
        den = jnp.sum(p, axis=1) + pn + jnp.exp(s - m)
        o = (jnp.sum(p * cv, axis=1) + pn * vn) / den
        zb = zb_ref[:, r * LANES:(r + 1) * LANES]
        yb_ref[:, r * LANES:(r + 1) * LANES] = (o * _silu(zb)).astype(BF16)
    j = lax.broadcasted_iota(jnp.int32, ck.shape, 1)
    ko_ref[...] = jnp.where(j == WINDOW - 1, kn[:, None, :], pltpu.roll(ck, WINDOW - 1, axis=1))
    vo_ref[...] = jnp.where(j == WINDOW - 1, vn[:, None, :], pltpu.roll(cv, WINDOW - 1, axis=1))


def _attn_step(qkv, kv32, zb, cache_k, cache_v, bias_rows, bias0, sink_rows, g128, bt):
    nb = qkv.shape[0]
    cache_spec = pl.BlockSpec((bt, WINDOW, KV_W), lambda i: (i, 0, 0))
    return pl.pallas_call(
        _attn_step_kernel,
        grid=(nb // bt,),
        in_specs=[pl.BlockSpec((bt, W_B), lambda i: (i, 0)),
                  pl.BlockSpec((bt, KV_W), lambda i: (i, 0)),
                  pl.BlockSpec((bt, KV_W), lambda i: (i, 1)),
                  pl.BlockSpec((bt, W_B), lambda i: (i, 0)),
                  cache_spec, cache_spec,
                  pl.BlockSpec((Q_PER_KV, WINDOW, KV_W), lambda i: (0, 0, 0)),
                  pl.BlockSpec((Q_PER_KV, 1, KV_W), lambda i: (0, 0, 0)),
                  pl.BlockSpec((Q_PER_KV, 1, KV_W), lambda i: (0, 0, 0)),
                  pl.BlockSpec((KV_W, KV_W), lambda i: (0, 0))],
        out_specs=[pl.BlockSpec((bt, W_B), lambda i: (i, 0)), cache_spec, cache_spec],
        out_shape=[jax.ShapeDtypeStruct((nb, W_B), BF16),
                   jax.ShapeDtypeStruct(cache_k.shape, F32),
                   jax.ShapeDtypeStruct(cache_v.shape, F32)],
        name="swa_step",
    )(qkv, kv32, kv32, zb, cache_k, cache_v, bias_rows, bias0, sink_rows, g128)


def _merge_kernel(yt_ref, zat_ref, yb_ref, ga_ref, gb_ref, x_ref, gate_ref, woa_ref, wob_ref, wout_ref, out_ref):
    yt = yt_ref[...]
    yat = (yt.reshape(W_A, yt.shape[2]) * _silu(zat_ref[...])).astype(BF16)
    pa = _dot_tn(yat, woa_ref[...])
    pb = _dot(yb_ref[...], wob_ref[...])
    merged = _sigmoid(ga_ref[...]) * pa + _sigmoid(gb_ref[...]) * pb
    out_ref[...] = x_ref[...] + gate_ref[...] * _dot(merged.astype(BF16), wout_ref[...])


def _merge(yt, zat, yb, gates, x, gate, woa_bf, wob_bf, wout_bf, tm, rows_per_seq, per_row):
    m = x.shape[0]
    full = lambda c: pl.BlockSpec((tm, D_MODEL), lambda i: (i, c))
    per_seq = rows_per_seq // tm
    y_spec = pl.BlockSpec((HEAD_SIZE, H_A, tm), lambda i: (0, i // per_seq, i % per_seq))
    return pl.pallas_call(
        _merge_kernel,
        grid=(m // tm,),
        in_specs=[y_spec, _t_spec(W_A, tm, rows_per_seq),
                  pl.BlockSpec((tm, W_B), lambda i: (i, 0)), full(0), full(1), full(0),
                  _mod_spec(per_row, tm, rows_per_seq),
                  pl.BlockSpec((W_A, D_MODEL), lambda i: (0, 0)),
                  pl.BlockSpec((W_B, D_MODEL), lambda i: (0, 0)),
                  pl.BlockSpec((D_MODEL, D_MODEL), lambda i: (0, 0))],
        out_specs=full(0),
        out_shape=jax.ShapeDtypeStruct((m, D_MODEL), F32),
        name="merge_out",
    )(yt, zat, yb, gates, gates, x, gate, woa_bf, wob_bf, wout_bf)


def _merge_rows_kernel(o_ref, bonus_ref, za_ref, yb_ref, ga_ref, gb_ref, x_ref, gate_ref, lng_ref, lnb_ref,
                       g512_ref, woa_ref, wob_ref, wout_ref, out_ref):
    o = o_ref[...]
    mean = _head_sum(o, g512_ref[...]) * (1.0 / HEAD_SIZE)
    c = o - mean
    var = _head_sum(c * c, g512_ref[...]) * (1.0 / HEAD_SIZE)
    on = c * lax.rsqrt(var + GN_EPS) * lng_ref[...] + lnb_ref[...]
    ya = (on + bonus_ref[...]) * _silu(za_ref[...])
    pa = _dot(ya.astype(BF16), woa_ref[...])
    pb = _dot(yb_ref[...], wob_ref[...])
    merged = _sigmoid(ga_ref[...]) * pa + _sigmoid(gb_ref[...]) * pb
    out_ref[...] = x_ref[...] + gate_ref[...] * _dot(merged.astype(BF16), wout_ref[...])


def _merge_rows(o, bonus, za, yb, gates, x, gate, lnx_g, lnx_b, g512, woa_bf, wob_bf, wout_bf, tm, rows_per_seq):
    m = x.shape[0]
    half = pl.BlockSpec((tm, W_A), lambda i: (i, 0))
    full = lambda c: pl.BlockSpec((tm, D_MODEL), lambda i: (i, c))
    vec = pl.BlockSpec((1, W_A), lambda i: (0, 0))
    return pl.pallas_call(
        _merge_rows_kernel,
        grid=(m // tm,),
        in_specs=[half, half, half, half, full(0), full(1), full(0),
                  _mod_spec(False, tm, rows_per_seq), vec, vec,
                  pl.BlockSpec((W_A, W_A), lambda i: (0, 0)),
                  pl.BlockSpec((W_A, D_MODEL), lambda i: (0, 0)),
                  pl.BlockSpec((W_B, D_MODEL), lambda i: (0, 0)),
                  pl.BlockSpec((D_MODEL, D_MODEL), lambda i: (0, 0))],
        out_specs=full(0),
        out_shape=jax.ShapeDtypeStruct((m, D_MODEL), F32),
        name="merge_out_rows",
    )(o, bonus, za, yb, gates, gates, x, gate, lnx_g.reshape(1, W_A), lnx_b.reshape(1, W_A), g512,
      woa_bf, wob_bf, wout_bf)


def _t5_bucket(dist):
    max_exact = N_BUCKETS // 2
    d = jnp.maximum(dist, 0)
    log_ratio = jnp.log(jnp.maximum(d, 1).astype(F32) / max_exact) / math.log(MAX_DISTANCE / max_exact)
    large = jnp.minimum(max_exact + (log_ratio * (N_BUCKETS - max_exact)).astype(jnp.int32), N_BUCKETS - 1)
    return jnp.where(d < max_exact, d, large)


def _block_ones(n, blk):
    i = np.arange(n) // blk
    return jnp.asarray((i[:, None] == i[None, :]).astype(np.float32), dtype=BF16)


def _rwkv_sample(usht, shift0, state0, lw):
    nb = usht.shape[2]
    outs = _prep(usht, _shift_major(shift0, 1).T[None], lw["mu"], lw["w0"], lw["a0"], lw["lorat"], lw["k_k"],
                 lw["k_a"], lw["r_k"], nb, True)
    tile = lambda a: jnp.broadcast_to(a.reshape(H_A, HEAD_SIZE, 1), (H_A, HEAD_SIZE, nb))
    y, state = _step([a.transpose(1, 0, 2) for a in outs[:6]], outs[6], tile(lw["lnx_g"]), tile(lw["lnx_b"]),
                     state0.transpose(1, 2, 3, 0))
    return y.transpose(1, 0, 2), state.transpose(3, 0, 1, 2)


def kernel(x_prompt, x_sample, c_prompt, c_sample, state_wkv, state_shift, cache_k, cache_v, norm_g, w_ada, b_ada, w_in, mu_shift, w0, w_decay_up, a0, w_a_up, k_k, k_a, r_k, lnx_g, lnx_b, w_o_a, q_norm_g, k_norm_g, rel_bias, sinks, w_o_b, w_out):
    nbp, t_len, _ = x_prompt.shape
    nbs = x_sample.shape[0]
    depth = norm_g.shape[0]
    mp = nbp * t_len
    assert (nbp * H_A) % LANES == 0 and nbs % LANES == 0 and t_len % LANES == 0
    g512 = _block_ones(W_B, HEAD_DIM)
    g128 = _block_ones(KV_W, HEAD_DIM)

    gtab = rel_bias[_t5_bucket(2 * WINDOW - jnp.arange(BIAS_SPAN))].astype(F32).T
    bias_s = rel_bias[_t5_bucket(WINDOW - jnp.arange(WINDOW + 1))].astype(F32)
    pair = lambda a: jnp.concatenate([jnp.repeat(a[..., 0:Q_PER_KV, None], HEAD_DIM, axis=-1),
                                      jnp.repeat(a[..., Q_PER_KV:H_B, None], HEAD_DIM, axis=-1)], axis=-1)
    bias_rows = pair(bias_s[:WINDOW]).transpose(1, 0, 2)
    bias0 = pair(bias_s[WINDOW:]).transpose(1, 0, 2)

    c_all = jnp.concatenate([c_prompt, c_sample], axis=0)
    xp = x_prompt.reshape(mp, D_MODEL)
    xs = x_sample.reshape(nbs, D_MODEL)
    tm_p = _pick((512, 256, 128), t_len)
    tm_in = _pick((512, 256, 128), t_len)
    outs = {k: [] for k in ("wkv_p", "shift_p", "kw_p", "vw_p", "wkv_s", "shift_s", "kw_s", "vw_s")}
    for l in range(depth):
        wl = w_in[l].astype(BF16)
        base = SHIFT_W
        z_a, q, kb, vb, z_b, gts = (wl[:, base:base + 512], wl[:, base + 512:base + 1024],
                                    wl[:, base + 1024:base + 1152], wl[:, base + 1152:base + 1280],
                                    wl[:, base + 1280:base + 1792], wl[:, base + 1792:])
        w_bf = jnp.concatenate([_head_pairs(q, 1), kb, vb, _head_pairs(z_b, 1), gts], axis=1)
        wa_bf = jnp.concatenate([wl[:, :SHIFT_W], z_a], axis=1)
        wt_bf = jnp.concatenate([_shift_major(wl[:, :SHIFT_W], 1), _chan_major(z_a, 1)], axis=1).T
        zeros = jnp.zeros((LORA, W_A), F32)
        lora = jnp.concatenate([jnp.concatenate([w_decay_up[l], zeros], axis=1),
                                jnp.concatenate([zeros, w_a_up[l]], axis=1)], axis=0).astype(BF16)
        lorat = jnp.concatenate([jnp.concatenate([_chan_major(w_decay_up[l], 1), zeros], axis=1),
                                 jnp.concatenate([zeros, _chan_major(w_a_up[l], 1)], axis=1)],
                                axis=0).T.astype(BF16)
        cm = lambda a: _chan_major(a.reshape(-1), 0)
        lw_p = dict(mu=mu_shift[l], w0=w0[l], a0=a0[l], lora=lora, k_k=k_k[l], k_a=k_a[l], r_k=r_k[l].reshape(-1))
        lw = dict(mu=_shift_major(mu_shift[l], 0), w0=cm(w0[l]), a0=cm(a0[l]), lorat=lorat,
                  k_k=cm(k_k[l]), k_a=cm(k_a[l]), r_k=cm(r_k[l]), lnx_g=lnx_g[l], lnx_b=lnx_b[l])
        woa_rows_bf = w_o_a[l].astype(BF16)
        woa_bf, wout_bf = _chan_major(woa_rows_bf, 0), w_out[l].astype(BF16)
        wob_bf = _head_pairs(w_o_b[l].astype(BF16), 0)
        q_gain = jnp.tile(q_norm_g[l], H_B).reshape(1, W_B)
        k_gain = jnp.tile(k_norm_g[l], KV_HEADS).reshape(1, KV_W)
        sink_col = jnp.repeat(sinks[l].reshape(KV_HEADS, Q_PER_KV), WINDOW, axis=1).reshape(
            KV_HEADS, Q_PER_KV * WINDOW, 1)
        sink_rows = pair(sinks[l].reshape(1, H_B)).transpose(1, 0, 2)

        mod = _ada(c_all, w_ada[l].astype(BF16), b_ada[l])
        shift, scale, gate = mod[:, :D_MODEL], mod[:, D_MODEL:2 * D_MODEL], mod[:, 2 * D_MODEL:]

        sp, scp, gp = (a[:nbp].reshape(nbp, 1, D_MODEL) for a in (shift, scale, gate))
        ush, za, qkv, kv32, zb, gts_p = _in_proj(xp, norm_g[l], scp, sp, w_bf, wa_bf, q_gain, k_gain, g512, g128,
                                                 tm_in, t_len, False, False)
        o, bonus, wkv = _chunk_scan(ush, nbp, lw_p, g512, _pick((CHUNKS_PER_STEP, 1), t_len // CHUNK))
        yb = _attn(qkv, zb, nbp, gtab, sink_col)
        xp = _merge_rows(o, bonus, za, yb, gts_p, xp, gp, lnx_g[l], lnx_b[l], g512, woa_rows_bf, wob_bf, wout_bf,
                         tm_p, t_len)
        win = kv32.reshape(nbp, t_len, 2, KV_HEADS, HEAD_DIM)[:, t_len - WINDOW:]
        outs["wkv_p"].append(wkv)
        outs["shift_p"].append(ush.reshape(nbp, t_len, SHIFT_W)[:, t_len - 1])
        outs["kw_p"].append(win[:, :, 0])
        outs["vw_p"].append(win[:, :, 1])

        ss, scs, gs = shift[nbp:], scale[nbp:], gate[nbp:]
        usht, zat, qkv, kv32, zb, gts_s = _in_proj(xs, norm_g[l], scs, ss, w_bf, wt_bf, q_gain, k_gain, g512, g128,
                                                   nbs, nbs, True, True)
        yt, wkv = _rwkv_sample(usht, state_shift[l], state_wkv[l], lw)
        yb, kwin, vwin = _attn_step(qkv, kv32, zb, cache_k[l].reshape(nbs, WINDOW, KV_W),
                                    cache_v[l].reshape(nbs, WINDOW, KV_W),
                                    bias_rows, bias0, sink_rows, g128, _pick((16, 8), nbs))
        xs = _merge(yt, zat, yb, gts_s, xs, gs, woa_bf, wob_bf, wout_bf, nbs, nbs, True)
        outs["wkv_s"].append(wkv)
        outs["shift_s"].append(_shift_major(usht[0].T, 1, inverse=True))
        outs["kw_s"].append(kwin.reshape(nbs, WINDOW, KV_HEADS, HEAD_DIM))
        outs["vw_s"].append(vwin.reshape(nbs, WINDOW, KV_HEADS, HEAD_DIM))

    st = lambda k: jnp.stack(outs[k])
    return (xp.reshape(nbp, t_len, D_MODEL), xs.reshape(nbs, 1, D_MODEL),
            st("wkv_p"), st("shift_p"), st("kw_p"), st("vw_p"),
            st("wkv_s"), st("shift_s"), st("kw_s"), st("vw_s"))
```

```python
import functools
import math

import numpy as np
import jax
import jax.numpy as jnp
from jax import lax
from jax.experimental import pallas as pl
from jax.experimental.pallas import tpu as pltpu

F32 = jnp.float32
BF16 = jnp.bfloat16

D_MODEL = 1024
W_A = 512
HEAD_SIZE = 64
H_A = W_A // HEAD_SIZE
LORA = 64
SHIFT_W = 3 * W_A + 2 * LORA
W_B = 512
HEAD_DIM = 64
H_B = W_B // HEAD_DIM
KV_HEADS = 2
Q_PER_KV = H_B // KV_HEADS
KV_W = KV_HEADS * HEAD_DIM
WINDOW = 128
N_BUCKETS = 32
MAX_DISTANCE = 128
NORM_EPS = 1e-6
GN_EPS = 64e-5
NEG_INF = -1e30
QKV_W = W_B + 2 * KV_W
GATE_W = 2 * D_MODEL
ROW_COLS = QKV_W + W_B + GATE_W
T_COLS = SHIFT_W + W_A
LANES = 128
SUBLANES = 8
def _head_pairs(x, axis):
    s = x.shape
    x = x.reshape(s[:axis] + (KV_HEADS, Q_PER_KV, HEAD_DIM) + s[axis + 1:])
    return jnp.swapaxes(x, axis, axis + 1).reshape(s)


def _chan_major(x, axis, inverse=False):
    s = x.shape
    split = (HEAD_SIZE, H_A) if inverse else (H_A, HEAD_SIZE)
    x = x.reshape(s[:axis] + split + s[axis + 1:])
    return jnp.swapaxes(x, axis, axis + 1).reshape(s)


def _shift_major(x, axis, inverse=False):
    parts = [lax.slice_in_dim(x, i * W_A, (i + 1) * W_A, axis=axis) for i in range(3)]
    tail = lax.slice_in_dim(x, 3 * W_A, SHIFT_W, axis=axis)
    return jnp.concatenate([_chan_major(p, axis, inverse) for p in parts] + [tail], axis=axis)
ATTN_BLOCKS = 4
BIAS_SPAN = 3 * WINDOW
CHUNK = 64
CHUNKS_PER_STEP = 4
PAIR = 2 * HEAD_SIZE
NEUMANN_STEPS = CHUNK.bit_length() - 2


def _dot(a, b):
    return jnp.dot(a, b, preferred_element_type=F32)


def _dot_nt(a, b):
    return lax.dot_general(a, b, (((1,), (1,)), ((), ())), preferred_element_type=F32)


def _dot_tn(a, b):
    return lax.dot_general(a, b, (((0,), (0,)), ((), ())), preferred_element_type=F32)


def _head_sum(x, g):
    return _dot(x.astype(BF16), g)


def _split_dot(x, g):
    hi = x.astype(BF16)
    lo = (x - hi.astype(F32)).astype(BF16)
    return _dot(hi, g) + _dot(lo, g)


def _silu(x):
    x = x.astype(F32)
    return x * jax.nn.sigmoid(x)


def _sigmoid(x):
    return jax.nn.sigmoid(x.astype(F32))


def _pick(cands, n):
    for c in cands:
        if n % c == 0:
            return c
    return n


def _ada_kernel(c_ref, w_ref, b_ref, o_ref):
    c = c_ref[...]
    o_ref[...] = _dot(_silu(c).astype(BF16), w_ref[...]) + b_ref[...]


def _ada(c, w_bf, b):
    nb, d = c.shape
    n = w_bf.shape[1]
    tn = 1024
    return pl.pallas_call(
        _ada_kernel,
        grid=(n // tn,),
        in_specs=[pl.BlockSpec((nb, d), lambda j: (0, 0)),
                  pl.BlockSpec((d, tn), lambda j: (0, j)),
                  pl.BlockSpec((1, tn), lambda j: (0, j))],
        out_specs=pl.BlockSpec((nb, tn), lambda j: (0, j)),
        out_shape=jax.ShapeDtypeStruct((nb, n), F32),
        name="ada",
    )(c, w_bf, b.reshape(1, n))


def _in_proj_kernel(x_ref, g_ref, scale_ref, shift_ref, w_ref, wa_ref, qg_ref, kg_ref, g512_ref, g128_ref,
                    ush_ref, za_ref, qkv_ref, kv32_ref, zb_ref, gate_ref, *, transposed):
    x = x_ref[...]
    ms = jnp.mean(x * x, axis=-1, keepdims=True)
    h = (x * lax.rsqrt(ms + NORM_EPS)) * g_ref[...]
    h = h * (1.0 + scale_ref[...]) + shift_ref[...]
    hb = h.astype(BF16)
    if transposed:
        ush_ref[...] = _dot_nt(wa_ref[0:SHIFT_W, :], hb)
        za_ref[...] = _dot_nt(wa_ref[SHIFT_W:T_COLS, :], hb).astype(BF16)
    else:
        ush_ref[...] = _dot(hb, wa_ref[:, 0:SHIFT_W])
        za_ref[...] = _dot(hb, wa_ref[:, SHIFT_W:T_COLS]).astype(BF16)
    qkv = _dot(hb, w_ref[:, 0:QKV_W])
    q, k, v = qkv[:, 0:W_B], qkv[:, W_B:W_B + KV_W], qkv[:, W_B + KV_W:QKV_W]
    qn = q * lax.rsqrt(_head_sum(q * q, g512_ref[...]) * (1.0 / HEAD_DIM) + NORM_EPS) * qg_ref[...]
    kn = k * lax.rsqrt(_head_sum(k * k, g128_ref[...]) * (1.0 / HEAD_DIM) + NORM_EPS) * kg_ref[...]
    qkv_ref[:, 0:W_B] = (qn * (HEAD_DIM ** -0.5)).astype(BF16)
    qkv_ref[:, W_B:W_B + KV_W] = kn.astype(BF16)
    qkv_ref[:, W_B + KV_W:QKV_W] = v.astype(BF16)
    kv32_ref[:, 0:KV_W] = kn
    kv32_ref[:, KV_W:2 * KV_W] = v
    zb_ref[...] = _dot(hb, w_ref[:, QKV_W:QKV_W + W_B]).astype(BF16)
    gate_ref[...] = _dot(hb, w_ref[:, QKV_W + W_B:ROW_COLS]).astype(BF16)


def _mod_spec(per_row, tm, rows_per_seq):
    if per_row:
        return pl.BlockSpec((tm, D_MODEL), lambda i: (i, 0))
    return pl.BlockSpec((None, 1, D_MODEL), lambda i: ((i * tm) // rows_per_seq, 0, 0))


def _t_spec(cols, tm, rows_per_seq):
    per_seq = rows_per_seq // tm
    return pl.BlockSpec((None, cols, tm), lambda i: (i // per_seq, 0, i % per_seq))


def _in_proj(x, norm_g, scale, shift, w_bf, wa_bf, q_gain, k_gain, g512, g128, tm, rows_per_seq, per_row,
             transposed):
    m = x.shape[0]
    nseq = m // rows_per_seq
    const = lambda r, c: pl.BlockSpec((r, c), lambda i: (0, 0))
    rows = lambda w: pl.BlockSpec((tm, w), lambda i: (i, 0))
    widths = (QKV_W, 2 * KV_W, W_B, GATE_W)
    dtypes = (BF16, F32, BF16, BF16)
    if transposed:
        a_specs = [_t_spec(SHIFT_W, tm, rows_per_seq), _t_spec(W_A, tm, rows_per_seq)]
        a_shapes = [jax.ShapeDtypeStruct((nseq, SHIFT_W, rows_per_seq), F32),
                    jax.ShapeDtypeStruct((nseq, W_A, rows_per_seq), BF16)]
    else:
        a_specs = [rows(SHIFT_W), rows(W_A)]
        a_shapes = [jax.ShapeDtypeStruct((m, SHIFT_W), F32), jax.ShapeDtypeStruct((m, W_A), BF16)]
    return pl.pallas_call(
        functools.partial(_in_proj_kernel, transposed=transposed),
        grid=(m // tm,),
        in_specs=[rows(D_MODEL),
                  const(1, D_MODEL),
                  _mod_spec(per_row, tm, rows_per_seq),
                  _mod_spec(per_row, tm, rows_per_seq),
                  const(D_MODEL, ROW_COLS), const(*wa_bf.shape),
                  const(1, W_B), const(1, KV_W), const(W_B, W_B), const(KV_W, KV_W)],
        out_specs=a_specs + [rows(w) for w in widths],
        out_shape=a_shapes + [jax.ShapeDtypeStruct((m, w), dt) for w, dt in zip(widths, dtypes)],
        name="in_proj",
    )(x, norm_g.reshape(1, D_MODEL), scale, shift, w_bf, wa_bf, q_gain, k_gain, g512, g128)


def _prep_kernel(u_ref, prev_ref, shift0_ref, mu_ref, w0_ref, a0_ref, lora_ref, kk_ref, ka_ref, rk_ref,
                 r_o, w_o, k_o, v_o, kk_o, b_o, coef_o, *, per_token_state):
    tm = u_ref.shape[1]
    heads = lambda x: x.reshape(HEAD_SIZE, H_A, LANES)
    lane = lax.broadcasted_iota(jnp.int32, (SHIFT_W, LANES), 1)
    row = lax.broadcasted_iota(jnp.int32, (2 * LORA, LANES), 0)
    if not per_token_state:
        before = jnp.where(pl.program_id(1) == 0, shift0_ref[...], prev_ref[...])
        rolled_before = pltpu.roll(before, 1, axis=1)
    for ci in range(tm // LANES):
        cols = slice(ci * LANES, (ci + 1) * LANES)
        u = u_ref[:, cols]
        if per_token_state:
            u_prev = shift0_ref[:, cols]
        else:
            rolled = pltpu.roll(u, 1, axis=1)
            u_prev = jnp.where(lane == 0, rolled_before, rolled)
            rolled_before = rolled
        xs = u + (u_prev - u) * mu_ref[...]
        r = xs[0:W_A]
        k = xs[W_A:2 * W_A]
        v = xs[2 * W_A:3 * W_A]
        tail = xs[3 * W_A:SHIFT_W]
        lora_in = jnp.where(row < LORA, jnp.tanh(tail), tail).astype(BF16)
        up = _dot(lora_ref[...], lora_in)
        neg = -(w0_ref[...] + up[0:W_A])
        softplus = jnp.maximum(neg, 0.0) + jnp.log(1.0 + jnp.exp(-jnp.abs(neg)))
        w_log = -softplus - 0.5
        decay = jnp.exp(-jnp.exp(w_log))
        a = jax.nn.sigmoid(a0_ref[...] + up[W_A:2 * W_A])
        kk = heads(k * kk_ref[...])
        norm = jnp.sqrt(jnp.sum(kk * kk, axis=0, keepdims=True))
        kk = kk / jnp.maximum(norm, 1e-12)
        k_mod = k * (1.0 + (a - 1.0) * ka_ref[...])
        r_o[:, :, cols] = heads(r)
        w_o[:, :, cols] = heads(decay)
        k_o[:, :, cols] = heads(k_mod)
        v_o[:, :, cols] = heads(v)
        kk_o[:, :, cols] = kk
        b_o[:, :, cols] = kk * heads(a)
        coef_o[:, cols] = jnp.sum(heads(r * k_mod * rk_ref[...]), axis=0)


def _prep(usht, shift0t, mu, w0, a0, lorat_bf, k_k, k_a, r_k, tm, per_token_state):
    nseq, _, t_len = usht.shape
    col = lambda a: jnp.broadcast_to(a.reshape(-1, 1), (a.size, LANES))
    cvec = lambda n: pl.BlockSpec((n, LANES), lambda s, j: (0, 0))
    blk = lambda cols: pl.BlockSpec((None, cols, tm), lambda s, j: (s, 0, j))
    if per_token_state:
        prev_spec = pl.BlockSpec((None, SHIFT_W, LANES), lambda s, j: (0, 0, 0))
        shift0_spec = blk(SHIFT_W)
    else:
        prev_spec = pl.BlockSpec((None, SHIFT_W, LANES),
                                 lambda s, j: (s, 0, jnp.maximum(j * (tm // LANES) - 1, 0)))
        shift0_spec = pl.BlockSpec((None, SHIFT_W, LANES), lambda s, j: (s, 0, 0))
    chan = pl.BlockSpec((HEAD_SIZE, H_A, tm), lambda s, j: (0, s, j))
    return pl.pallas_call(
        functools.partial(_prep_kernel, per_token_state=per_token_state),
        grid=(nseq, t_len // tm),
        in_specs=[blk(SHIFT_W), prev_spec, shift0_spec, cvec(SHIFT_W), cvec(W_A), cvec(W_A),
                  pl.BlockSpec((2 * W_A, 2 * LORA), lambda s, j: (0, 0)), cvec(W_A), cvec(W_A), cvec(W_A)],
        out_specs=[chan] * 6 + [pl.BlockSpec((H_A, tm), lambda s, j: (s, j))],
        out_shape=[jax.ShapeDtypeStruct((HEAD_SIZE, nseq * H_A, t_len), F32)] * 6
                  + [jax.ShapeDtypeStruct((nseq * H_A, t_len), F32)],
        name="rwkv_prep",
    )(usht, usht, shift0t, col(mu), col(w0), col(a0), lorat_bf, col(k_k), col(k_a), col(r_k))


def _step_kernel(r_ref, w_ref, k_ref, v_ref, kk_ref, b_ref, coef_ref, lng_ref, lnb_ref, s_ref,
                 y_ref, so_ref, o_ref):
    kk, w, b, km, r = kk_ref[...], w_ref[...], b_ref[...], k_ref[...], r_ref[...]
    for vi in range(HEAD_SIZE):
        s = s_ref[vi]
        sa = -jnp.sum(s * kk, axis=0, keepdims=True)
        s = s * w + sa * b + v_ref[pl.ds(vi, 1), :] * km
        so_ref[vi] = s
        o_ref[pl.ds(vi, 1), :] = jnp.sum(s * r, axis=0, keepdims=True)
    o = o_ref[...]
    mean = jnp.sum(o, axis=0, keepdims=True) * (1.0 / HEAD_SIZE)
    c = o - mean
    var = jnp.sum(c * c, axis=0, keepdims=True) * (1.0 / HEAD_SIZE)
    coef = coef_ref[pl.ds(pl.program_id(0), 1), :]
    y_ref[...] = c * lax.rsqrt(var + GN_EPS) * lng_ref[...] + lnb_ref[...] + coef * v_ref[...]


def _step(seqs, coef, lng, lnb, state):
    nb = coef.shape[1]
    seq_spec = pl.BlockSpec((None, HEAD_SIZE, nb), lambda h: (h, 0, 0))
    state_spec = pl.BlockSpec((None, HEAD_SIZE, HEAD_SIZE, nb), lambda h: (h, 0, 0, 0))
    return pl.pallas_call(
        _step_kernel,
        grid=(H_A,),
        in_specs=[seq_spec] * 6 + [pl.BlockSpec((H_A, nb), lambda h: (0, 0)), seq_spec, seq_spec, state_spec],
        out_specs=[seq_spec, state_spec],
        out_shape=[jax.ShapeDtypeStruct((H_A, HEAD_SIZE, nb), F32),
                   jax.ShapeDtypeStruct((H_A, HEAD_SIZE, HEAD_SIZE, nb), F32)],
        scratch_shapes=[pltpu.VMEM((HEAD_SIZE, nb), F32)],
        name="rwkv_step",
    )(*seqs, coef, lng, lnb, state)


def _prep_rows(u_ref, prev_ref, first, mu_ref, w0_ref, a0_ref, lora_ref, kk_ref, ka_ref, rk_ref, g_ref):
    u = u_ref[...]
    prev_row = jnp.where(first, 0.0, prev_ref[SUBLANES - 1:SUBLANES, :])
    row = lax.broadcasted_iota(jnp.int32, u.shape, 0)
    u_prev = jnp.where(row == 0, prev_row, pltpu.roll(u, 1, axis=0))
    xs = u + (u_prev - u) * mu_ref[...]
    r = xs[:, 0:W_A]
    k = xs[:, W_A:2 * W_A]
    v = xs[:, 2 * W_A:3 * W_A]
    tail = xs[:, 3 * W_A:SHIFT_W]
    lane = lax.broadcasted_iota(jnp.int32, tail.shape, 1)
    lora_in = jnp.where(lane < LORA, jnp.tanh(tail), tail).astype(BF16)
    up = _dot(lora_in, lora_ref[...])
    neg = -(w0_ref[...] + up[:, 0:W_A])
    softplus = jnp.maximum(neg, 0.0) + jnp.log(1.0 + jnp.exp(-jnp.abs(neg)))
    w_log = -softplus - 0.5
    a = jax.nn.sigmoid(a0_ref[...] + up[:, W_A:2 * W_A])
    kk = k * kk_ref[...]
    norm = jnp.sqrt(_head_sum(kk * kk, g_ref[...]))
    kk = kk / jnp.maximum(norm, 1e-12)
    k_mod = k * (1.0 + (a - 1.0) * ka_ref[...])
    lw = -jnp.exp(w_log)
    bonus = _head_sum(r * k_mod * rk_ref[...], g_ref[...]) * v
    return r, lw, k_mod, v, kk, kk * a, bonus


def _chunk_kernel(u_ref, prev_ref, mu_ref, w0_ref, a0_ref, lora_ref, kkp_ref, ka_ref, rk_ref, g_ref,
                  o_ref, bonus_ref, pf_ref, p_ref, *, nchunk):
    npair = W_A // PAIR
    first = pl.program_id(1) == 0

    @pl.when(first)
    def _():
        p_ref[...] = jnp.zeros(p_ref.shape, F32)

    seqs = _prep_rows(u_ref, prev_ref, first, mu_ref, w0_ref, a0_ref, lora_ref, kkp_ref, ka_ref, rk_ref, g_ref)
    bonus_ref[...] = seqs[6].astype(BF16)

    ti = lax.broadcasted_iota(jnp.int32, (CHUNK, CHUNK), 0)
    tj = lax.broadcasted_iota(jnp.int32, (CHUNK, CHUNK), 1)
    strict = ti > tj
    incl = ti >= tj
    eye = jnp.where(ti == tj, 1.0, 0.0).astype(F32)
    tri = jnp.where(incl, 1.0, 0.0).astype(BF16)
    lane = lax.broadcasted_iota(jnp.int32, (CHUNK, PAIR), 1)
    head_a = lane < HEAD_SIZE
    pi = lax.broadcasted_iota(jnp.int32, (PAIR, PAIR), 0)
    pj = lax.broadcasted_iota(jnp.int32, (PAIR, PAIR), 1)
    same_head = (pi < HEAD_SIZE) == (pj < HEAD_SIZE)
    on_diag = pi == pj
    bf = lambda x: x.astype(BF16)
    cat = lambda a, b: jnp.concatenate([a, b], axis=1)

    items = [(pr, c) for pr in range(npair) for c in range(nchunk)]
    sels = (head_a, jnp.logical_not(head_a))
    pre = []
    for pr, c in items:
        r, lw, k, v, kk, b = (x[c * CHUNK:(c + 1) * CHUNK, pr * PAIR:(pr + 1) * PAIR] for x in seqs[:6])
        g = _dot(tri, bf(lw))
        g = g + _dot(tri, bf(lw - bf(lw).astype(F32)))
        g_end = g[CHUNK - 1:CHUNK, :]
        e_neg = jnp.exp(-g)
        e_end = jnp.exp(g_end - g)
        kkt = kk * jnp.exp(g - lw)
        pre.append(dict(kkt=kkt, rt=r * jnp.exp(g), kkt_b=bf(kkt), bt_b=bf(b * e_neg), kt_b=bf(k * e_neg),
                        v_b=bf(v), bh_b=bf(b * e_end), kh_b=bf(k * e_end), gamma_end=jnp.exp(g_end)))
    tris = []
    for it in pre:
        for sel in sels:
            lhs = bf(jnp.concatenate([jnp.where(sel, it["kkt"], 0.0), jnp.where(sel, it["rt"], 0.0)], axis=0))
            xb = _dot_nt(lhs, it["bt_b"])
            xk = _dot_nt(lhs, it["kt_b"])
            tris.append((jnp.where(strict, xb[0:CHUNK], 0.0), jnp.where(strict, xk[0:CHUNK], 0.0),
                         jnp.where(incl, xb[CHUNK:2 * CHUNK], 0.0), jnp.where(incl, xk[CHUNK:2 * CHUNK], 0.0)))
    t_inv = [eye - t[0] for t in tris]
    power = [t[0] for t in tris]
    for _ in range(NEUMANN_STEPS):
        power = [_dot(bf(x), bf(x)) for x in power]
        t_inv = [t + _dot(bf(t), bf(x)) for t, x in zip(t_inv, power)]
    owner = [it for it in pre for _ in sels]
    lkv = [_dot(bf(t[1]), it["v_b"]) for t, it in zip(tris, owner)]
    tx = [_dot(bf(t), cat(it["kkt_b"], bf(x))) for t, it, x in zip(t_inv, owner, lkv)]
    mx = [_dot(bf(t[2]), bf(x)) for t, x in zip(tris, tx)]
    mv = [_dot(bf(t[3]), it["v_b"]) for t, it in zip(tris, owner)]
    pick = lambda xa, xb: jnp.where(head_a, xa, xb)
    affine = {}
    for n, ((pr, c), it) in enumerate(zip(items, pre)):
        ia, ib = 2 * n, 2 * n + 1
        w_all = pick(tx[ia][:, 0:PAIR], tx[ib][:, 0:PAIR])
        u_loc = pick(tx[ia][:, PAIR:2 * PAIR], tx[ib][:, PAIR:2 * PAIR])
        q_eff = it["rt"] - pick(mx[ia][:, 0:PAIR], mx[ib][:, 0:PAIR])
        o_loc = pick(mv[ia], mv[ib]) - pick(mx[ia][:, PAIR:2 * PAIR], mx[ib][:, PAIR:2 * PAIR])
        bx = _dot_tn(it["bh_b"], cat(bf(w_all), bf(u_loc)))
        kv = _dot_tn(it["kh_b"], it["v_b"])
        a_eff = jnp.where(on_diag, it["gamma_end"], 0.0) - jnp.where(same_head, bx[:, 0:PAIR], 0.0)
        p_loc = jnp.where(same_head, kv - bx[:, PAIR:2 * PAIR], 0.0)
        affine[(pr, c)] = (bf(a_eff), p_loc, bf(q_eff), o_loc)

    state = [p_ref[pr] for pr in range(npair)]
    for c in range(nchunk):
        for pr in range(npair):
            a_eff, p_loc, q_eff, o_loc = affine[(pr, c)]
            p_b = bf(state[pr])
            o_ref[c * CHUNK:(c + 1) * CHUNK, pr * PAIR:(pr + 1) * PAIR] = (_dot(q_eff, p_b) + o_loc).astype(BF16)
            state[pr] = _dot(a_eff, p_b) + p_loc
    for pr in range(npair):
        p_ref[pr] = state[pr]

    @pl.when(pl.program_id(1) == pl.num_programs(1) - 1)
    def _():
        for pr in range(npair):
            pf_ref[2 * pr] = state[pr][0:HEAD_SIZE, 0:HEAD_SIZE].T
            pf_ref[2 * pr + 1] = state[pr][HEAD_SIZE:PAIR, HEAD_SIZE:PAIR].T


def _chunk_scan(ush, nseq, lw, g512, nchunk):
    m = ush.shape[0]
    tt = nchunk * CHUNK
    steps = m // (nseq * tt)
    row = lambda a: a.reshape(1, -1)
    vec = lambda n: pl.BlockSpec((1, n), lambda s, i: (0, 0))
    blk = lambda w: pl.BlockSpec((tt, w), lambda s, i: (s * steps + i, 0))
    return pl.pallas_call(
        functools.partial(_chunk_kernel, nchunk=nchunk),
        grid=(nseq, steps),
        in_specs=[blk(SHIFT_W),
                  pl.BlockSpec((SUBLANES, SHIFT_W),
                               lambda s, i: (jnp.maximum((s * steps + i) * (tt // SUBLANES) - 1, 0), 0)),
                  vec(SHIFT_W), vec(W_A), vec(W_A),
                  pl.BlockSpec((2 * LORA, 2 * W_A), lambda s, i: (0, 0)),
                  vec(W_A), vec(W_A), vec(W_A),
                  pl.BlockSpec((W_A, W_A), lambda s, i: (0, 0))],
        out_specs=[blk(W_A), blk(W_A),
                   pl.BlockSpec((None, H_A, HEAD_SIZE, HEAD_SIZE), lambda s, i: (s, 0, 0, 0))],
        out_shape=[jax.ShapeDtypeStruct((m, W_A), BF16), jax.ShapeDtypeStruct((m, W_A), BF16),
                   jax.ShapeDtypeStruct((nseq, H_A, HEAD_SIZE, HEAD_SIZE), F32)],
        scratch_shapes=[pltpu.VMEM((W_A // PAIR, PAIR, PAIR), F32)],
        compiler_params=pltpu.CompilerParams(dimension_semantics=("arbitrary", "arbitrary")),
        name="rwkv_chunk_scan",
    )(ush, ush, row(lw["mu"]), row(lw["w0"]), row(lw["a0"]), lw["lora"], row(lw["k_k"]), row(lw["k_a"]),
      row(lw["r_k"]), g512)


def _attn_kernel(cur_ref, prev_ref, zb_ref, gtab_ref, sink_ref, yb_ref, bias_ref, *, blocks):
    s = pl.program_id(1)
    rows = Q_PER_KV * WINDOW

    @pl.when((pl.program_id(0) == 0) & (s == 0))
    def _():
        qi = lax.broadcasted_iota(jnp.int32, (WINDOW, 2 * WINDOW), 0)
        kj = lax.broadcasted_iota(jnp.int32, (WINDOW, 2 * WINDOW), 1)
        dist = qi + WINDOW - kj
        band = (dist >= 0) & (dist <= WINDOW)
        for g in range(KV_HEADS):
            for j in range(Q_PER_KV):
                h = g * Q_PER_KV + j
                row = jnp.broadcast_to(gtab_ref[h:h + 1, :], (WINDOW, BIAS_SPAN))
                toe = pltpu.roll(row, 0, 1, stride=1, stride_axis=0)[:, WINDOW:BIAS_SPAN]
                bias_ref[g, j * WINDOW:(j + 1) * WINDOW, :] = jnp.where(band, toe, NEG_INF)

    cur = cur_ref[...]
    prev = prev_ref[...]
    k_all = jnp.concatenate([prev[:, W_B:W_B + KV_W], cur[:, W_B:W_B + KV_W]], axis=0)
    v_all = jnp.concatenate([prev[:, W_B + KV_W:QKV_W], cur[:, W_B + KV_W:QKV_W]], axis=0)
    lane = lax.broadcasted_iota(jnp.int32, k_all.shape, 1)
    zero = jnp.zeros_like(k_all)
    k_g = [jnp.where(lane < HEAD_DIM, k_all, zero), jnp.where(lane >= HEAD_DIM, k_all, zero)]
    v_g = [jnp.where(lane < HEAD_DIM, v_all, zero), jnp.where(lane >= HEAD_DIM, v_all, zero)]
    ones = jnp.ones((2 * WINDOW, LANES), BF16)
    kj = lax.broadcasted_iota(jnp.int32, (rows, 2 * WINDOW), 1)
    insts = [(i, g) for i in range(blocks) for g in range(KV_HEADS)]
    keys_of = lambda x, i: x[i * WINDOW:(i + 2) * WINDOW]
    q4 = [jnp.concatenate([cur[i * WINDOW:(i + 1) * WINDOW, j * LANES:(j + 1) * LANES] for j in range(Q_PER_KV)],
                          axis=0) for i in range(blocks)]
    logits = []
    for i, g in insts:
        lg = _dot_nt(q4[i], keys_of(k_g[g], i)) + bias_ref[g]
        if i == 0:
            lg = jnp.where((kj >= WINDOW) | (s > 0), lg, NEG_INF)
        logits.append(lg)
    tops = [jnp.maximum(jnp.max(lg, axis=-1, keepdims=True), sink_ref[g]) for (i, g), lg in zip(insts, logits)]
    probs = [jnp.exp(lg - m).astype(BF16) for lg, m in zip(logits, tops)]
    dens = [_dot(p, ones) + jnp.exp(sink_ref[g] - m) for (i, g), p, m in zip(insts, probs, tops)]
    outs = [_dot(p, keys_of(v_g[g], i)) / den for (i, g), p, den in zip(insts, probs, dens)]
    for i in range(blocks):
        acc = outs[KV_HEADS * i] + outs[KV_HEADS * i + 1]
        r0 = i * WINDOW
        for j in range(Q_PER_KV):
            zb = zb_ref[r0:r0 + WINDOW, j * LANES:(j + 1) * LANES]
            yb_ref[r0:r0 + WINDOW, j * LANES:(j + 1) * LANES] = (
                acc[j * WINDOW:(j + 1) * WINDOW] * _silu(zb)).astype(BF16)


def _attn(qkv, zb, nb_seq, gtab, sink_col):
    m = qkv.shape[0]
    nblk = m // (nb_seq * WINDOW)
    blocks = _pick((ATTN_BLOCKS, 2, 1), nblk)
    steps = nblk // blocks
    rows = Q_PER_KV * WINDOW
    return pl.pallas_call(
        functools.partial(_attn_kernel, blocks=blocks),
        grid=(nb_seq, steps),
        in_specs=[pl.BlockSpec((blocks * WINDOW, QKV_W), lambda b, s: (b * steps + s, 0)),
                  pl.BlockSpec((WINDOW, QKV_W), lambda b, s: (jnp.maximum((b * steps + s) * blocks - 1, 0), 0)),
                  pl.BlockSpec((blocks * WINDOW, W_B), lambda b, s: (b * steps + s, 0)),
                  pl.BlockSpec((H_B, BIAS_SPAN), lambda b, s: (0, 0)),
                  pl.BlockSpec((KV_HEADS, rows, 1), lambda b, s: (0, 0, 0))],
        out_specs=pl.BlockSpec((blocks * WINDOW, W_B), lambda b, s: (b * steps + s, 0)),
        out_shape=jax.ShapeDtypeStruct((m, W_B), BF16),
        scratch_shapes=[pltpu.VMEM((KV_HEADS, rows, 2 * WINDOW), F32)],
        compiler_params=pltpu.CompilerParams(dimension_semantics=("arbitrary", "arbitrary")),
        name="swa_prompt",
    )(qkv, qkv, zb, gtab, sink_col)


def _attn_step_kernel(q_ref, kn_ref, vn_ref, zb_ref, ck_ref, cv_ref, bias_ref, bias0_ref, sink_ref, g128_ref,
                      yb_ref, ko_ref, vo_ref):
    q = q_ref[...].astype(F32)
    kn = kn_ref[...]
    vn = vn_ref[...]

    def rows_of(ref):
        x = ref[...]
        return jnp.concatenate([jnp.swapaxes(x[:, g], 1, 2) for g in range(KV_HEADS)], axis=-1)

    def store(ref, y):
        for g in range(KV_HEADS):
            ref[:, g] = jnp.swapaxes(y[:, :, g * HEAD_DIM:(g + 1) * HEAD_DIM], 1, 2)

    ck = rows_of(ck_ref)
    cv = rows_of(cv_ref)
    bt = q.shape[0]
    for r in range(Q_PER_KV):
        qsel = q[:, r * LANES:(r + 1) * LANES]
        prod = (ck * qsel[:, None, :]).reshape(bt * WINDOW, KV_W)
        lg = _split_dot(prod, g128_ref[...]).reshape(bt, WINDOW, KV_W) + bias_ref[r]
        lgn = _split_dot(kn * qsel, g128_ref[...]) + bias0_ref[r]
        s = sink_ref[r]
        m = jnp.maximum(jnp.maximum(jnp.max(lg, axis=1), lgn), s)
        p = jnp.exp(lg - m[:, None, :])
        pn = jnp.exp(lgn - m)
        den = jnp.sum(p, axis=1) + pn + jnp.exp(s - m)
        o = (jnp.sum(p * cv, axis=1) + pn * vn) / den
        zb = zb_ref[:, r * LANES:(r + 1) * LANES]
        yb_ref[:, r * LANES:(r + 1) * LANES] = (o * _silu(zb)).astype(BF16)
    j = lax.broadcasted_iota(jnp.int32, ck.shape, 1)
    store(ko_ref, jnp.where(j == WINDOW - 1, kn[:, None, :], pltpu.roll(ck, WINDOW - 1, axis=1)))
    store(vo_ref, jnp.where(j == WINDOW - 1, vn[:, None, :], pltpu.roll(cv, WINDOW - 1, axis=1)))


def _attn_step(qkv, kv32, zb, cache_k, cache_v, bias_rows, bias0, sink_rows, g128, bt):
    nb = qkv.shape[0]
    cache_spec = pl.BlockSpec((bt, KV_HEADS, HEAD_DIM, WINDOW), lambda i: (i, 0, 0, 0))
    return pl.pallas_call(
        _attn_step_kernel,
        grid=(nb // bt,),
        in_specs=[pl.BlockSpec((bt, W_B), lambda i: (i, 0)),
                  pl.BlockSpec((bt, KV_W), lambda i: (i, 0)),
                  pl.BlockSpec((bt, KV_W), lambda i: (i, 1)),
                  pl.BlockSpec((bt, W_B), lambda i: (i, 0)),
                  cache_spec, cache_spec,
                  pl.BlockSpec((Q_PER_KV, WINDOW, KV_W), lambda i: (0, 0, 0)),
                  pl.BlockSpec((Q_PER_KV, 1, KV_W), lambda i: (0, 0, 0)),
                  pl.BlockSpec((Q_PER_KV, 1, KV_W), lambda i: (0, 0, 0)),
                  pl.BlockSpec((KV_W, KV_W), lambda i: (0, 0))],
        out_specs=[pl.BlockSpec((bt, W_B), lambda i: (i, 0)), cache_spec, cache_spec],
        out_shape=[jax.ShapeDtypeStruct((nb, W_B), BF16),
                   jax.ShapeDtypeStruct(cache_k.shape, F32),
                   jax.ShapeDtypeStruct(cache_v.shape, F32)],
        name="swa_step",
    )(qkv, kv32, kv32, zb, cache_k, cache_v, bias_rows, bias0, sink_rows, g128)


def _merge_kernel(yt_ref, zat_ref, yb_ref, ga_ref, gb_ref, x_ref, gate_ref, woa_ref, wob_ref, wout_ref, out_ref):
    yt = yt_ref[...]
    yat = (yt.reshape(W_A, yt.shape[2]) * _silu(zat_ref[...])).astype(BF16)
    pa = _dot_tn(yat, woa_ref[...])
    pb = _dot(yb_ref[...], wob_ref[...])
    merged = _sigmoid(ga_ref[...]) * pa + _sigmoid(gb_ref[...]) * pb
    out_ref[...] = x_ref[...] + gate_ref[...] * _dot(merged.astype(BF16), wout_ref[...])


def _merge(yt, zat, yb, gates, x, gate, woa_bf, wob_bf, wout_bf, tm, rows_per_seq, per_row):
    m = x.shape[0]
    full = lambda c: pl.BlockSpec((tm, D_MODEL), lambda i: (i, c))
    per_seq = rows_per_seq // tm
    y_spec = pl.BlockSpec((HEAD_SIZE, H_A, tm), lambda i: (0, i // per_seq, i % per_seq))
    return pl.pallas_call(
        _merge_kernel,
        grid=(m // tm,),
        in_specs=[y_spec, _t_spec(W_A, tm, rows_per_seq),
                  pl.BlockSpec((tm, W_B), lambda i: (i, 0)), full(0), full(1), full(0),
                  _mod_spec(per_row, tm, rows_per_seq),
                  pl.BlockSpec((W_A, D_MODEL), lambda i: (0, 0)),
                  pl.BlockSpec((W_B, D_MODEL), lambda i: (0, 0)),
                  pl.BlockSpec((D_MODEL, D_MODEL), lambda i: (0, 0))],
        out_specs=full(0),
        out_shape=jax.ShapeDtypeStruct((m, D_MODEL), F32),
        name="merge_out",
    )(yt, zat, yb, gates, gates, x, gate, woa_bf, wob_bf, wout_bf)


def _merge_rows_kernel(o_ref, bonus_ref, za_ref, yb_ref, ga_ref, gb_ref, x_ref, gate_ref, lng_ref, lnb_ref,
                       g512_ref, woa_ref, wob_ref, wout_ref, out_ref):
    o_b = o_ref[...]
    o = o_b.astype(F32)
    mean = _dot(o_b, g512_ref[...]) * (1.0 / HEAD_SIZE)
    c = o - mean
    var = _head_sum(c * c, g512_ref[...]) * (1.0 / HEAD_SIZE)
    on = c * lax.rsqrt(var + GN_EPS) * lng_ref[...] + lnb_ref[...]
    ya = (on + bonus_ref[...].astype(F32)) * _silu(za_ref[...])
    pa = _dot(ya.astype(BF16), woa_ref[...])
    pb = _dot(yb_ref[...], wob_ref[...])
    merged = _sigmoid(ga_ref[...]) * pa + _sigmoid(gb_ref[...]) * pb
    out_ref[...] = x_ref[...] + gate_ref[...] * _dot(merged.astype(BF16), wout_ref[...])


def _merge_rows(o, bonus, za, yb, gates, x, gate, lnx_g, lnx_b, g512, woa_bf, wob_bf, wout_bf, tm, rows_per_seq):
    m = x.shape[0]
    half = pl.BlockSpec((tm, W_A), lambda i: (i, 0))
    full = lambda c: pl.BlockSpec((tm, D_MODEL), lambda i: (i, c))
    vec = pl.BlockSpec((1, W_A), lambda i: (0, 0))
    return pl.pallas_call(
        _merge_rows_kernel,
        grid=(m // tm,),
        in_specs=[half, half, half, half, full(0), full(1), full(0),
                  _mod_spec(False, tm, rows_per_seq), vec, vec,
                  pl.BlockSpec((W_A, W_A), lambda i: (0, 0)),
                  pl.BlockSpec((W_A, D_MODEL), lambda i: (0, 0)),
                  pl.BlockSpec((W_B, D_MODEL), lambda i: (0, 0)),
                  pl.BlockSpec((D_MODEL, D_MODEL), lambda i: (0, 0))],
        out_specs=full(0),
        out_shape=jax.ShapeDtypeStruct((m, D_MODEL), F32),
        name="merge_out_rows",
    )(o, bonus, za, yb, gates, gates, x, gate, lnx_g.reshape(1, W_A), lnx_b.reshape(1, W_A), g512,
      woa_bf, wob_bf, wout_bf)


def _t5_bucket(dist):
    max_exact = N_BUCKETS // 2
    d = jnp.maximum(dist, 0)
    log_ratio = jnp.log(jnp.maximum(d, 1).astype(F32) / max_exact) / math.log(MAX_DISTANCE / max_exact)
    large = jnp.minimum(max_exact + (log_ratio * (N_BUCKETS - max_exact)).astype(jnp.int32), N_BUCKETS - 1)
    return jnp.where(d < max_exact, d, large)


def _block_ones(n, blk):
    i = np.arange(n) // blk
    return jnp.asarray((i[:, None] == i[None, :]).astype(np.float32), dtype=BF16)


def _rwkv_sample(usht, shift0, state0, lw):
    nb = usht.shape[2]
    outs = _prep(usht, _shift_major(shift0, 1).T[None], lw["mu"], lw["w0"], lw["a0"], lw["lorat"], lw["k_k"],
                 lw["k_a"], lw["r_k"], nb, True)
    tile = lambda a: jnp.broadcast_to(a.reshape(H_A, HEAD_SIZE, 1), (H_A, HEAD_SIZE, nb))
    y, state = _step([a.transpose(1, 0, 2) for a in outs[:6]], outs[6], tile(lw["lnx_g"]), tile(lw["lnx_b"]),
                     state0.transpose(1, 2, 3, 0))
    return y.transpose(1, 0, 2), state.transpose(3, 0, 1, 2)


def kernel(x_prompt, x_sample, c_prompt, c_sample, state_wkv, state_shift, cache_k, cache_v, norm_g, w_ada, b_ada, w_in, mu_shift, w0, w_decay_up, a0, w_a_up, k_k, k_a, r_k, lnx_g, lnx_b, w_o_a, q_norm_g, k_norm_g, rel_bias, sinks, w_o_b, w_out):
    nbp, t_len, _ = x_prompt.shape
    nbs = x_sample.shape[0]
    depth = norm_g.shape[0]
    mp = nbp * t_len
    assert (nbp * H_A) % LANES == 0 and nbs % LANES == 0 and t_len % LANES == 0
    g512 = _block_ones(W_B, HEAD_DIM)
    g128 = _block_ones(KV_W, HEAD_DIM)

    gtab = rel_bias[_t5_bucket(2 * WINDOW - jnp.arange(BIAS_SPAN))].astype(F32).T
    bias_s = rel_bias[_t5_bucket(WINDOW - jnp.arange(WINDOW + 1))].astype(F32)
    pair = lambda a: jnp.concatenate([jnp.repeat(a[..., 0:Q_PER_KV, None], HEAD_DIM, axis=-1),
                                      jnp.repeat(a[..., Q_PER_KV:H_B, None], HEAD_DIM, axis=-1)], axis=-1)
    bias_rows = pair(bias_s[:WINDOW]).transpose(1, 0, 2)
    bias0 = pair(bias_s[WINDOW:]).transpose(1, 0, 2)

    c_all = jnp.concatenate([c_prompt, c_sample], axis=0)
    xp = x_prompt.reshape(mp, D_MODEL)
    xs = x_sample.reshape(nbs, D_MODEL)
    tm_p = _pick((512, 256, 128), t_len)
    tm_in = _pick((512, 256, 128), t_len)
    outs = {k: [] for k in ("wkv_p", "shift_p", "kw_p", "vw_p", "wkv_s", "shift_s", "kw_s", "vw_s")}
    for l in range(depth):
        wl = w_in[l].astype(BF16)
        base = SHIFT_W
        z_a, q, kb, vb, z_b, gts = (wl[:, base:base + 512], wl[:, base + 512:base + 1024],
                                    wl[:, base + 1024:base + 1152], wl[:, base + 1152:base + 1280],
                                    wl[:, base + 1280:base + 1792], wl[:, base + 1792:])
        w_bf = jnp.concatenate([_head_pairs(q, 1), kb, vb, _head_pairs(z_b, 1), gts], axis=1)
        wa_bf = jnp.concatenate([wl[:, :SHIFT_W], z_a], axis=1)
        wt_bf = jnp.concatenate([_shift_major(wl[:, :SHIFT_W], 1), _chan_major(z_a, 1)], axis=1).T
        zeros = jnp.zeros((LORA, W_A), F32)
        lora = jnp.concatenate([jnp.concatenate([w_decay_up[l], zeros], axis=1),
                                jnp.concatenate([zeros, w_a_up[l]], axis=1)], axis=0).astype(BF16)
        lorat = jnp.concatenate([jnp.concatenate([_chan_major(w_decay_up[l], 1), zeros], axis=1),
                                 jnp.concatenate([zeros, _chan_major(w_a_up[l], 1)], axis=1)],
                                axis=0).T.astype(BF16)
        cm = lambda a: _chan_major(a.reshape(-1), 0)
        lw_p = dict(mu=mu_shift[l], w0=w0[l], a0=a0[l], lora=lora, k_k=k_k[l], k_a=k_a[l], r_k=r_k[l].reshape(-1))
        lw = dict(mu=_shift_major(mu_shift[l], 0), w0=cm(w0[l]), a0=cm(a0[l]), lorat=lorat,
                  k_k=cm(k_k[l]), k_a=cm(k_a[l]), r_k=cm(r_k[l]), lnx_g=lnx_g[l], lnx_b=lnx_b[l])
        woa_rows_bf = w_o_a[l].astype(BF16)
        woa_bf, wout_bf = _chan_major(woa_rows_bf, 0), w_out[l].astype(BF16)
        wob_bf = _head_pairs(w_o_b[l].astype(BF16), 0)
        q_gain = jnp.tile(q_norm_g[l], H_B).reshape(1, W_B)
        k_gain = jnp.tile(k_norm_g[l], KV_HEADS).reshape(1, KV_W)
        sink_col = jnp.repeat(sinks[l].reshape(KV_HEADS, Q_PER_KV), WINDOW, axis=1).reshape(
            KV_HEADS, Q_PER_KV * WINDOW, 1)
        sink_rows = pair(sinks[l].reshape(1, H_B)).transpose(1, 0, 2)

        mod = _ada(c_all, w_ada[l].astype(BF16), b_ada[l])
        shift, scale, gate = mod[:, :D_MODEL], mod[:, D_MODEL:2 * D_MODEL], mod[:, 2 * D_MODEL:]

        sp, scp, gp = (a[:nbp].reshape(nbp, 1, D_MODEL) for a in (shift, scale, gate))
        ush, za, qkv, kv32, zb, gts_p = _in_proj(xp, norm_g[l], scp, sp, w_bf, wa_bf, q_gain, k_gain, g512, g128,
                                                 tm_in, t_len, False, False)
        o, bonus, wkv = _chunk_scan(ush, nbp, lw_p, g512, _pick((CHUNKS_PER_STEP, 1), t_len // CHUNK))
        yb = _attn(qkv, zb, nbp, gtab, sink_col)
        xp = _merge_rows(o, bonus, za, yb, gts_p, xp, gp, lnx_g[l], lnx_b[l], g512, woa_rows_bf, wob_bf, wout_bf,
                         tm_p, t_len)
        win = kv32.reshape(nbp, t_len, 2, KV_HEADS, HEAD_DIM)[:, t_len - WINDOW:]
        outs["wkv_p"].append(wkv)
        outs["shift_p"].append(ush.reshape(nbp, t_len, SHIFT_W)[:, t_len - 1])
        outs["kw_p"].append(win[:, :, 0])
        outs["vw_p"].append(win[:, :, 1])

        ss, scs, gs = shift[nbp:], scale[nbp:], gate[nbp:]
        usht, zat, qkv, kv32, zb, gts_s = _in_proj(xs, norm_g[l], scs, ss, w_bf, wt_bf, q_gain, k_gain, g512, g128,
                                                   nbs, nbs, True, True)
        yt, wkv = _rwkv_sample(usht, state_shift[l], state_wkv[l], lw)
        yb, kwin, vwin = _attn_step(qkv, kv32, zb, cache_k[l].transpose(0, 2, 3, 1), cache_v[l].transpose(0, 2, 3, 1),
                                    bias_rows, bias0, sink_rows, g128, _pick((16, 8), nbs))
        xs = _merge(yt, zat, yb, gts_s, xs, gs, woa_bf, wob_bf, wout_bf, nbs, nbs, True)
        outs["wkv_s"].append(wkv)
        outs["shift_s"].append(_shift_major(usht[0].T, 1, inverse=True))
        outs["kw_s"].append(kwin.transpose(0, 3, 1, 2))
        outs["vw_s"].append(vwin.transpose(0, 3, 1, 2))

    st = lambda k: jnp.stack(outs[k])
    return (xp.reshape(nbp, t_len, D_MODEL), xs.reshape(nbs, 1, D_MODEL),
            st("wkv_p"), st("shift_p"), st("kw_p"), st("vw_p"),
            st("wkv_s"), st("shift_s"), st("kw_s"), st("vw_s"))
```

```python
import functools
import math

import numpy as np
import jax
import jax.numpy as jnp
from jax import lax
from jax.experimental import pallas as pl
from jax.experimental.pallas import tpu as pltpu

F32 = jnp.float32
BF16 = jnp.bfloat16

D_MODEL = 1024
W_A = 512
HEAD_SIZE = 64
H_A = W_A // HEAD_SIZE
LORA = 64
SHIFT_W = 3 * W_A + 2 * LORA
W_B = 512
HEAD_DIM = 64
H_B = W_B // HEAD_DIM
KV_HEADS = 2
Q_PER_KV = H_B // KV_HEADS
KV_W = KV_HEADS * HEAD_DIM
WINDOW = 128
N_BUCKETS = 32
MAX_DISTANCE = 128
NORM_EPS = 1e-6
GN_EPS = 64e-5
NEG_INF = -1e30
QKV_W = W_B + 2 * KV_W
GATE_W = 2 * D_MODEL
ROW_COLS = QKV_W + W_B + GATE_W
T_COLS = SHIFT_W + W_A
LANES = 128
SUBLANES = 8
def _head_pairs(x, axis):
    s = x.shape
    x = x.reshape(s[:axis] + (KV_HEADS, Q_PER_KV, HEAD_DIM) + s[axis + 1:])
    return jnp.swapaxes(x, axis, axis + 1).reshape(s)


def _chan_major(x, axis, inverse=False):
    s = x.shape
    split = (HEAD_SIZE, H_A) if inverse else (H_A, HEAD_SIZE)
    x = x.reshape(s[:axis] + split + s[axis + 1:])
    return jnp.swapaxes(x, axis, axis + 1).reshape(s)


def _shift_major(x, axis, inverse=False):
    parts = [lax.slice_in_dim(x, i * W_A, (i + 1) * W_A, axis=axis) for i in range(3)]
    tail = lax.slice_in_dim(x, 3 * W_A, SHIFT_W, axis=axis)
    return jnp.concatenate([_chan_major(p, axis, inverse) for p in parts] + [tail], axis=axis)
ATTN_BLOCKS = 4
BIAS_SPAN = 3 * WINDOW
CHUNK = 64
CHUNKS_PER_STEP = 8
PAIR = 2 * HEAD_SIZE
GROUP = PAIR
NEUMANN_STEPS = CHUNK.bit_length() - 2


def _dot(a, b):
    return jnp.dot(a, b, preferred_element_type=F32)


def _dot_nt(a, b):
    return lax.dot_general(a, b, (((1,), (1,)), ((), ())), preferred_element_type=F32)


def _dot_tn(a, b):
    return lax.dot_general(a, b, (((0,), (0,)), ((), ())), preferred_element_type=F32)


def _head_sum(x, g):
    return _dot(x.astype(BF16), g)


def _split_dot(x, g):
    hi = x.astype(BF16)
    lo = (x - hi.astype(F32)).astype(BF16)
    return _dot(hi, g) + _dot(lo, g)


def _silu(x):
    x = x.astype(F32)
    return x * jax.nn.sigmoid(x)


def _sigmoid(x):
    return jax.nn.sigmoid(x.astype(F32))


def _pick(cands, n):
    for c in cands:
        if n % c == 0:
            return c
    return n


def _ada_kernel(c_ref, w_ref, b_ref, o_ref):
    c = c_ref[...]
    o_ref[...] = _dot(_silu(c).astype(BF16), w_ref[...]) + b_ref[...]


def _ada(c, w_bf, b):
    nb, d = c.shape
    n = w_bf.shape[1]
    tn = 1024
    return pl.pallas_call(
        _ada_kernel,
        grid=(n // tn,),
        in_specs=[pl.BlockSpec((nb, d), lambda j: (0, 0)),
                  pl.BlockSpec((d, tn), lambda j: (0, j)),
                  pl.BlockSpec((1, tn), lambda j: (0, j))],
        out_specs=pl.BlockSpec((nb, tn), lambda j: (0, j)),
        out_shape=jax.ShapeDtypeStruct((nb, n), F32),
        name="ada",
    )(c, w_bf, b.reshape(1, n))


def _in_proj_kernel(x_ref, g_ref, scale_ref, shift_ref, w_ref, wa_ref, qg_ref, kg_ref, g512_ref, g128_ref,
                    ush_ref, za_ref, qkv_ref, kv32_ref, zb_ref, gate_ref, *, transposed):
    x = x_ref[...]
    ms = jnp.mean(x * x, axis=-1, keepdims=True)
    h = (x * lax.rsqrt(ms + NORM_EPS)) * g_ref[...]
    h = h * (1.0 + scale_ref[...]) + shift_ref[...]
    hb = h.astype(BF16)
    if transposed:
        ush_ref[...] = _dot_nt(wa_ref[0:SHIFT_W, :], hb)
        za_ref[...] = _dot_nt(wa_ref[SHIFT_W:T_COLS, :], hb).astype(BF16)
    else:
        ush_ref[...] = _dot(hb, wa_ref[:, 0:SHIFT_W])
        za_ref[...] = _dot(hb, wa_ref[:, SHIFT_W:T_COLS]).astype(BF16)
    qkv = _dot(hb, w_ref[:, 0:QKV_W])
    q, k, v = qkv[:, 0:W_B], qkv[:, W_B:W_B + KV_W], qkv[:, W_B + KV_W:QKV_W]
    qn = q * lax.rsqrt(_head_sum(q * q, g512_ref[...]) * (1.0 / HEAD_DIM) + NORM_EPS) * qg_ref[...]
    kn = k * lax.rsqrt(_head_sum(k * k, g128_ref[...]) * (1.0 / HEAD_DIM) + NORM_EPS) * kg_ref[...]
    qkv_ref[:, 0:W_B] = (qn * (HEAD_DIM ** -0.5)).astype(BF16)
    qkv_ref[:, W_B:W_B + KV_W] = kn.astype(BF16)
    qkv_ref[:, W_B + KV_W:QKV_W] = v.astype(BF16)
    kv32_ref[:, 0:KV_W] = kn
    kv32_ref[:, KV_W:2 * KV_W] = v
    zb_ref[...] = _dot(hb, w_ref[:, QKV_W:QKV_W + W_B]).astype(BF16)
    gate_ref[...] = _dot(hb, w_ref[:, QKV_W + W_B:ROW_COLS]).astype(BF16)


def _mod_spec(per_row, tm, rows_per_seq):
    if per_row:
        return pl.BlockSpec((tm, D_MODEL), lambda i: (i, 0))
    return pl.BlockSpec((None, 1, D_MODEL), lambda i: ((i * tm) // rows_per_seq, 0, 0))


def _t_spec(cols, tm, rows_per_seq):
    per_seq = rows_per_seq // tm
    return pl.BlockSpec((None, cols, tm), lambda i: (i // per_seq, 0, i % per_seq))


def _in_proj(x, norm_g, scale, shift, w_bf, wa_bf, q_gain, k_gain, g512, g128, tm, rows_per_seq, per_row,
             transposed):
    m = x.shape[0]
    nseq = m // rows_per_seq
    const = lambda r, c: pl.BlockSpec((r, c), lambda i: (0, 0))
    rows = lambda w: pl.BlockSpec((tm, w), lambda i: (i, 0))
    widths = (QKV_W, 2 * KV_W, W_B, GATE_W)
    dtypes = (BF16, F32, BF16, BF16)
    if transposed:
        a_specs = [_t_spec(SHIFT_W, tm, rows_per_seq), _t_spec(W_A, tm, rows_per_seq)]
        a_shapes = [jax.ShapeDtypeStruct((nseq, SHIFT_W, rows_per_seq), F32),
                    jax.ShapeDtypeStruct((nseq, W_A, rows_per_seq), BF16)]
    else:
        a_specs = [rows(SHIFT_W), rows(W_A)]
        a_shapes = [jax.ShapeDtypeStruct((m, SHIFT_W), F32), jax.ShapeDtypeStruct((m, W_A), BF16)]
    return pl.pallas_call(
        functools.partial(_in_proj_kernel, transposed=transposed),
        grid=(m // tm,),
        in_specs=[rows(D_MODEL),
                  const(1, D_MODEL),
                  _mod_spec(per_row, tm, rows_per_seq),
                  _mod_spec(per_row, tm, rows_per_seq),
                  const(D_MODEL, ROW_COLS), const(*wa_bf.shape),
                  const(1, W_B), const(1, KV_W), const(W_B, W_B), const(KV_W, KV_W)],
        out_specs=a_specs + [rows(w) for w in widths],
        out_shape=a_shapes + [jax.ShapeDtypeStruct((m, w), dt) for w, dt in zip(widths, dtypes)],
        name="in_proj",
    )(x, norm_g.reshape(1, D_MODEL), scale, shift, w_bf, wa_bf, q_gain, k_gain, g512, g128)


def _prep_kernel(u_ref, prev_ref, shift0_ref, mu_ref, w0_ref, a0_ref, lora_ref, kk_ref, ka_ref, rk_ref,
                 r_o, w_o, k_o, v_o, kk_o, b_o, coef_o, *, per_token_state):
    tm = u_ref.shape[1]
    heads = lambda x: x.reshape(HEAD_SIZE, H_A, LANES)
    lane = lax.broadcasted_iota(jnp.int32, (SHIFT_W, LANES), 1)
    row = lax.broadcasted_iota(jnp.int32, (2 * LORA, LANES), 0)
    if not per_token_state:
        before = jnp.where(pl.program_id(1) == 0, shift0_ref[...], prev_ref[...])
        rolled_before = pltpu.roll(before, 1, axis=1)
    for ci in range(tm // LANES):
        cols = slice(ci * LANES, (ci + 1) * LANES)
        u = u_ref[:, cols]
        if per_token_state:
            u_prev = shift0_ref[:, cols]
        else:
            rolled = pltpu.roll(u, 1, axis=1)
            u_prev = jnp.where(lane == 0, rolled_before, rolled)
            rolled_before = rolled
        xs = u + (u_prev - u) * mu_ref[...]
        r = xs[0:W_A]
        k = xs[W_A:2 * W_A]
        v = xs[2 * W_A:3 * W_A]
        tail = xs[3 * W_A:SHIFT_W]
        lora_in = jnp.where(row < LORA, jnp.tanh(tail), tail).astype(BF16)
        up = _dot(lora_ref[...], lora_in)
        neg = -(w0_ref[...] + up[0:W_A])
        softplus = jnp.maximum(neg, 0.0) + jnp.log(1.0 + jnp.exp(-jnp.abs(neg)))
        w_log = -softplus - 0.5
        decay = jnp.exp(-jnp.exp(w_log))
        a = jax.nn.sigmoid(a0_ref[...] + up[W_A:2 * W_A])
        kk = heads(k * kk_ref[...])
        norm = jnp.sqrt(jnp.sum(kk * kk, axis=0, keepdims=True))
        kk = kk / jnp.maximum(norm, 1e-12)
        k_mod = k * (1.0 + (a - 1.0) * ka_ref[...])
        r_o[:, :, cols] = heads(r)
        w_o[:, :, cols] = heads(decay)
        k_o[:, :, cols] = heads(k_mod)
        v_o[:, :, cols] = heads(v)
        kk_o[:, :, cols] = kk
        b_o[:, :, cols] = kk * heads(a)
        coef_o[:, cols] = jnp.sum(heads(r * k_mod * rk_ref[...]), axis=0)


def _prep(usht, shift0t, mu, w0, a0, lorat_bf, k_k, k_a, r_k, tm, per_token_state):
    nseq, _, t_len = usht.shape
    col = lambda a: jnp.broadcast_to(a.reshape(-1, 1), (a.size, LANES))
    cvec = lambda n: pl.BlockSpec((n, LANES), lambda s, j: (0, 0))
    blk = lambda cols: pl.BlockSpec((None, cols, tm), lambda s, j: (s, 0, j))
    if per_token_state:
        prev_spec = pl.BlockSpec((None, SHIFT_W, LANES), lambda s, j: (0, 0, 0))
        shift0_spec = blk(SHIFT_W)
    else:
        prev_spec = pl.BlockSpec((None, SHIFT_W, LANES),
                                 lambda s, j: (s, 0, jnp.maximum(j * (tm // LANES) - 1, 0)))
        shift0_spec = pl.BlockSpec((None, SHIFT_W, LANES), lambda s, j: (s, 0, 0))
    chan = pl.BlockSpec((HEAD_SIZE, H_A, tm), lambda s, j: (0, s, j))
    return pl.pallas_call(
        functools.partial(_prep_kernel, per_token_state=per_token_state),
        grid=(nseq, t_len // tm),
        in_specs=[blk(SHIFT_W), prev_spec, shift0_spec, cvec(SHIFT_W), cvec(W_A), cvec(W_A),
                  pl.BlockSpec((2 * W_A, 2 * LORA), lambda s, j: (0, 0)), cvec(W_A), cvec(W_A), cvec(W_A)],
        out_specs=[chan] * 6 + [pl.BlockSpec((H_A, tm), lambda s, j: (s, j))],
        out_shape=[jax.ShapeDtypeStruct((HEAD_SIZE, nseq * H_A, t_len), F32)] * 6
                  + [jax.ShapeDtypeStruct((nseq * H_A, t_len), F32)],
        name="rwkv_prep",
    )(usht, usht, shift0t, col(mu), col(w0), col(a0), lorat_bf, col(k_k), col(k_a), col(r_k))


def _step_kernel(r_ref, w_ref, k_ref, v_ref, kk_ref, b_ref, coef_ref, lng_ref, lnb_ref, s_ref,
                 y_ref, so_ref, o_ref):
    kk, w, b, km, r = kk_ref[...], w_ref[...], b_ref[...], k_ref[...], r_ref[...]
    for vi in range(HEAD_SIZE):
        s = s_ref[vi]
        sa = -jnp.sum(s * kk, axis=0, keepdims=True)
        s = s * w + sa * b + v_ref[pl.ds(vi, 1), :] * km
        so_ref[vi] = s
        o_ref[pl.ds(vi, 1), :] = jnp.sum(s * r, axis=0, keepdims=True)
    o = o_ref[...]
    mean = jnp.sum(o, axis=0, keepdims=True) * (1.0 / HEAD_SIZE)
    c = o - mean
    var = jnp.sum(c * c, axis=0, keepdims=True) * (1.0 / HEAD_SIZE)
    coef = coef_ref[pl.ds(pl.program_id(0), 1), :]
    y_ref[...] = c * lax.rsqrt(var + GN_EPS) * lng_ref[...] + lnb_ref[...] + coef * v_ref[...]


def _step(seqs, coef, lng, lnb, state):
    nb = coef.shape[1]
    seq_spec = pl.BlockSpec((None, HEAD_SIZE, nb), lambda h: (h, 0, 0))
    state_spec = pl.BlockSpec((None, HEAD_SIZE, HEAD_SIZE, nb), lambda h: (h, 0, 0, 0))
    return pl.pallas_call(
        _step_kernel,
        grid=(H_A,),
        in_specs=[seq_spec] * 6 + [pl.BlockSpec((H_A, nb), lambda h: (0, 0)), seq_spec, seq_spec, state_spec],
        out_specs=[seq_spec, state_spec],
        out_shape=[jax.ShapeDtypeStruct((H_A, HEAD_SIZE, nb), F32),
                   jax.ShapeDtypeStruct((H_A, HEAD_SIZE, HEAD_SIZE, nb), F32)],
        scratch_shapes=[pltpu.VMEM((HEAD_SIZE, nb), F32)],
        name="rwkv_step",
    )(*seqs, coef, lng, lnb, state)


def _prep_rows(u_ref, prev_ref, first, mu_ref, w0_ref, a0_ref, lora_ref, kk_ref, ka_ref, rk_ref, g_ref):
    u = u_ref[...]
    prev_row = jnp.where(first, 0.0, prev_ref[SUBLANES - 1:SUBLANES, :])
    row = lax.broadcasted_iota(jnp.int32, u.shape, 0)
    u_prev = jnp.where(row == 0, prev_row, pltpu.roll(u, 1, axis=0))
    xs = u + (u_prev - u) * mu_ref[...]
    r = xs[:, 0:W_A]
    k = xs[:, W_A:2 * W_A]
    v = xs[:, 2 * W_A:3 * W_A]
    tail = xs[:, 3 * W_A:SHIFT_W]
    lane = lax.broadcasted_iota(jnp.int32, tail.shape, 1)
    lora_in = jnp.where(lane < LORA, jnp.tanh(tail), tail).astype(BF16)
    up = _dot(lora_in, lora_ref[...])
    neg = -(w0_ref[...] + up[:, 0:W_A])
    softplus = jnp.maximum(neg, 0.0) + jnp.log(1.0 + jnp.exp(-jnp.abs(neg)))
    w_log = -softplus - 0.5
    a = jax.nn.sigmoid(a0_ref[...] + up[:, W_A:2 * W_A])
    kk = k * kk_ref[...]
    norm = jnp.sqrt(_head_sum(kk * kk, g_ref[...]))
    kk = kk / jnp.maximum(norm, 1e-12)
    k_mod = k * (1.0 + (a - 1.0) * ka_ref[...])
    lw = -jnp.exp(w_log)
    bonus = _head_sum(r * k_mod * rk_ref[...], g_ref[...]) * v
    return r, lw, k_mod, v, kk, kk * a, bonus


def _chunk_kernel(u_ref, prev_ref, mu_ref, w0_ref, a0_ref, lora_ref, kkp_ref, ka_ref, rk_ref, g_ref,
                  o_ref, bonus_ref, pf_ref, p_ref, *, nchunk):
    npair = W_A // PAIR
    first = pl.program_id(1) == 0

    @pl.when(first)
    def _():
        p_ref[...] = jnp.zeros(p_ref.shape, F32)

    seqs = _prep_rows(u_ref, prev_ref, first, mu_ref, w0_ref, a0_ref, lora_ref, kkp_ref, ka_ref, rk_ref, g_ref)
    bonus_ref[...] = seqs[6].astype(BF16)

    tq = lax.broadcasted_iota(jnp.int32, (CHUNK, GROUP), 0)
    lq = lax.broadcasted_iota(jnp.int32, (CHUNK, GROUP), 1)
    head_of = lq // HEAD_SIZE
    col = lq % HEAD_SIZE
    strict = tq > col
    incl = tq >= col
    eye = jnp.where(tq == col, 1.0, 0.0).astype(F32)
    ti = lax.broadcasted_iota(jnp.int32, (CHUNK, CHUNK), 0)
    tj = lax.broadcasted_iota(jnp.int32, (CHUNK, CHUNK), 1)
    tri = jnp.where(ti >= tj, 1.0, 0.0).astype(BF16)
    pi = lax.broadcasted_iota(jnp.int32, (PAIR, PAIR), 0)
    pj = lax.broadcasted_iota(jnp.int32, (PAIR, PAIR), 1)
    same_head = (pi < HEAD_SIZE) == (pj < HEAD_SIZE)
    on_diag = pi == pj
    bf = lambda x: x.astype(BF16)
    cat = lambda a, b: jnp.concatenate([a, b], axis=1)

    def spread(x):
        return jnp.concatenate([jnp.where(head_of == j, x, jnp.zeros_like(x)) for j in range(GROUP // HEAD_SIZE)],
                               axis=0)

    items = [(gr, c) for gr in range(W_A // GROUP) for c in range(nchunk)]
    pre = []
    for gr, c in items:
        r, lw, k, v, kk, b = (x[c * CHUNK:(c + 1) * CHUNK, gr * GROUP:(gr + 1) * GROUP] for x in seqs[:6])
        g = _dot(tri, bf(lw))
        g = g + _dot(tri, bf(lw - bf(lw).astype(F32)))
        g_end = g[CHUNK - 1:CHUNK, :]
        e_neg = jnp.exp(-g)
        e_end = jnp.exp(g_end - g)
        kkt = kk * jnp.exp(g - lw)
        rt = r * jnp.exp(g)
        pre.append(dict(rt=rt, lhs=bf(jnp.concatenate([kkt, rt], axis=0)), kkt_b=bf(kkt), bt_b=bf(b * e_neg),
                        kt_b=bf(k * e_neg), v_b=bf(v), bh_b=bf(b * e_end), kh_b=bf(k * e_end),
                        gamma_end=jnp.exp(g_end)))
    xb = [_dot_nt(it["lhs"], spread(it["bt_b"])) for it in pre]
    xk = [_dot_nt(it["lhs"], spread(it["kt_b"])) for it in pre]
    l_b = [jnp.where(strict, x[0:CHUNK], 0.0) for x in xb]
    l_k = [jnp.where(strict, x[0:CHUNK], 0.0) for x in xk]
    m_b = [jnp.where(incl, x[CHUNK:2 * CHUNK], 0.0) for x in xb]
    m_k = [jnp.where(incl, x[CHUNK:2 * CHUNK], 0.0) for x in xk]
    t_inv = [eye - x for x in l_b]
    power = l_b
    for _ in range(NEUMANN_STEPS):
        power = [_dot(bf(x), spread(bf(x))) for x in power]
        t_inv = [t + _dot(bf(t), spread(bf(x))) for t, x in zip(t_inv, power)]
    v_s = [spread(it["v_b"]) for it in pre]
    lkv = [_dot(bf(x), vs) for x, vs in zip(l_k, v_s)]
    tx = [_dot(bf(t), cat(spread(it["kkt_b"]), spread(bf(x)))) for t, it, x in zip(t_inv, pre, lkv)]
    mx = [_dot(bf(m), cat(spread(bf(x[:, 0:GROUP])), spread(bf(x[:, GROUP:2 * GROUP])))) for m, x in zip(m_b, tx)]
    mv = [_dot(bf(m), vs) for m, vs in zip(m_k, v_s)]
    affine = {}
    for (gr, c), it, x, y, z in zip(items, pre, tx, mx, mv):
        q_eff = it["rt"] - y[:, 0:GROUP]
        o_loc = z - y[:, GROUP:2 * GROUP]
        for half in range(GROUP // PAIR):
            lanes = slice(half * PAIR, (half + 1) * PAIR)
            wu = cat(bf(x[:, lanes]), bf(x[:, GROUP + half * PAIR:GROUP + (half + 1) * PAIR]))
            bx = _dot_tn(it["bh_b"][:, lanes], wu)
            kv = _dot_tn(it["kh_b"][:, lanes], it["v_b"][:, lanes])
            a_eff = jnp.where(on_diag, it["gamma_end"][:, lanes], 0.0) - jnp.where(same_head, bx[:, 0:PAIR], 0.0)
            p_loc = jnp.where(same_head, kv - bx[:, PAIR:2 * PAIR], 0.0)
            affine[(gr * (GROUP // PAIR) + half, c)] = (bf(a_eff), p_loc, bf(q_eff[:, lanes]), o_loc[:, lanes])

    state = [p_ref[pr] for pr in range(npair)]
    for c in range(nchunk):
        for pr in range(npair):
            a_eff, p_loc, q_eff, o_loc = affine[(pr, c)]
            p_b = bf(state[pr])
            o_ref[c * CHUNK:(c + 1) * CHUNK, pr * PAIR:(pr + 1) * PAIR] = (_dot(q_eff, p_b) + o_loc).astype(BF16)
            state[pr] = _dot(a_eff, p_b) + p_loc
    for pr in range(npair):
        p_ref[pr] = state[pr]

    @pl.when(pl.program_id(1) == pl.num_programs(1) - 1)
    def _():
        for pr in range(npair):
            pf_ref[2 * pr] = state[pr][0:HEAD_SIZE, 0:HEAD_SIZE].T
            pf_ref[2 * pr + 1] = state[pr][HEAD_SIZE:PAIR, HEAD_SIZE:PAIR].T


def _chunk_scan(ush, nseq, lw, g512, nchunk):
    m = ush.shape[0]
    tt = nchunk * CHUNK
    steps = m // (nseq * tt)
    row = lambda a: a.reshape(1, -1)
    vec = lambda n: pl.BlockSpec((1, n), lambda s, i: (0, 0))
    blk = lambda w: pl.BlockSpec((tt, w), lambda s, i: (s * steps + i, 0))
    return pl.pallas_call(
        functools.partial(_chunk_kernel, nchunk=nchunk),
        grid=(nseq, steps),
        in_specs=[blk(SHIFT_W),
                  pl.BlockSpec((SUBLANES, SHIFT_W),
                               lambda s, i: (jnp.maximum((s * steps + i) * (tt // SUBLANES) - 1, 0), 0)),
                  vec(SHIFT_W), vec(W_A), vec(W_A),
                  pl.BlockSpec((2 * LORA, 2 * W_A), lambda s, i: (0, 0)),
                  vec(W_A), vec(W_A), vec(W_A),
                  pl.BlockSpec((W_A, W_A), lambda s, i: (0, 0))],
        out_specs=[blk(W_A), blk(W_A),
                   pl.BlockSpec((None, H_A, HEAD_SIZE, HEAD_SIZE), lambda s, i: (s, 0, 0, 0))],
        out_shape=[jax.ShapeDtypeStruct((m, W_A), BF16), jax.ShapeDtypeStruct((m, W_A), BF16),
                   jax.ShapeDtypeStruct((nseq, H_A, HEAD_SIZE, HEAD_SIZE), F32)],
        scratch_shapes=[pltpu.VMEM((W_A // PAIR, PAIR, PAIR), F32)],
        compiler_params=pltpu.CompilerParams(dimension_semantics=("arbitrary", "arbitrary")),
        name="rwkv_chunk_scan",
    )(ush, ush, row(lw["mu"]), row(lw["w0"]), row(lw["a0"]), lw["lora"], row(lw["k_k"]), row(lw["k_a"]),
      row(lw["r_k"]), g512)


def _attn_kernel(cur_ref, prev_ref, zb_ref, gtab_ref, sink_ref, yb_ref, bias_ref, *, blocks):
    s = pl.program_id(1)
    rows = Q_PER_KV * WINDOW

    @pl.when((pl.program_id(0) == 0) & (s == 0))
    def _():
        qi = lax.broadcasted_iota(jnp.int32, (WINDOW, 2 * WINDOW), 0)
        kj = lax.broadcasted_iota(jnp.int32, (WINDOW, 2 * WINDOW), 1)
        dist = qi + WINDOW - kj
        band = (dist >= 0) & (dist <= WINDOW)
        for g in range(KV_HEADS):
            for j in range(Q_PER_KV):
                h = g * Q_PER_KV + j
                row = jnp.broadcast_to(gtab_ref[h:h + 1, :], (WINDOW, BIAS_SPAN))
                toe = pltpu.roll(row, 0, 1, stride=1, stride_axis=0)[:, WINDOW:BIAS_SPAN]
                bias_ref[g, j * WINDOW:(j + 1) * WINDOW, :] = jnp.where(band, toe, NEG_INF)

    cur = cur_ref[...]
    prev = prev_ref[...]
    k_all = jnp.concatenate([prev[:, W_B:W_B + KV_W], cur[:, W_B:W_B + KV_W]], axis=0)
    v_all = jnp.concatenate([prev[:, W_B + KV_W:QKV_W], cur[:, W_B + KV_W:QKV_W]], axis=0)
    lane = lax.broadcasted_iota(jnp.int32, k_all.shape, 1)
    zero = jnp.zeros_like(k_all)
    k_g = [jnp.where(lane < HEAD_DIM, k_all, zero), jnp.where(lane >= HEAD_DIM, k_all, zero)]
    v_g = [jnp.where(lane < HEAD_DIM, v_all, zero), jnp.where(lane >= HEAD_DIM, v_all, zero)]
    ones = jnp.ones((2 * WINDOW, LANES), BF16)
    kj = lax.broadcasted_iota(jnp.int32, (rows, 2 * WINDOW), 1)
    insts = [(i, g) for i in range(blocks) for g in range(KV_HEADS)]
    keys_of = lambda x, i: x[i * WINDOW:(i + 2) * WINDOW]
    q4 = [jnp.concatenate([cur[i * WINDOW:(i + 1) * WINDOW, j * LANES:(j + 1) * LANES] for j in range(Q_PER_KV)],
                          axis=0) for i in range(blocks)]
    logits = []
    for i, g in insts:
        lg = _dot_nt(q4[i], keys_of(k_g[g], i)) + bias_ref[g]
        if i == 0:
            lg = jnp.where((kj >= WINDOW) | (s > 0), lg, NEG_INF)
        logits.append(lg)
    tops = [jnp.maximum(jnp.max(lg, axis=-1, keepdims=True), sink_ref[g]) for (i, g), lg in zip(insts, logits)]
    probs = [jnp.exp(lg - m).astype(BF16) for lg, m in zip(logits, tops)]
    dens = [_dot(p, ones) + jnp.exp(sink_ref[g] - m) for (i, g), p, m in zip(insts, probs, tops)]
    outs = [_dot(p, keys_of(v_g[g], i)) / den for (i, g), p, den in zip(insts, probs, dens)]
    for i in range(blocks):
        acc = outs[KV_HEADS * i] + outs[KV_HEADS * i + 1]
        r0 = i * WINDOW
        for j in range(Q_PER_KV):
            zb = zb_ref[r0:r0 + WINDOW, j * LANES:(j + 1) * LANES]
            yb_ref[r0:r0 + WINDOW, j * LANES:(j + 1) * LANES] = (
                acc[j * WINDOW:(j + 1) * WINDOW] * _silu(zb)).astype(BF16)


def _attn(qkv, zb, nb_seq, gtab, sink_col):
    m = qkv.shape[0]
    nblk = m // (nb_seq * WINDOW)
    blocks = _pick((ATTN_BLOCKS, 2, 1), nblk)
    steps = nblk // blocks
    rows = Q_PER_KV * WINDOW
    return pl.pallas_call(
        functools.partial(_attn_kernel, blocks=blocks),
        grid=(nb_seq, steps),
        in_specs=[pl.BlockSpec((blocks * WINDOW, QKV_W), lambda b, s: (b * steps + s, 0)),
                  pl.BlockSpec((WINDOW, QKV_W), lambda b, s: (jnp.maximum((b * steps + s) * blocks - 1, 0), 0)),
                  pl.BlockSpec((blocks * WINDOW, W_B), lambda b, s: (b * steps + s, 0)),
                  pl.BlockSpec((H_B, BIAS_SPAN), lambda b, s: (0, 0)),
                  pl.BlockSpec((KV_HEADS, rows, 1), lambda b, s: (0, 0, 0))],
        out_specs=pl.BlockSpec((blocks * WINDOW, W_B), lambda b, s: (b * steps + s, 0)),
        out_shape=jax.ShapeDtypeStruct((m, W_B), BF16),
        scratch_shapes=[pltpu.VMEM((KV_HEADS, rows, 2 * WINDOW), F32)],
        compiler_params=pltpu.CompilerParams(dimension_semantics=("arbitrary", "arbitrary")),
        name="swa_prompt",
    )(qkv, qkv, zb, gtab, sink_col)


def _attn_step_kernel(q_ref, kn_ref, vn_ref, zb_ref, ck_ref, cv_ref, bias_ref, bias0_ref, sink_ref, g128_ref,
                      yb_ref, ko_ref, vo_ref):
    q = q_ref[...].astype(F32)
    kn = kn_ref[...]
    vn = vn_ref[...]

    def rows_of(ref):
        x = ref[...]
        return jnp.concatenate([jnp.swapaxes(x[:, g], 1, 2) for g in range(KV_HEADS)], axis=-1)

    def store(ref, y):
        for g in range(KV_HEADS):
            ref[:, g] = jnp.swapaxes(y[:, :, g * HEAD_DIM:(g + 1) * HEAD_DIM], 1, 2)

    ck = rows_of(ck_ref)
    cv = rows_of(cv_ref)
    bt = q.shape[0]
    for r in range(Q_PER_KV):
        qsel = q[:, r * LANES:(r + 1) * LANES]
        prod = (ck * qsel[:, None, :]).reshape(bt * WINDOW, KV_W)
        lg = _split_dot(prod, g128_ref[...]).reshape(bt, WINDOW, KV_W) + bias_ref[r]
        lgn = _split_dot(kn * qsel, g128_ref[...]) + bias0_ref[r]
        s = sink_ref[r]
        m = jnp.maximum(jnp.maximum(jnp.max(lg, axis=1), lgn), s)
        p = jnp.exp(lg - m[:, None, :])
        pn = jnp.exp(lgn - m)
        den = jnp.sum(p, axis=1) + pn + jnp.exp(s - m)
        o = (jnp.sum(p * cv, axis=1) + pn * vn) / den
        zb = zb_ref[:, r * LANES:(r + 1) * LANES]
        yb_ref[:, r * LANES:(r + 1) * LANES] = (o * _silu(zb)).astype(BF16)
    j = lax.broadcasted_iota(jnp.int32, ck.shape, 1)
    store(ko_ref, jnp.where(j == WINDOW - 1, kn[:, None, :], pltpu.roll(ck, WINDOW - 1, axis=1)))
    store(vo_ref, jnp.where(j == WINDOW - 1, vn[:, None, :], pltpu.roll(cv, WINDOW - 1, axis=1)))


def _attn_step(qkv, kv32, zb, cache_k, cache_v, bias_rows, bias0, sink_rows, g128, bt):
    nb = qkv.shape[0]
    cache_spec = pl.BlockSpec((bt, KV_HEADS, HEAD_DIM, WINDOW), lambda i: (i, 0, 0, 0))
    return pl.pallas_call(
        _attn_step_kernel,
        grid=(nb // bt,),
        in_specs=[pl.BlockSpec((bt, W_B), lambda i: (i, 0)),
                  pl.BlockSpec((bt, KV_W), lambda i: (i, 0)),
                  pl.BlockSpec((bt, KV_W), lambda i: (i, 1)),
                  pl.BlockSpec((bt, W_B), lambda i: (i, 0)),
                  cache_spec, cache_spec,
                  pl.BlockSpec((Q_PER_KV, WINDOW, KV_W), lambda i: (0, 0, 0)),
                  pl.BlockSpec((Q_PER_KV, 1, KV_W), lambda i: (0, 0, 0)),
                  pl.BlockSpec((Q_PER_KV, 1, KV_W), lambda i: (0, 0, 0)),
                  pl.BlockSpec((KV_W, KV_W), lambda i: (0, 0))],
        out_specs=[pl.BlockSpec((bt, W_B), lambda i: (i, 0)), cache_spec, cache_spec],
        out_shape=[jax.ShapeDtypeStruct((nb, W_B), BF16),
                   jax.ShapeDtypeStruct(cache_k.shape, F32),
                   jax.ShapeDtypeStruct(cache_v.shape, F32)],
        name="swa_step",
    )(qkv, kv32, kv32, zb, cache_k, cache_v, bias_rows, bias0, sink_rows, g128)


def _merge_kernel(yt_ref, zat_ref, yb_ref, ga_ref, gb_ref, x_ref, gate_ref, woa_ref, wob_ref, wout_ref, out_ref):
    yt = yt_ref[...]
    yat = (yt.reshape(W_A, yt.shape[2]) * _silu(zat_ref[...])).astype(BF16)
    pa = _dot_tn(yat, woa_ref[...])
    pb = _dot(yb_ref[...], wob_ref[...])
    merged = _sigmoid(ga_ref[...]) * pa + _sigmoid(gb_ref[...]) * pb
    out_ref[...] = x_ref[...] + gate_ref[...] * _dot(merged.astype(BF16), wout_ref[...])


def _merge(yt, zat, yb, gates, x, gate, woa_bf, wob_bf, wout_bf, tm, rows_per_seq, per_row):
    m = x.shape[0]
    full = lambda c: pl.BlockSpec((tm, D_MODEL), lambda i: (i, c))
    per_seq = rows_per_seq // tm
    y_spec = pl.BlockSpec((HEAD_SIZE, H_A, tm), lambda i: (0, i // per_seq, i % per_seq))
    return pl.pallas_call(
        _merge_kernel,
        grid=(m // tm,),
        in_specs=[y_spec, _t_spec(W_A, tm, rows_per_seq),
                  pl.BlockSpec((tm, W_B), lambda i: (i, 0)), full(0), full(1), full(0),
                  _mod_spec(per_row, tm, rows_per_seq),
                  pl.BlockSpec((W_A, D_MODEL), lambda i: (0, 0)),
                  pl.BlockSpec((W_B, D_MODEL), lambda i: (0, 0)),
                  pl.BlockSpec((D_MODEL, D_MODEL), lambda i: (0, 0))],
        out_specs=full(0),
        out_shape=jax.ShapeDtypeStruct((m, D_MODEL), F32),
        name="merge_out",
    )(yt, zat, yb, gates, gates, x, gate, woa_bf, wob_bf, wout_bf)


def _merge_rows_kernel(o_ref, bonus_ref, za_ref, yb_ref, ga_ref, gb_ref, x_ref, gate_ref, lng_ref, lnb_ref,
                       g512_ref, woa_ref, wob_ref, wout_ref, out_ref):
    o_b = o_ref[...]
    o = o_b.astype(F32)
    mean = _dot(o_b, g512_ref[...]) * (1.0 / HEAD_SIZE)
    c = o - mean
    var = _head_sum(c * c, g512_ref[...]) * (1.0 / HEAD_SIZE)
    on = c * lax.rsqrt(var + GN_EPS) * lng_ref[...] + lnb_ref[...]
    ya = (on + bonus_ref[...].astype(F32)) * _silu(za_ref[...])
    pa = _dot(ya.astype(BF16), woa_ref[...])
    pb = _dot(yb_ref[...], wob_ref[...])
    merged = _sigmoid(ga_ref[...]) * pa + _sigmoid(gb_ref[...]) * pb
    out_ref[...] = x_ref[...] + gate_ref[...] * _dot(merged.astype(BF16), wout_ref[...])


def _merge_rows(o, bonus, za, yb, gates, x, gate, lnx_g, lnx_b, g512, woa_bf, wob_bf, wout_bf, tm, rows_per_seq):
    m = x.shape[0]
    half = pl.BlockSpec((tm, W_A), lambda i: (i, 0))
    full = lambda c: pl.BlockSpec((tm, D_MODEL), lambda i: (i, c))
    vec = pl.BlockSpec((1, W_A), lambda i: (0, 0))
    return pl.pallas_call(
        _merge_rows_kernel,
        grid=(m // tm,),
        in_specs=[half, half, half, half, full(0), full(1), full(0),
                  _mod_spec(False, tm, rows_per_seq), vec, vec,
                  pl.BlockSpec((W_A, W_A), lambda i: (0, 0)),
                  pl.BlockSpec((W_A, D_MODEL), lambda i: (0, 0)),
                  pl.BlockSpec((W_B, D_MODEL), lambda i: (0, 0)),
                  pl.BlockSpec((D_MODEL, D_MODEL), lambda i: (0, 0))],
        out_specs=full(0),
        out_shape=jax.ShapeDtypeStruct((m, D_MODEL), F32),
        name="merge_out_rows",
    )(o, bonus, za, yb, gates, gates, x, gate, lnx_g.reshape(1, W_A), lnx_b.reshape(1, W_A), g512,
      woa_bf, wob_bf, wout_bf)


def _t5_bucket(dist):
    max_exact = N_BUCKETS // 2
    d = jnp.maximum(dist, 0)
    log_ratio = jnp.log(jnp.maximum(d, 1).astype(F32) / max_exact) / math.log(MAX_DISTANCE / max_exact)
    large = jnp.minimum(max_exact + (log_ratio * (N_BUCKETS - max_exact)).astype(jnp.int32), N_BUCKETS - 1)
    return jnp.where(d < max_exact, d, large)


def _block_ones(n, blk):
    i = np.arange(n) // blk
    return jnp.asarray((i[:, None] == i[None, :]).astype(np.float32), dtype=BF16)


def _rwkv_sample(usht, shift0, state0, lw):
    nb = usht.shape[2]
    outs = _prep(usht, _shift_major(shift0, 1).T[None], lw["mu"], lw["w0"], lw["a0"], lw["lorat"], lw["k_k"],
                 lw["k_a"], lw["r_k"], nb, True)
    tile = lambda a: jnp.broadcast_to(a.reshape(H_A, HEAD_SIZE, 1), (H_A, HEAD_SIZE, nb))
    y, state = _step([a.transpose(1, 0, 2) for a in outs[:6]], outs[6], tile(lw["lnx_g"]), tile(lw["lnx_b"]),
                     state0.transpose(1, 2, 3, 0))
    return y.transpose(1, 0, 2), state.transpose(3, 0, 1, 2)


def kernel(x_prompt, x_sample, c_prompt, c_sample, state_wkv, state_shift, cache_k, cache_v, norm_g, w_ada, b_ada, w_in, mu_shift, w0, w_decay_up, a0, w_a_up, k_k, k_a, r_k, lnx_g, lnx_b, w_o_a, q_norm_g, k_norm_g, rel_bias, sinks, w_o_b, w_out):
    nbp, t_len, _ = x_prompt.shape
    nbs = x_sample.shape[0]
    depth = norm_g.shape[0]
    mp = nbp * t_len
    assert (nbp * H_A) % LANES == 0 and nbs % LANES == 0 and t_len % LANES == 0
    g512 = _block_ones(W_B, HEAD_DIM)
    g128 = _block_ones(KV_W, HEAD_DIM)

    gtab = rel_bias[_t5_bucket(2 * WINDOW - jnp.arange(BIAS_SPAN))].astype(F32).T
    bias_s = rel_bias[_t5_bucket(WINDOW - jnp.arange(WINDOW + 1))].astype(F32)
    pair = lambda a: jnp.concatenate([jnp.repeat(a[..., 0:Q_PER_KV, None], HEAD_DIM, axis=-1),
                                      jnp.repeat(a[..., Q_PER_KV:H_B, None], HEAD_DIM, axis=-1)], axis=-1)
    bias_rows = pair(bias_s[:WINDOW]).transpose(1, 0, 2)
    bias0 = pair(bias_s[WINDOW:]).transpose(1, 0, 2)

    c_all = jnp.concatenate([c_prompt, c_sample], axis=0)
    xp = x_prompt.reshape(mp, D_MODEL)
    xs = x_sample.reshape(nbs, D_MODEL)
    tm_p = _pick((512, 256, 128), t_len)
    tm_in = _pick((512, 256, 128), t_len)
    outs = {k: [] for k in ("wkv_p", "shift_p", "kw_p", "vw_p", "wkv_s", "shift_s", "kw_s", "vw_s")}
    for l in range(depth):
        wl = w_in[l].astype(BF16)
        base = SHIFT_W
        z_a, q, kb, vb, z_b, gts = (wl[:, base:base + 512], wl[:, base + 512:base + 1024],
                                    wl[:, base + 1024:base + 1152], wl[:, base + 1152:base + 1280],
                                    wl[:, base + 1280:base + 1792], wl[:, base + 1792:])
        w_bf = jnp.concatenate([_head_pairs(q, 1), kb, vb, _head_pairs(z_b, 1), gts], axis=1)
        wa_bf = jnp.concatenate([wl[:, :SHIFT_W], z_a], axis=1)
        wt_bf = jnp.concatenate([_shift_major(wl[:, :SHIFT_W], 1), _chan_major(z_a, 1)], axis=1).T
        zeros = jnp.zeros((LORA, W_A), F32)
        lora = jnp.concatenate([jnp.concatenate([w_decay_up[l], zeros], axis=1),
                                jnp.concatenate([zeros, w_a_up[l]], axis=1)], axis=0).astype(BF16)
        lorat = jnp.concatenate([jnp.concatenate([_chan_major(w_decay_up[l], 1), zeros], axis=1),
                                 jnp.concatenate([zeros, _chan_major(w_a_up[l], 1)], axis=1)],
                                axis=0).T.astype(BF16)
        cm = lambda a: _chan_major(a.reshape(-1), 0)
        lw_p = dict(mu=mu_shift[l], w0=w0[l], a0=a0[l], lora=lora, k_k=k_k[l], k_a=k_a[l], r_k=r_k[l].reshape(-1))
        lw = dict(mu=_shift_major(mu_shift[l], 0), w0=cm(w0[l]), a0=cm(a0[l]), lorat=lorat,
                  k_k=cm(k_k[l]), k_a=cm(k_a[l]), r_k=cm(r_k[l]), lnx_g=lnx_g[l], lnx_b=lnx_b[l])
        woa_rows_bf = w_o_a[l].astype(BF16)
        woa_bf, wout_bf = _chan_major(woa_rows_bf, 0), w_out[l].astype(BF16)
        wob_bf = _head_pairs(w_o_b[l].astype(BF16), 0)
        q_gain = jnp.tile(q_norm_g[l], H_B).reshape(1, W_B)
        k_gain = jnp.tile(k_norm_g[l], KV_HEADS).reshape(1, KV_W)
        sink_col = jnp.repeat(sinks[l].reshape(KV_HEADS, Q_PER_KV), WINDOW, axis=1).reshape(
            KV_HEADS, Q_PER_KV * WINDOW, 1)
        sink_rows = pair(sinks[l].reshape(1, H_B)).transpose(1, 0, 2)

        mod = _ada(c_all, w_ada[l].astype(BF16), b_ada[l])
        shift, scale, gate = mod[:, :D_MODEL], mod[:, D_MODEL:2 * D_MODEL], mod[:, 2 * D_MODEL:]

        sp, scp, gp = (a[:nbp].reshape(nbp, 1, D_MODEL) for a in (shift, scale, gate))
        ush, za, qkv, kv32, zb, gts_p = _in_proj(xp, norm_g[l], scp, sp, w_bf, wa_bf, q_gain, k_gain, g512, g128,
                                                 tm_in, t_len, False, False)
        o, bonus, wkv = _chunk_scan(ush, nbp, lw_p, g512, _pick((CHUNKS_PER_STEP, 1), t_len // CHUNK))
        yb = _attn(qkv, zb, nbp, gtab, sink_col)
        xp = _merge_rows(o, bonus, za, yb, gts_p, xp, gp, lnx_g[l], lnx_b[l], g512, woa_rows_bf, wob_bf, wout_bf,
                         tm_p, t_len)
        win = kv32.reshape(nbp, t_len, 2, KV_HEADS, HEAD_DIM)[:, t_len - WINDOW:]
        outs["wkv_p"].append(wkv)
        outs["shift_p"].append(ush.reshape(nbp, t_len, SHIFT_W)[:, t_len - 1])
        outs["kw_p"].append(win[:, :, 0])
        outs["vw_p"].append(win[:, :, 1])

        ss, scs, gs = shift[nbp:], scale[nbp:], gate[nbp:]
        usht, zat, qkv, kv32, zb, gts_s = _in_proj(xs, norm_g[l], scs, ss, w_bf, wt_bf, q_gain, k_gain, g512, g128,
                                                   nbs, nbs, True, True)
        yt, wkv = _rwkv_sample(usht, state_shift[l], state_wkv[l], lw)
        yb, kwin, vwin = _attn_step(qkv, kv32, zb, cache_k[l].transpose(0, 2, 3, 1), cache_v[l].transpose(0, 2, 3, 1),
                                    bias_rows, bias0, sink_rows, g128, _pick((16, 8), nbs))
        xs = _merge(yt, zat, yb, gts_s, xs, gs, woa_bf, wob_bf, wout_bf, nbs, nbs, True)
        outs["wkv_s"].append(wkv)
        outs["shift_s"].append(_shift_major(usht[0].T, 1, inverse=True))
        outs["kw_s"].append(kwin.transpose(0, 3, 1, 2))
        outs["vw_s"].append(vwin.transpose(0, 3, 1, 2))

    st = lambda k: jnp.stack(outs[k])
    return (xp.reshape(nbp, t_len, D_MODEL), xs.reshape(nbs, 1, D_MODEL),
            st("wkv_p"), st("shift_p"), st("kw_p"), st("vw_p"),
            st("wkv_s"), st("shift_s"), st("kw_s"), st("vw_s"))
```

```python
import functools
import math

import numpy as np
import jax
import jax.numpy as jnp
from jax import lax
from jax.experimental import pallas as pl
from jax.experimental.pallas import tpu as pltpu

F32 = jnp.float32
BF16 = jnp.bfloat16

D_MODEL = 1024
W_A = 512
HEAD_SIZE = 64
H_A = W_A // HEAD_SIZE
LORA = 64
SHIFT_W = 3 * W_A + 2 * LORA
W_B = 512
HEAD_DIM = 64
H_B = W_B // HEAD_DIM
KV_HEADS = 2
Q_PER_KV = H_B // KV_HEADS
KV_W = KV_HEADS * HEAD_DIM
WINDOW = 128
N_BUCKETS = 32
MAX_DISTANCE = 128
NORM_EPS = 1e-6
GN_EPS = 64e-5
NEG_INF = -1e30
QKV_W = W_B + 2 * KV_W
GATE_W = 2 * D_MODEL
ROW_COLS = QKV_W + W_B + GATE_W
T_COLS = SHIFT_W + W_A
LANES = 128
SUBLANES = 8
def _head_pairs(x, axis):
    s = x.shape
    x = x.reshape(s[:axis] + (KV_HEADS, Q_PER_KV, HEAD_DIM) + s[axis + 1:])
    return jnp.swapaxes(x, axis, axis + 1).reshape(s)


def _chan_major(x, axis, inverse=False):
    s = x.shape
    split = (HEAD_SIZE, H_A) if inverse else (H_A, HEAD_SIZE)
    x = x.reshape(s[:axis] + split + s[axis + 1:])
    return jnp.swapaxes(x, axis, axis + 1).reshape(s)


def _shift_major(x, axis, inverse=False):
    parts = [lax.slice_in_dim(x, i * W_A, (i + 1) * W_A, axis=axis) for i in range(3)]
    tail = lax.slice_in_dim(x, 3 * W_A, SHIFT_W, axis=axis)
    return jnp.concatenate([_chan_major(p, axis, inverse) for p in parts] + [tail], axis=axis)
ATTN_BLOCKS = 4
BIAS_SPAN = 3 * WINDOW
CHUNK = 64
CHUNKS_PER_STEP = 8
PAIR = 2 * HEAD_SIZE
GROUP = PAIR
NEUMANN_STEPS = CHUNK.bit_length() - 2


def _dot(a, b):
    return jnp.dot(a, b, preferred_element_type=F32)


def _dot_nt(a, b):
    return lax.dot_general(a, b, (((1,), (1,)), ((), ())), preferred_element_type=F32)


def _dot_tn(a, b):
    return lax.dot_general(a, b, (((0,), (0,)), ((), ())), preferred_element_type=F32)


def _head_sum(x, g):
    return _dot(x.astype(BF16), g)


def _split_dot(x, g):
    hi = x.astype(BF16)
    lo = (x - hi.astype(F32)).astype(BF16)
    return _dot(hi, g) + _dot(lo, g)


def _silu(x):
    x = x.astype(F32)
    return x * jax.nn.sigmoid(x)


def _sigmoid(x):
    return jax.nn.sigmoid(x.astype(F32))


def _pick(cands, n):
    for c in cands:
        if n % c == 0:
            return c
    return n


def _ada_kernel(c_ref, w_ref, b_ref, o_ref):
    c = c_ref[...]
    o_ref[...] = _dot(_silu(c).astype(BF16), w_ref[...]) + b_ref[...]


def _ada(c, w_bf, b):
    nb, d = c.shape
    n = w_bf.shape[1]
    tn = 1024
    return pl.pallas_call(
        _ada_kernel,
        grid=(n // tn,),
        in_specs=[pl.BlockSpec((nb, d), lambda j: (0, 0)),
                  pl.BlockSpec((d, tn), lambda j: (0, j)),
                  pl.BlockSpec((1, tn), lambda j: (0, j))],
        out_specs=pl.BlockSpec((nb, tn), lambda j: (0, j)),
        out_shape=jax.ShapeDtypeStruct((nb, n), F32),
        name="ada",
    )(c, w_bf, b.reshape(1, n))


def _in_proj_kernel(x_ref, g_ref, scale_ref, shift_ref, w_ref, wa_ref, qg_ref, kg_ref, g512_ref, g128_ref,
                    ush_ref, za_ref, qkv_ref, kv32_ref, zb_ref, gate_ref, *, transposed):
    x = x_ref[...]
    ms = jnp.mean(x * x, axis=-1, keepdims=True)
    h = (x * lax.rsqrt(ms + NORM_EPS)) * g_ref[...]
    h = h * (1.0 + scale_ref[...]) + shift_ref[...]
    hb = h.astype(BF16)
    if transposed:
        ush_ref[...] = _dot_nt(wa_ref[0:SHIFT_W, :], hb)
        za_ref[...] = _dot_nt(wa_ref[SHIFT_W:T_COLS, :], hb).astype(BF16)
    else:
        ush_ref[...] = _dot(hb, wa_ref[:, 0:SHIFT_W])
        za_ref[...] = _dot(hb, wa_ref[:, SHIFT_W:T_COLS]).astype(BF16)
    qkv = _dot(hb, w_ref[:, 0:QKV_W])
    q, k, v = qkv[:, 0:W_B], qkv[:, W_B:W_B + KV_W], qkv[:, W_B + KV_W:QKV_W]
    qn = q * lax.rsqrt(_head_sum(q * q, g512_ref[...]) * (1.0 / HEAD_DIM) + NORM_EPS) * qg_ref[...]
    kn = k * lax.rsqrt(_head_sum(k * k, g128_ref[...]) * (1.0 / HEAD_DIM) + NORM_EPS) * kg_ref[...]
    qkv_ref[:, 0:W_B] = (qn * (HEAD_DIM ** -0.5)).astype(BF16)
    qkv_ref[:, W_B:W_B + KV_W] = kn.astype(BF16)
    qkv_ref[:, W_B + KV_W:QKV_W] = v.astype(BF16)
    kv32_ref[:, 0:KV_W] = kn
    kv32_ref[:, KV_W:2 * KV_W] = v
    zb_ref[...] = _dot(hb, w_ref[:, QKV_W:QKV_W + W_B]).astype(BF16)
    gate_ref[...] = _dot(hb, w_ref[:, QKV_W + W_B:ROW_COLS]).astype(BF16)


def _mod_spec(per_row, tm, rows_per_seq):
    if per_row:
        return pl.BlockSpec((tm, D_MODEL), lambda i: (i, 0))
    return pl.BlockSpec((None, 1, D_MODEL), lambda i: ((i * tm) // rows_per_seq, 0, 0))


def _t_spec(cols, tm, rows_per_seq):
    per_seq = rows_per_seq // tm
    return pl.BlockSpec((None, cols, tm), lambda i: (i // per_seq, 0, i % per_seq))


def _in_proj(x, norm_g, scale, shift, w_bf, wa_bf, q_gain, k_gain, g512, g128, tm, rows_per_seq, per_row,
             transposed):
    m = x.shape[0]
    nseq = m // rows_per_seq
    const = lambda r, c: pl.BlockSpec((r, c), lambda i: (0, 0))
    rows = lambda w: pl.BlockSpec((tm, w), lambda i: (i, 0))
    widths = (QKV_W, 2 * KV_W, W_B, GATE_W)
    dtypes = (BF16, F32, BF16, BF16)
    if transposed:
        a_specs = [_t_spec(SHIFT_W, tm, rows_per_seq), _t_spec(W_A, tm, rows_per_seq)]
        a_shapes = [jax.ShapeDtypeStruct((nseq, SHIFT_W, rows_per_seq), F32),
                    jax.ShapeDtypeStruct((nseq, W_A, rows_per_seq), BF16)]
    else:
        a_specs = [rows(SHIFT_W), rows(W_A)]
        a_shapes = [jax.ShapeDtypeStruct((m, SHIFT_W), F32), jax.ShapeDtypeStruct((m, W_A), BF16)]
    return pl.pallas_call(
        functools.partial(_in_proj_kernel, transposed=transposed),
        grid=(m // tm,),
        in_specs=[rows(D_MODEL),
                  const(1, D_MODEL),
                  _mod_spec(per_row, tm, rows_per_seq),
                  _mod_spec(per_row, tm, rows_per_seq),
                  const(D_MODEL, ROW_COLS), const(*wa_bf.shape),
                  const(1, W_B), const(1, KV_W), const(W_B, W_B), const(KV_W, KV_W)],
        out_specs=a_specs + [rows(w) for w in widths],
        out_shape=a_shapes + [jax.ShapeDtypeStruct((m, w), dt) for w, dt in zip(widths, dtypes)],
        name="in_proj",
    )(x, norm_g.reshape(1, D_MODEL), scale, shift, w_bf, wa_bf, q_gain, k_gain, g512, g128)


def _prep_kernel(u_ref, prev_ref, shift0_ref, mu_ref, w0_ref, a0_ref, lora_ref, kk_ref, ka_ref, rk_ref,
                 r_o, w_o, k_o, v_o, kk_o, b_o, coef_o, *, per_token_state):
    tm = u_ref.shape[1]
    heads = lambda x: x.reshape(HEAD_SIZE, H_A, LANES)
    lane = lax.broadcasted_iota(jnp.int32, (SHIFT_W, LANES), 1)
    row = lax.broadcasted_iota(jnp.int32, (2 * LORA, LANES), 0)
    if not per_token_state:
        before = jnp.where(pl.program_id(1) == 0, shift0_ref[...], prev_ref[...])
        rolled_before = pltpu.roll(before, 1, axis=1)
    for ci in range(tm // LANES):
        cols = slice(ci * LANES, (ci + 1) * LANES)
        u = u_ref[:, cols]
        if per_token_state:
            u_prev = shift0_ref[:, cols]
        else:
            rolled = pltpu.roll(u, 1, axis=1)
            u_prev = jnp.where(lane == 0, rolled_before, rolled)
            rolled_before = rolled
        xs = u + (u_prev - u) * mu_ref[...]
        r = xs[0:W_A]
        k = xs[W_A:2 * W_A]
        v = xs[2 * W_A:3 * W_A]
        tail = xs[3 * W_A:SHIFT_W]
        lora_in = jnp.where(row < LORA, jnp.tanh(tail), tail).astype(BF16)
        up = _dot(lora_ref[...], lora_in)
        neg = -(w0_ref[...] + up[0:W_A])
        softplus = jnp.maximum(neg, 0.0) + jnp.log(1.0 + jnp.exp(-jnp.abs(neg)))
        w_log = -softplus - 0.5
        decay = jnp.exp(-jnp.exp(w_log))
        a = jax.nn.sigmoid(a0_ref[...] + up[W_A:2 * W_A])
        kk = heads(k * kk_ref[...])
        norm = jnp.sqrt(jnp.sum(kk * kk, axis=0, keepdims=True))
        kk = kk / jnp.maximum(norm, 1e-12)
        k_mod = k * (1.0 + (a - 1.0) * ka_ref[...])
        r_o[:, :, cols] = heads(r)
        w_o[:, :, cols] = heads(decay)
        k_o[:, :, cols] = heads(k_mod)
        v_o[:, :, cols] = heads(v)
        kk_o[:, :, cols] = kk
        b_o[:, :, cols] = kk * heads(a)
        coef_o[:, cols] = jnp.sum(heads(r * k_mod * rk_ref[...]), axis=0)


def _prep(usht, shift0t, mu, w0, a0, lorat_bf, k_k, k_a, r_k, tm, per_token_state):
    nseq, _, t_len = usht.shape
    col = lambda a: jnp.broadcast_to(a.reshape(-1, 1), (a.size, LANES))
    cvec = lambda n: pl.BlockSpec((n, LANES), lambda s, j: (0, 0))
    blk = lambda cols: pl.BlockSpec((None, cols, tm), lambda s, j: (s, 0, j))
    if per_token_state:
        prev_spec = pl.BlockSpec((None, SHIFT_W, LANES), lambda s, j: (0, 0, 0))
        shift0_spec = blk(SHIFT_W)
    else:
        prev_spec = pl.BlockSpec((None, SHIFT_W, LANES),
                                 lambda s, j: (s, 0, jnp.maximum(j * (tm // LANES) - 1, 0)))
        shift0_spec = pl.BlockSpec((None, SHIFT_W, LANES), lambda s, j: (s, 0, 0))
    chan = pl.BlockSpec((HEAD_SIZE, H_A, tm), lambda s, j: (0, s, j))
    return pl.pallas_call(
        functools.partial(_prep_kernel, per_token_state=per_token_state),
        grid=(nseq, t_len // tm),
        in_specs=[blk(SHIFT_W), prev_spec, shift0_spec, cvec(SHIFT_W), cvec(W_A), cvec(W_A),
                  pl.BlockSpec((2 * W_A, 2 * LORA), lambda s, j: (0, 0)), cvec(W_A), cvec(W_A), cvec(W_A)],
        out_specs=[chan] * 6 + [pl.BlockSpec((H_A, tm), lambda s, j: (s, j))],
        out_shape=[jax.ShapeDtypeStruct((HEAD_SIZE, nseq * H_A, t_len), F32)] * 6
                  + [jax.ShapeDtypeStruct((nseq * H_A, t_len), F32)],
        name="rwkv_prep",
    )(usht, usht, shift0t, col(mu), col(w0), col(a0), lorat_bf, col(k_k), col(k_a), col(r_k))


def _step_kernel(r_ref, w_ref, k_ref, v_ref, kk_ref, b_ref, coef_ref, lng_ref, lnb_ref, s_ref,
                 y_ref, so_ref, o_ref):
    kk, w, b, km, r = kk_ref[...], w_ref[...], b_ref[...], k_ref[...], r_ref[...]
    for vi in range(HEAD_SIZE):
        s = s_ref[vi]
        sa = -jnp.sum(s * kk, axis=0, keepdims=True)
        s = s * w + sa * b + v_ref[pl.ds(vi, 1), :] * km
        so_ref[vi] = s
        o_ref[pl.ds(vi, 1), :] = jnp.sum(s * r, axis=0, keepdims=True)
    o = o_ref[...]
    mean = jnp.sum(o, axis=0, keepdims=True) * (1.0 / HEAD_SIZE)
    c = o - mean
    var = jnp.sum(c * c, axis=0, keepdims=True) * (1.0 / HEAD_SIZE)
    coef = coef_ref[pl.ds(pl.program_id(0), 1), :]
    y_ref[...] = c * lax.rsqrt(var + GN_EPS) * lng_ref[...] + lnb_ref[...] + coef * v_ref[...]


def _step(seqs, coef, lng, lnb, states, layer):
    nb = coef.shape[1]
    seq_spec = pl.BlockSpec((None, HEAD_SIZE, nb), lambda h: (h, 0, 0))
    state_in = pl.BlockSpec((None, None, HEAD_SIZE, HEAD_SIZE, nb), lambda h: (layer, h, 0, 0, 0))
    state_spec = pl.BlockSpec((None, HEAD_SIZE, HEAD_SIZE, nb), lambda h: (h, 0, 0, 0))
    return pl.pallas_call(
        _step_kernel,
        grid=(H_A,),
        in_specs=[seq_spec] * 6 + [pl.BlockSpec((H_A, nb), lambda h: (0, 0)), seq_spec, seq_spec, state_in],
        out_specs=[seq_spec, state_spec],
        out_shape=[jax.ShapeDtypeStruct((H_A, HEAD_SIZE, nb), F32),
                   jax.ShapeDtypeStruct((H_A, HEAD_SIZE, HEAD_SIZE, nb), F32)],
        scratch_shapes=[pltpu.VMEM((HEAD_SIZE, nb), F32)],
        name="rwkv_step",
    )(*seqs, coef, lng, lnb, states)


def _prep_rows(u_ref, prev_ref, first, mu_ref, w0_ref, a0_ref, lora_ref, kk_ref, ka_ref, rk_ref, g_ref):
    u = u_ref[...]
    prev_row = jnp.where(first, 0.0, prev_ref[SUBLANES - 1:SUBLANES, :])
    row = lax.broadcasted_iota(jnp.int32, u.shape, 0)
    u_prev = jnp.where(row == 0, prev_row, pltpu.roll(u, 1, axis=0))
    xs = u + (u_prev - u) * mu_ref[...]
    r = xs[:, 0:W_A]
    k = xs[:, W_A:2 * W_A]
    v = xs[:, 2 * W_A:3 * W_A]
    tail = xs[:, 3 * W_A:SHIFT_W]
    lane = lax.broadcasted_iota(jnp.int32, tail.shape, 1)
    lora_in = jnp.where(lane < LORA, jnp.tanh(tail), tail).astype(BF16)
    up = _dot(lora_in, lora_ref[...])
    neg = -(w0_ref[...] + up[:, 0:W_A])
    softplus = jnp.maximum(neg, 0.0) + jnp.log(1.0 + jnp.exp(-jnp.abs(neg)))
    w_log = -softplus - 0.5
    a = jax.nn.sigmoid(a0_ref[...] + up[:, W_A:2 * W_A])
    kk = k * kk_ref[...]
    norm = jnp.sqrt(_head_sum(kk * kk, g_ref[...]))
    kk = kk / jnp.maximum(norm, 1e-12)
    k_mod = k * (1.0 + (a - 1.0) * ka_ref[...])
    lw = -jnp.exp(w_log)
    bonus = _head_sum(r * k_mod * rk_ref[...], g_ref[...]) * v
    return r, lw, k_mod, v, kk, kk * a, bonus


def _chunk_kernel(u_ref, prev_ref, mu_ref, w0_ref, a0_ref, lora_ref, kkp_ref, ka_ref, rk_ref, g_ref,
                  o_ref, bonus_ref, pf_ref, p_ref, *, nchunk):
    npair = W_A // PAIR
    first = pl.program_id(1) == 0

    @pl.when(first)
    def _():
        p_ref[...] = jnp.zeros(p_ref.shape, F32)

    seqs = _prep_rows(u_ref, prev_ref, first, mu_ref, w0_ref, a0_ref, lora_ref, kkp_ref, ka_ref, rk_ref, g_ref)
    bonus_ref[...] = seqs[6].astype(BF16)

    tq = lax.broadcasted_iota(jnp.int32, (CHUNK, GROUP), 0)
    lq = lax.broadcasted_iota(jnp.int32, (CHUNK, GROUP), 1)
    head_of = lq // HEAD_SIZE
    col = lq % HEAD_SIZE
    strict = tq > col
    incl = tq >= col
    eye = jnp.where(tq == col, 1.0, 0.0).astype(F32)
    ti = lax.broadcasted_iota(jnp.int32, (CHUNK, CHUNK), 0)
    tj = lax.broadcasted_iota(jnp.int32, (CHUNK, CHUNK), 1)
    tri = jnp.where(ti >= tj, 1.0, 0.0).astype(BF16)
    pi = lax.broadcasted_iota(jnp.int32, (PAIR, PAIR), 0)
    pj = lax.broadcasted_iota(jnp.int32, (PAIR, PAIR), 1)
    same_head = (pi < HEAD_SIZE) == (pj < HEAD_SIZE)
    on_diag = pi == pj
    bf = lambda x: x.astype(BF16)
    cat = lambda a, b: jnp.concatenate([a, b], axis=1)

    def spread(x):
        return jnp.concatenate([jnp.where(head_of == j, x, jnp.zeros_like(x)) for j in range(GROUP // HEAD_SIZE)],
                               axis=0)

    items = [(gr, c) for gr in range(W_A // GROUP) for c in range(nchunk)]
    pre = []
    for gr, c in items:
        r, lw, k, v, kk, b = (x[c * CHUNK:(c + 1) * CHUNK, gr * GROUP:(gr + 1) * GROUP] for x in seqs[:6])
        g = _dot(tri, bf(lw))
        g = g + _dot(tri, bf(lw - bf(lw).astype(F32)))
        g_end = g[CHUNK - 1:CHUNK, :]
        e_neg = jnp.exp(-g)
        e_end = jnp.exp(g_end - g)
        kkt = kk * jnp.exp(g - lw)
        rt = r * jnp.exp(g)
        pre.append(dict(rt=rt, lhs=bf(jnp.concatenate([kkt, rt], axis=0)), kkt_b=bf(kkt), bt_b=bf(b * e_neg),
                        kt_b=bf(k * e_neg), v_b=bf(v), bh_b=bf(b * e_end), kh_b=bf(k * e_end),
                        gamma_end=jnp.exp(g_end)))
    xb = [_dot_nt(it["lhs"], spread(it["bt_b"])) for it in pre]
    xk = [_dot_nt(it["lhs"], spread(it["kt_b"])) for it in pre]
    l_b = [jnp.where(strict, x[0:CHUNK], 0.0) for x in xb]
    l_k = [jnp.where(strict, x[0:CHUNK], 0.0) for x in xk]
    m_b = [jnp.where(incl, x[CHUNK:2 * CHUNK], 0.0) for x in xb]
    m_k = [jnp.where(incl, x[CHUNK:2 * CHUNK], 0.0) for x in xk]
    t_inv = [eye - x for x in l_b]
    power = l_b
    for _ in range(NEUMANN_STEPS):
        power = [_dot(bf(x), spread(bf(x))) for x in power]
        t_inv = [t + _dot(bf(t), spread(bf(x))) for t, x in zip(t_inv, power)]
    v_s = [spread(it["v_b"]) for it in pre]
    lkv = [_dot(bf(x), vs) for x, vs in zip(l_k, v_s)]
    tx = [_dot(bf(t), cat(spread(it["kkt_b"]), spread(bf(x)))) for t, it, x in zip(t_inv, pre, lkv)]
    mx = [_dot(bf(m), cat(spread(bf(x[:, 0:GROUP])), spread(bf(x[:, GROUP:2 * GROUP])))) for m, x in zip(m_b, tx)]
    mv = [_dot(bf(m), vs) for m, vs in zip(m_k, v_s)]
    affine = {}
    for (gr, c), it, x, y, z in zip(items, pre, tx, mx, mv):
        q_eff = it["rt"] - y[:, 0:GROUP]
        o_loc = z - y[:, GROUP:2 * GROUP]
        for half in range(GROUP // PAIR):
            lanes = slice(half * PAIR, (half + 1) * PAIR)
            wu = cat(bf(x[:, lanes]), bf(x[:, GROUP + half * PAIR:GROUP + (half + 1) * PAIR]))
            bx = _dot_tn(it["bh_b"][:, lanes], wu)
            kv = _dot_tn(it["kh_b"][:, lanes], it["v_b"][:, lanes])
            a_eff = jnp.where(on_diag, it["gamma_end"][:, lanes], 0.0) - jnp.where(same_head, bx[:, 0:PAIR], 0.0)
            p_loc = jnp.where(same_head, kv - bx[:, PAIR:2 * PAIR], 0.0)
            affine[(gr * (GROUP // PAIR) + half, c)] = (bf(a_eff), p_loc, bf(q_eff[:, lanes]), o_loc[:, lanes])

    state = [p_ref[pr] for pr in range(npair)]
    for c in range(nchunk):
        for pr in range(npair):
            a_eff, p_loc, q_eff, o_loc = affine[(pr, c)]
            p_b = bf(state[pr])
            o_ref[c * CHUNK:(c + 1) * CHUNK, pr * PAIR:(pr + 1) * PAIR] = (_dot(q_eff, p_b) + o_loc).astype(BF16)
            state[pr] = _dot(a_eff, p_b) + p_loc
    for pr in range(npair):
        p_ref[pr] = state[pr]

    @pl.when(pl.program_id(1) == pl.num_programs(1) - 1)
    def _():
        for pr in range(npair):
            pf_ref[2 * pr] = state[pr][0:HEAD_SIZE, 0:HEAD_SIZE].T
            pf_ref[2 * pr + 1] = state[pr][HEAD_SIZE:PAIR, HEAD_SIZE:PAIR].T


def _chunk_scan(ush, nseq, lw, g512, nchunk):
    m = ush.shape[0]
    tt = nchunk * CHUNK
    steps = m // (nseq * tt)
    row = lambda a: a.reshape(1, -1)
    vec = lambda n: pl.BlockSpec((1, n), lambda s, i: (0, 0))
    blk = lambda w: pl.BlockSpec((tt, w), lambda s, i: (s * steps + i, 0))
    return pl.pallas_call(
        functools.partial(_chunk_kernel, nchunk=nchunk),
        grid=(nseq, steps),
        in_specs=[blk(SHIFT_W),
                  pl.BlockSpec((SUBLANES, SHIFT_W),
                               lambda s, i: (jnp.maximum((s * steps + i) * (tt // SUBLANES) - 1, 0), 0)),
                  vec(SHIFT_W), vec(W_A), vec(W_A),
                  pl.BlockSpec((2 * LORA, 2 * W_A), lambda s, i: (0, 0)),
                  vec(W_A), vec(W_A), vec(W_A),
                  pl.BlockSpec((W_A, W_A), lambda s, i: (0, 0))],
        out_specs=[blk(W_A), blk(W_A),
                   pl.BlockSpec((None, H_A, HEAD_SIZE, HEAD_SIZE), lambda s, i: (s, 0, 0, 0))],
        out_shape=[jax.ShapeDtypeStruct((m, W_A), BF16), jax.ShapeDtypeStruct((m, W_A), BF16),
                   jax.ShapeDtypeStruct((nseq, H_A, HEAD_SIZE, HEAD_SIZE), F32)],
        scratch_shapes=[pltpu.VMEM((W_A // PAIR, PAIR, PAIR), F32)],
        compiler_params=pltpu.CompilerParams(dimension_semantics=("arbitrary", "arbitrary")),
        name="rwkv_chunk_scan",
    )(ush, ush, row(lw["mu"]), row(lw["w0"]), row(lw["a0"]), lw["lora"], row(lw["k_k"]), row(lw["k_a"]),
      row(lw["r_k"]), g512)


def _attn_kernel(cur_ref, prev_ref, zb_ref, gtab_ref, sink_ref, yb_ref, bias_ref, *, blocks):
    s = pl.program_id(1)
    rows = Q_PER_KV * WINDOW

    @pl.when((pl.program_id(0) == 0) & (s == 0))
    def _():
        qi = lax.broadcasted_iota(jnp.int32, (WINDOW, 2 * WINDOW), 0)
        kj = lax.broadcasted_iota(jnp.int32, (WINDOW, 2 * WINDOW), 1)
        dist = qi + WINDOW - kj
        band = (dist >= 0) & (dist <= WINDOW)
        for g in range(KV_HEADS):
            for j in range(Q_PER_KV):
                h = g * Q_PER_KV + j
                row = jnp.broadcast_to(gtab_ref[h:h + 1, :], (WINDOW, BIAS_SPAN))
                toe = pltpu.roll(row, 0, 1, stride=1, stride_axis=0)[:, WINDOW:BIAS_SPAN]
                bias_ref[g, j * WINDOW:(j + 1) * WINDOW, :] = jnp.where(band, toe, NEG_INF)

    cur = cur_ref[...]
    prev = prev_ref[...]
    k_all = jnp.concatenate([prev[:, W_B:W_B + KV_W], cur[:, W_B:W_B + KV_W]], axis=0)
    v_all = jnp.concatenate([prev[:, W_B + KV_W:QKV_W], cur[:, W_B + KV_W:QKV_W]], axis=0)
    lane = lax.broadcasted_iota(jnp.int32, k_all.shape, 1)
    zero = jnp.zeros_like(k_all)
    k_g = [jnp.where(lane < HEAD_DIM, k_all, zero), jnp.where(lane >= HEAD_DIM, k_all, zero)]
    v_g = [jnp.where(lane < HEAD_DIM, v_all, zero), jnp.where(lane >= HEAD_DIM, v_all, zero)]
    ones = jnp.ones((2 * WINDOW, LANES), BF16)
    kj = lax.broadcasted_iota(jnp.int32, (rows, 2 * WINDOW), 1)
    insts = [(i, g) for i in range(blocks) for g in range(KV_HEADS)]
    keys_of = lambda x, i: x[i * WINDOW:(i + 2) * WINDOW]
    q4 = [jnp.concatenate([cur[i * WINDOW:(i + 1) * WINDOW, j * LANES:(j + 1) * LANES] for j in range(Q_PER_KV)],
                          axis=0) for i in range(blocks)]
    logits = []
    for i, g in insts:
        lg = _dot_nt(q4[i], keys_of(k_g[g], i)) + bias_ref[g]
        if i == 0:
            lg = jnp.where((kj >= WINDOW) | (s > 0), lg, NEG_INF)
        logits.append(lg)
    tops = [jnp.maximum(jnp.max(lg, axis=-1, keepdims=True), sink_ref[g]) for (i, g), lg in zip(insts, logits)]
    probs = [jnp.exp(lg - m).astype(BF16) for lg, m in zip(logits, tops)]
    dens = [_dot(p, ones) + jnp.exp(sink_ref[g] - m) for (i, g), p, m in zip(insts, probs, tops)]
    outs = [_dot(p, keys_of(v_g[g], i)) / den for (i, g), p, den in zip(insts, probs, dens)]
    for i in range(blocks):
        acc = outs[KV_HEADS * i] + outs[KV_HEADS * i + 1]
        r0 = i * WINDOW
        for j in range(Q_PER_KV):
            zb = zb_ref[r0:r0 + WINDOW, j * LANES:(j + 1) * LANES]
            yb_ref[r0:r0 + WINDOW, j * LANES:(j + 1) * LANES] = (
                acc[j * WINDOW:(j + 1) * WINDOW] * _silu(zb)).astype(BF16)


def _attn(qkv, zb, nb_seq, gtab, sink_col):
    m = qkv.shape[0]
    nblk = m // (nb_seq * WINDOW)
    blocks = _pick((ATTN_BLOCKS, 2, 1), nblk)
    steps = nblk // blocks
    rows = Q_PER_KV * WINDOW
    return pl.pallas_call(
        functools.partial(_attn_kernel, blocks=blocks),
        grid=(nb_seq, steps),
        in_specs=[pl.BlockSpec((blocks * WINDOW, QKV_W), lambda b, s: (b * steps + s, 0)),
                  pl.BlockSpec((WINDOW, QKV_W), lambda b, s: (jnp.maximum((b * steps + s) * blocks - 1, 0), 0)),
                  pl.BlockSpec((blocks * WINDOW, W_B), lambda b, s: (b * steps + s, 0)),
                  pl.BlockSpec((H_B, BIAS_SPAN), lambda b, s: (0, 0)),
                  pl.BlockSpec((KV_HEADS, rows, 1), lambda b, s: (0, 0, 0))],
        out_specs=pl.BlockSpec((blocks * WINDOW, W_B), lambda b, s: (b * steps + s, 0)),
        out_shape=jax.ShapeDtypeStruct((m, W_B), BF16),
        scratch_shapes=[pltpu.VMEM((KV_HEADS, rows, 2 * WINDOW), F32)],
        compiler_params=pltpu.CompilerParams(dimension_semantics=("arbitrary", "arbitrary")),
        name="swa_prompt",
    )(qkv, qkv, zb, gtab, sink_col)


def _attn_step_kernel(q_ref, kn_ref, vn_ref, zb_ref, ck_ref, cv_ref, bias_ref, bias0_ref, sink_ref, g128_ref,
                      yb_ref, ko_ref, vo_ref):
    q = q_ref[...].astype(F32)
    kn = kn_ref[...]
    vn = vn_ref[...]

    def rows_of(ref):
        x = ref[...]
        return jnp.concatenate([jnp.swapaxes(x[:, g], 1, 2) for g in range(KV_HEADS)], axis=-1)

    def store(ref, y):
        for g in range(KV_HEADS):
            ref[:, g] = jnp.swapaxes(y[:, :, g * HEAD_DIM:(g + 1) * HEAD_DIM], 1, 2)

    ck = rows_of(ck_ref)
    cv = rows_of(cv_ref)
    bt = q.shape[0]
    for r in range(Q_PER_KV):
        qsel = q[:, r * LANES:(r + 1) * LANES]
        prod = (ck * qsel[:, None, :]).reshape(bt * WINDOW, KV_W)
        lg = _split_dot(prod, g128_ref[...]).reshape(bt, WINDOW, KV_W) + bias_ref[r]
        lgn = _split_dot(kn * qsel, g128_ref[...]) + bias0_ref[r]
        s = sink_ref[r]
        m = jnp.maximum(jnp.maximum(jnp.max(lg, axis=1), lgn), s)
        p = jnp.exp(lg - m[:, None, :])
        pn = jnp.exp(lgn - m)
        den = jnp.sum(p, axis=1) + pn + jnp.exp(s - m)
        o = (jnp.sum(p * cv, axis=1) + pn * vn) / den
        zb = zb_ref[:, r * LANES:(r + 1) * LANES]
        yb_ref[:, r * LANES:(r + 1) * LANES] = (o * _silu(zb)).astype(BF16)
    j = lax.broadcasted_iota(jnp.int32, ck.shape, 1)
    store(ko_ref, jnp.where(j == WINDOW - 1, kn[:, None, :], pltpu.roll(ck, WINDOW - 1, axis=1)))
    store(vo_ref, jnp.where(j == WINDOW - 1, vn[:, None, :], pltpu.roll(cv, WINDOW - 1, axis=1)))


def _attn_step(qkv, kv32, zb, cache_k, cache_v, layer, bias_rows, bias0, sink_rows, g128, bt):
    nb = qkv.shape[0]
    cache_in = pl.BlockSpec((None, bt, KV_HEADS, HEAD_DIM, WINDOW), lambda i: (layer, i, 0, 0, 0))
    cache_spec = pl.BlockSpec((bt, KV_HEADS, HEAD_DIM, WINDOW), lambda i: (i, 0, 0, 0))
    return pl.pallas_call(
        _attn_step_kernel,
        grid=(nb // bt,),
        in_specs=[pl.BlockSpec((bt, W_B), lambda i: (i, 0)),
                  pl.BlockSpec((bt, KV_W), lambda i: (i, 0)),
                  pl.BlockSpec((bt, KV_W), lambda i: (i, 1)),
                  pl.BlockSpec((bt, W_B), lambda i: (i, 0)),
                  cache_in, cache_in,
                  pl.BlockSpec((Q_PER_KV, WINDOW, KV_W), lambda i: (0, 0, 0)),
                  pl.BlockSpec((Q_PER_KV, 1, KV_W), lambda i: (0, 0, 0)),
                  pl.BlockSpec((Q_PER_KV, 1, KV_W), lambda i: (0, 0, 0)),
                  pl.BlockSpec((KV_W, KV_W), lambda i: (0, 0))],
        out_specs=[pl.BlockSpec((bt, W_B), lambda i: (i, 0)), cache_spec, cache_spec],
        out_shape=[jax.ShapeDtypeStruct((nb, W_B), BF16),
                   jax.ShapeDtypeStruct(cache_k.shape[1:], F32),
                   jax.ShapeDtypeStruct(cache_v.shape[1:], F32)],
        name="swa_step",
    )(qkv, kv32, kv32, zb, cache_k, cache_v, bias_rows, bias0, sink_rows, g128)


def _merge_kernel(yt_ref, zat_ref, yb_ref, ga_ref, gb_ref, x_ref, gate_ref, woa_ref, wob_ref, wout_ref, out_ref):
    yt = yt_ref[...]
    yat = (yt.reshape(W_A, yt.shape[2]) * _silu(zat_ref[...])).astype(BF16)
    pa = _dot_tn(yat, woa_ref[...])
    pb = _dot(yb_ref[...], wob_ref[...])
    merged = _sigmoid(ga_ref[...]) * pa + _sigmoid(gb_ref[...]) * pb
    out_ref[...] = x_ref[...] + gate_ref[...] * _dot(merged.astype(BF16), wout_ref[...])


def _merge(yt, zat, yb, gates, x, gate, woa_bf, wob_bf, wout_bf, tm, rows_per_seq, per_row):
    m = x.shape[0]
    full = lambda c: pl.BlockSpec((tm, D_MODEL), lambda i: (i, c))
    per_seq = rows_per_seq // tm
    y_spec = pl.BlockSpec((HEAD_SIZE, H_A, tm), lambda i: (0, i // per_seq, i % per_seq))
    return pl.pallas_call(
        _merge_kernel,
        grid=(m // tm,),
        in_specs=[y_spec, _t_spec(W_A, tm, rows_per_seq),
                  pl.BlockSpec((tm, W_B), lambda i: (i, 0)), full(0), full(1), full(0),
                  _mod_spec(per_row, tm, rows_per_seq),
                  pl.BlockSpec((W_A, D_MODEL), lambda i: (0, 0)),
                  pl.BlockSpec((W_B, D_MODEL), lambda i: (0, 0)),
                  pl.BlockSpec((D_MODEL, D_MODEL), lambda i: (0, 0))],
        out_specs=full(0),
        out_shape=jax.ShapeDtypeStruct((m, D_MODEL), F32),
        name="merge_out",
    )(yt, zat, yb, gates, gates, x, gate, woa_bf, wob_bf, wout_bf)


def _merge_rows_kernel(o_ref, bonus_ref, za_ref, yb_ref, ga_ref, gb_ref, x_ref, gate_ref, lng_ref, lnb_ref,
                       g512_ref, woa_ref, wob_ref, wout_ref, out_ref):
    o_b = o_ref[...]
    o = o_b.astype(F32)
    mean = _dot(o_b, g512_ref[...]) * (1.0 / HEAD_SIZE)
    c = o - mean
    var = _head_sum(c * c, g512_ref[...]) * (1.0 / HEAD_SIZE)
    on = c * lax.rsqrt(var + GN_EPS) * lng_ref[...] + lnb_ref[...]
    ya = (on + bonus_ref[...].astype(F32)) * _silu(za_ref[...])
    pa = _dot(ya.astype(BF16), woa_ref[...])
    pb = _dot(yb_ref[...], wob_ref[...])
    merged = _sigmoid(ga_ref[...]) * pa + _sigmoid(gb_ref[...]) * pb
    out_ref[...] = x_ref[...] + gate_ref[...] * _dot(merged.astype(BF16), wout_ref[...])


def _merge_rows(o, bonus, za, yb, gates, x, gate, lnx_g, lnx_b, g512, woa_bf, wob_bf, wout_bf, tm, rows_per_seq):
    m = x.shape[0]
    half = pl.BlockSpec((tm, W_A), lambda i: (i, 0))
    full = lambda c: pl.BlockSpec((tm, D_MODEL), lambda i: (i, c))
    vec = pl.BlockSpec((1, W_A), lambda i: (0, 0))
    return pl.pallas_call(
        _merge_rows_kernel,
        grid=(m // tm,),
        in_specs=[half, half, half, half, full(0), full(1), full(0),
                  _mod_spec(False, tm, rows_per_seq), vec, vec,
                  pl.BlockSpec((W_A, W_A), lambda i: (0, 0)),
                  pl.BlockSpec((W_A, D_MODEL), lambda i: (0, 0)),
                  pl.BlockSpec((W_B, D_MODEL), lambda i: (0, 0)),
                  pl.BlockSpec((D_MODEL, D_MODEL), lambda i: (0, 0))],
        out_specs=full(0),
        out_shape=jax.ShapeDtypeStruct((m, D_MODEL), F32),
        name="merge_out_rows",
    )(o, bonus, za, yb, gates, gates, x, gate, lnx_g.reshape(1, W_A), lnx_b.reshape(1, W_A), g512,
      woa_bf, wob_bf, wout_bf)


def _t5_bucket(dist):
    max_exact = N_BUCKETS // 2
    d = jnp.maximum(dist, 0)
    log_ratio = jnp.log(jnp.maximum(d, 1).astype(F32) / max_exact) / math.log(MAX_DISTANCE / max_exact)
    large = jnp.minimum(max_exact + (log_ratio * (N_BUCKETS - max_exact)).astype(jnp.int32), N_BUCKETS - 1)
    return jnp.where(d < max_exact, d, large)


def _block_ones(n, blk):
    i = np.arange(n) // blk
    return jnp.asarray((i[:, None] == i[None, :]).astype(np.float32), dtype=BF16)


def _rwkv_sample(usht, shift0, states, layer, lw):
    nb = usht.shape[2]
    outs = _prep(usht, _shift_major(shift0, 1).T[None], lw["mu"], lw["w0"], lw["a0"], lw["lorat"], lw["k_k"],
                 lw["k_a"], lw["r_k"], nb, True)
    tile = lambda a: jnp.broadcast_to(a.reshape(H_A, HEAD_SIZE, 1), (H_A, HEAD_SIZE, nb))
    y, state = _step([a.transpose(1, 0, 2) for a in outs[:6]], outs[6], tile(lw["lnx_g"]), tile(lw["lnx_b"]),
                     states, layer)
    return y.transpose(1, 0, 2), state.transpose(3, 0, 1, 2)


def kernel(x_prompt, x_sample, c_prompt, c_sample, state_wkv, state_shift, cache_k, cache_v, norm_g, w_ada, b_ada, w_in, mu_shift, w0, w_decay_up, a0, w_a_up, k_k, k_a, r_k, lnx_g, lnx_b, w_o_a, q_norm_g, k_norm_g, rel_bias, sinks, w_o_b, w_out):
    nbp, t_len, _ = x_prompt.shape
    nbs = x_sample.shape[0]
    depth = norm_g.shape[0]
    mp = nbp * t_len
    assert (nbp * H_A) % LANES == 0 and nbs % LANES == 0 and t_len % LANES == 0
    g512 = _block_ones(W_B, HEAD_DIM)
    g128 = _block_ones(KV_W, HEAD_DIM)

    gtab = rel_bias[_t5_bucket(2 * WINDOW - jnp.arange(BIAS_SPAN))].astype(F32).T
    bias_s = rel_bias[_t5_bucket(WINDOW - jnp.arange(WINDOW + 1))].astype(F32)
    pair = lambda a: jnp.concatenate([jnp.repeat(a[..., 0:Q_PER_KV, None], HEAD_DIM, axis=-1),
                                      jnp.repeat(a[..., Q_PER_KV:H_B, None], HEAD_DIM, axis=-1)], axis=-1)
    bias_rows = pair(bias_s[:WINDOW]).transpose(1, 0, 2)
    bias0 = pair(bias_s[WINDOW:]).transpose(1, 0, 2)

    states_t = state_wkv.transpose(0, 2, 3, 4, 1)
    cache_k_t = cache_k.transpose(0, 1, 3, 4, 2)
    cache_v_t = cache_v.transpose(0, 1, 3, 4, 2)
    c_all = jnp.concatenate([c_prompt, c_sample], axis=0)
    xp = x_prompt.reshape(mp, D_MODEL)
    xs = x_sample.reshape(nbs, D_MODEL)
    tm_p = _pick((512, 256, 128), t_len)
    tm_in = _pick((512, 256, 128), t_len)
    outs = {k: [] for k in ("wkv_p", "shift_p", "kw_p", "vw_p", "wkv_s", "shift_s", "kw_s", "vw_s")}
    for l in range(depth):
        wl = w_in[l].astype(BF16)
        base = SHIFT_W
        z_a, q, kb, vb, z_b, gts = (wl[:, base:base + 512], wl[:, base + 512:base + 1024],
                                    wl[:, base + 1024:base + 1152], wl[:, base + 1152:base + 1280],
                                    wl[:, base + 1280:base + 1792], wl[:, base + 1792:])
        w_bf = jnp.concatenate([_head_pairs(q, 1), kb, vb, _head_pairs(z_b, 1), gts], axis=1)
        wa_bf = jnp.concatenate([wl[:, :SHIFT_W], z_a], axis=1)
        wt_bf = jnp.concatenate([_shift_major(wl[:, :SHIFT_W], 1), _chan_major(z_a, 1)], axis=1).T
        zeros = jnp.zeros((LORA, W_A), F32)
        lora = jnp.concatenate([jnp.concatenate([w_decay_up[l], zeros], axis=1),
                                jnp.concatenate([zeros, w_a_up[l]], axis=1)], axis=0).astype(BF16)
        lorat = jnp.concatenate([jnp.concatenate([_chan_major(w_decay_up[l], 1), zeros], axis=1),
                                 jnp.concatenate([zeros, _chan_major(w_a_up[l], 1)], axis=1)],
                                axis=0).T.astype(BF16)
        cm = lambda a: _chan_major(a.reshape(-1), 0)
        lw_p = dict(mu=mu_shift[l], w0=w0[l], a0=a0[l], lora=lora, k_k=k_k[l], k_a=k_a[l], r_k=r_k[l].reshape(-1))
        lw = dict(mu=_shift_major(mu_shift[l], 0), w0=cm(w0[l]), a0=cm(a0[l]), lorat=lorat,
                  k_k=cm(k_k[l]), k_a=cm(k_a[l]), r_k=cm(r_k[l]), lnx_g=lnx_g[l], lnx_b=lnx_b[l])
        woa_rows_bf = w_o_a[l].astype(BF16)
        woa_bf, wout_bf = _chan_major(woa_rows_bf, 0), w_out[l].astype(BF16)
        wob_bf = _head_pairs(w_o_b[l].astype(BF16), 0)
        q_gain = jnp.tile(q_norm_g[l], H_B).reshape(1, W_B)
        k_gain = jnp.tile(k_norm_g[l], KV_HEADS).reshape(1, KV_W)
        sink_col = jnp.repeat(sinks[l].reshape(KV_HEADS, Q_PER_KV), WINDOW, axis=1).reshape(
            KV_HEADS, Q_PER_KV * WINDOW, 1)
        sink_rows = pair(sinks[l].reshape(1, H_B)).transpose(1, 0, 2)

        mod = _ada(c_all, w_ada[l].astype(BF16), b_ada[l])
        shift, scale, gate = mod[:, :D_MODEL], mod[:, D_MODEL:2 * D_MODEL], mod[:, 2 * D_MODEL:]

        sp, scp, gp = (a[:nbp].reshape(nbp, 1, D_MODEL) for a in (shift, scale, gate))
        ush, za, qkv, kv32, zb, gts_p = _in_proj(xp, norm_g[l], scp, sp, w_bf, wa_bf, q_gain, k_gain, g512, g128,
                                                 tm_in, t_len, False, False)
        o, bonus, wkv = _chunk_scan(ush, nbp, lw_p, g512, _pick((CHUNKS_PER_STEP, 1), t_len // CHUNK))
        yb = _attn(qkv, zb, nbp, gtab, sink_col)
        xp = _merge_rows(o, bonus, za, yb, gts_p, xp, gp, lnx_g[l], lnx_b[l], g512, woa_rows_bf, wob_bf, wout_bf,
                         tm_p, t_len)
        win = kv32.reshape(nbp, t_len, 2, KV_HEADS, HEAD_DIM)[:, t_len - WINDOW:]
        outs["wkv_p"].append(wkv)
        outs["shift_p"].append(ush.reshape(nbp, t_len, SHIFT_W)[:, t_len - 1])
        outs["kw_p"].append(win[:, :, 0])
        outs["vw_p"].append(win[:, :, 1])

        ss, scs, gs = shift[nbp:], scale[nbp:], gate[nbp:]
        usht, zat, qkv, kv32, zb, gts_s = _in_proj(xs, norm_g[l], scs, ss, w_bf, wt_bf, q_gain, k_gain, g512, g128,
                                                   nbs, nbs, True, True)
        yt, wkv = _rwkv_sample(usht, state_shift[l], states_t, l, lw)
        yb, kwin, vwin = _attn_step(qkv, kv32, zb, cache_k_t, cache_v_t, l, bias_rows, bias0, sink_rows, g128,
                                    _pick((16, 8), nbs))
        xs = _merge(yt, zat, yb, gts_s, xs, gs, woa_bf, wob_bf, wout_bf, nbs, nbs, True)
        outs["wkv_s"].append(wkv)
        outs["shift_s"].append(_shift_major(usht[0].T, 1, inverse=True))
        outs["kw_s"].append(kwin.transpose(0, 3, 1, 2))
        outs["vw_s"].append(vwin.transpose(0, 3, 1, 2))

    st = lambda k: jnp.stack(outs[k])
    return (xp.reshape(nbp, t_len, D_MODEL), xs.reshape(nbs, 1, D_MODEL),
            st("wkv_p"), st("shift_p"), st("kw_p"), st("vw_p"),
            st("wkv_s"), st("shift_s"), st("kw_s"), st("vw_s"))
```

```python
import functools
import math

import numpy as np
import jax
import jax.numpy as jnp
from jax import lax
from jax.experimental import pallas as pl
from jax.experimental.pallas import tpu as pltpu

F32 = jnp.float32
BF16 = jnp.bfloat16

D_MODEL = 1024
W_A = 512
HEAD_SIZE = 64
H_A = W_A // HEAD_SIZE
LORA = 64
SHIFT_W = 3 * W_A + 2 * LORA
W_B = 512
HEAD_DIM = 64
H_B = W_B // HEAD_DIM
KV_HEADS = 2
Q_PER_KV = H_B // KV_HEADS
KV_W = KV_HEADS * HEAD_DIM
WINDOW = 128
N_BUCKETS = 32
MAX_DISTANCE = 128
NORM_EPS = 1e-6
GN_EPS = 64e-5
NEG_INF = -1e30
QKV_W = W_B + 2 * KV_W
GATE_W = 2 * D_MODEL
ROW_COLS = QKV_W + W_B + GATE_W
T_COLS = SHIFT_W + W_A
LANES = 128
SUBLANES = 8
def _head_pairs(x, axis):
    s = x.shape
    x = x.reshape(s[:axis] + (KV_HEADS, Q_PER_KV, HEAD_DIM) + s[axis + 1:])
    return jnp.swapaxes(x, axis, axis + 1).reshape(s)


def _chan_major(x, axis, inverse=False):
    s = x.shape
    split = (HEAD_SIZE, H_A) if inverse else (H_A, HEAD_SIZE)
    x = x.reshape(s[:axis] + split + s[axis + 1:])
    return jnp.swapaxes(x, axis, axis + 1).reshape(s)


def _shift_major(x, axis, inverse=False):
    parts = [lax.slice_in_dim(x, i * W_A, (i + 1) * W_A, axis=axis) for i in range(3)]
    tail = lax.slice_in_dim(x, 3 * W_A, SHIFT_W, axis=axis)
    return jnp.concatenate([_chan_major(p, axis, inverse) for p in parts] + [tail], axis=axis)
ATTN_BLOCKS = 16
BIAS_SPAN = 3 * WINDOW
CHUNK = 64
CHUNKS_PER_STEP = 8
PAIR = 2 * HEAD_SIZE
GROUP = PAIR
NEUMANN_STEPS = CHUNK.bit_length() - 2


def _dot(a, b):
    return jnp.dot(a, b, preferred_element_type=F32)


def _dot_nt(a, b):
    return lax.dot_general(a, b, (((1,), (1,)), ((), ())), preferred_element_type=F32)


def _dot_tn(a, b):
    return lax.dot_general(a, b, (((0,), (0,)), ((), ())), preferred_element_type=F32)


def _head_sum(x, g):
    return _dot(x.astype(BF16), g)


def _split_dot(x, g):
    hi = x.astype(BF16)
    lo = (x - hi.astype(F32)).astype(BF16)
    return _dot(hi, g) + _dot(lo, g)


def _silu(x):
    x = x.astype(F32)
    return x * jax.nn.sigmoid(x)


def _sigmoid(x):
    return jax.nn.sigmoid(x.astype(F32))


def _pick(cands, n):
    for c in cands:
        if n % c == 0:
            return c
    return n


def _ada_kernel(c_ref, w_ref, b_ref, o_ref):
    c = c_ref[...]
    o_ref[...] = _dot(_silu(c).astype(BF16), w_ref[...]) + b_ref[...]


def _ada(c, w_bf, b):
    nb, d = c.shape
    n = w_bf.shape[1]
    tn = 1024
    return pl.pallas_call(
        _ada_kernel,
        grid=(n // tn,),
        in_specs=[pl.BlockSpec((nb, d), lambda j: (0, 0)),
                  pl.BlockSpec((d, tn), lambda j: (0, j)),
                  pl.BlockSpec((1, tn), lambda j: (0, j))],
        out_specs=pl.BlockSpec((nb, tn), lambda j: (0, j)),
        out_shape=jax.ShapeDtypeStruct((nb, n), F32),
        name="ada",
    )(c, w_bf, b.reshape(1, n))


def _in_proj_kernel(x_ref, g_ref, scale_ref, shift_ref, w_ref, wa_ref, qg_ref, kg_ref, g512_ref, g128_ref,
                    ush_ref, za_ref, qkv_ref, kv32_ref, zb_ref, gate_ref, *, transposed):
    x = x_ref[...]
    ms = jnp.mean(x * x, axis=-1, keepdims=True)
    h = (x * lax.rsqrt(ms + NORM_EPS)) * g_ref[...]
    h = h * (1.0 + scale_ref[...]) + shift_ref[...]
    hb = h.astype(BF16)
    if transposed:
        ush_ref[...] = _dot_nt(wa_ref[0:SHIFT_W, :], hb)
        za_ref[...] = _dot_nt(wa_ref[SHIFT_W:T_COLS, :], hb).astype(BF16)
    else:
        ush_ref[...] = _dot(hb, wa_ref[:, 0:SHIFT_W])
        za_ref[...] = _dot(hb, wa_ref[:, SHIFT_W:T_COLS]).astype(BF16)
    qkv = _dot(hb, w_ref[:, 0:QKV_W])
    q, k, v = qkv[:, 0:W_B], qkv[:, W_B:W_B + KV_W], qkv[:, W_B + KV_W:QKV_W]
    qn = q * lax.rsqrt(_head_sum(q * q, g512_ref[...]) * (1.0 / HEAD_DIM) + NORM_EPS) * qg_ref[...]
    kn = k * lax.rsqrt(_head_sum(k * k, g128_ref[...]) * (1.0 / HEAD_DIM) + NORM_EPS) * kg_ref[...]
    qkv_ref[:, 0:W_B] = (qn * (HEAD_DIM ** -0.5)).astype(BF16)
    qkv_ref[:, W_B:W_B + KV_W] = kn.astype(BF16)
    qkv_ref[:, W_B + KV_W:QKV_W] = v.astype(BF16)
    kv32_ref[:, 0:KV_W] = kn
    kv32_ref[:, KV_W:2 * KV_W] = v
    zb_ref[...] = _dot(hb, w_ref[:, QKV_W:QKV_W + W_B]).astype(BF16)
    gate_ref[...] = _dot(hb, w_ref[:, QKV_W + W_B:ROW_COLS]).astype(BF16)


def _mod_spec(per_row, tm, rows_per_seq):
    if per_row:
        return pl.BlockSpec((tm, D_MODEL), lambda i: (i, 0))
    return pl.BlockSpec((None, 1, D_MODEL), lambda i: ((i * tm) // rows_per_seq, 0, 0))


def _t_spec(cols, tm, rows_per_seq):
    per_seq = rows_per_seq // tm
    return pl.BlockSpec((None, cols, tm), lambda i: (i // per_seq, 0, i % per_seq))


def _in_proj(x, norm_g, scale, shift, w_bf, wa_bf, q_gain, k_gain, g512, g128, tm, rows_per_seq, per_row,
             transposed):
    m = x.shape[0]
    nseq = m // rows_per_seq
    const = lambda r, c: pl.BlockSpec((r, c), lambda i: (0, 0))
    rows = lambda w: pl.BlockSpec((tm, w), lambda i: (i, 0))
    widths = (QKV_W, 2 * KV_W, W_B, GATE_W)
    dtypes = (BF16, F32, BF16, BF16)
    if transposed:
        a_specs = [_t_spec(SHIFT_W, tm, rows_per_seq), _t_spec(W_A, tm, rows_per_seq)]
        a_shapes = [jax.ShapeDtypeStruct((nseq, SHIFT_W, rows_per_seq), F32),
                    jax.ShapeDtypeStruct((nseq, W_A, rows_per_seq), BF16)]
    else:
        a_specs = [rows(SHIFT_W), rows(W_A)]
        a_shapes = [jax.ShapeDtypeStruct((m, SHIFT_W), F32), jax.ShapeDtypeStruct((m, W_A), BF16)]
    return pl.pallas_call(
        functools.partial(_in_proj_kernel, transposed=transposed),
        grid=(m // tm,),
        in_specs=[rows(D_MODEL),
                  const(1, D_MODEL),
                  _mod_spec(per_row, tm, rows_per_seq),
                  _mod_spec(per_row, tm, rows_per_seq),
                  const(D_MODEL, ROW_COLS), const(*wa_bf.shape),
                  const(1, W_B), const(1, KV_W), const(W_B, W_B), const(KV_W, KV_W)],
        out_specs=a_specs + [rows(w) for w in widths],
        out_shape=a_shapes + [jax.ShapeDtypeStruct((m, w), dt) for w, dt in zip(widths, dtypes)],
        name="in_proj",
    )(x, norm_g.reshape(1, D_MODEL), scale, shift, w_bf, wa_bf, q_gain, k_gain, g512, g128)


def _prep_kernel(u_ref, prev_ref, shift0_ref, mu_ref, w0_ref, a0_ref, lora_ref, kk_ref, ka_ref, rk_ref,
                 r_o, w_o, k_o, v_o, kk_o, b_o, coef_o, *, per_token_state):
    tm = u_ref.shape[1]
    heads = lambda x: x.reshape(HEAD_SIZE, H_A, LANES)
    lane = lax.broadcasted_iota(jnp.int32, (SHIFT_W, LANES), 1)
    row = lax.broadcasted_iota(jnp.int32, (2 * LORA, LANES), 0)
    if not per_token_state:
        before = jnp.where(pl.program_id(1) == 0, shift0_ref[...], prev_ref[...])
        rolled_before = pltpu.roll(before, 1, axis=1)
    for ci in range(tm // LANES):
        cols = slice(ci * LANES, (ci + 1) * LANES)
        u = u_ref[:, cols]
        if per_token_state:
            u_prev = shift0_ref[:, cols]
        else:
            rolled = pltpu.roll(u, 1, axis=1)
            u_prev = jnp.where(lane == 0, rolled_before, rolled)
            rolled_before = rolled
        xs = u + (u_prev - u) * mu_ref[...]
        r = xs[0:W_A]
        k = xs[W_A:2 * W_A]
        v = xs[2 * W_A:3 * W_A]
        tail = xs[3 * W_A:SHIFT_W]
        lora_in = jnp.where(row < LORA, jnp.tanh(tail), tail).astype(BF16)
        up = _dot(lora_ref[...], lora_in)
        neg = -(w0_ref[...] + up[0:W_A])
        softplus = jnp.maximum(neg, 0.0) + jnp.log(1.0 + jnp.exp(-jnp.abs(neg)))
        w_log = -softplus - 0.5
        decay = jnp.exp(-jnp.exp(w_log))
        a = jax.nn.sigmoid(a0_ref[...] + up[W_A:2 * W_A])
        kk = heads(k * kk_ref[...])
        norm = jnp.sqrt(jnp.sum(kk * kk, axis=0, keepdims=True))
        kk = kk / jnp.maximum(norm, 1e-12)
        k_mod = k * (1.0 + (a - 1.0) * ka_ref[...])
        r_o[:, :, cols] = heads(r)
        w_o[:, :, cols] = heads(decay)
        k_o[:, :, cols] = heads(k_mod)
        v_o[:, :, cols] = heads(v)
        kk_o[:, :, cols] = kk
        b_o[:, :, cols] = kk * heads(a)
        coef_o[:, cols] = jnp.sum(heads(r * k_mod * rk_ref[...]), axis=0)


def _prep(usht, shift0t, mu, w0, a0, lorat_bf, k_k, k_a, r_k, tm, per_token_state):
    nseq, _, t_len = usht.shape
    col = lambda a: jnp.broadcast_to(a.reshape(-1, 1), (a.size, LANES))
    cvec = lambda n: pl.BlockSpec((n, LANES), lambda s, j: (0, 0))
    blk = lambda cols: pl.BlockSpec((None, cols, tm), lambda s, j: (s, 0, j))
    if per_token_state:
        prev_spec = pl.BlockSpec((None, SHIFT_W, LANES), lambda s, j: (0, 0, 0))
        shift0_spec = blk(SHIFT_W)
    else:
        prev_spec = pl.BlockSpec((None, SHIFT_W, LANES),
                                 lambda s, j: (s, 0, jnp.maximum(j * (tm // LANES) - 1, 0)))
        shift0_spec = pl.BlockSpec((None, SHIFT_W, LANES), lambda s, j: (s, 0, 0))
    chan = pl.BlockSpec((HEAD_SIZE, H_A, tm), lambda s, j: (0, s, j))
    return pl.pallas_call(
        functools.partial(_prep_kernel, per_token_state=per_token_state),
        grid=(nseq, t_len // tm),
        in_specs=[blk(SHIFT_W), prev_spec, shift0_spec, cvec(SHIFT_W), cvec(W_A), cvec(W_A),
                  pl.BlockSpec((2 * W_A, 2 * LORA), lambda s, j: (0, 0)), cvec(W_A), cvec(W_A), cvec(W_A)],
        out_specs=[chan] * 6 + [pl.BlockSpec((H_A, tm), lambda s, j: (s, j))],
        out_shape=[jax.ShapeDtypeStruct((HEAD_SIZE, nseq * H_A, t_len), F32)] * 6
                  + [jax.ShapeDtypeStruct((nseq * H_A, t_len), F32)],
        name="rwkv_prep",
    )(usht, usht, shift0t, col(mu), col(w0), col(a0), lorat_bf, col(k_k), col(k_a), col(r_k))


def _step_kernel(r_ref, w_ref, k_ref, v_ref, kk_ref, b_ref, coef_ref, lng_ref, lnb_ref, s_ref,
                 y_ref, so_ref, o_ref):
    kk, w, b, km, r = kk_ref[...], w_ref[...], b_ref[...], k_ref[...], r_ref[...]
    for vi in range(HEAD_SIZE):
        s = s_ref[vi]
        sa = -jnp.sum(s * kk, axis=0, keepdims=True)
        s = s * w + sa * b + v_ref[pl.ds(vi, 1), :] * km
        so_ref[vi] = s
        o_ref[pl.ds(vi, 1), :] = jnp.sum(s * r, axis=0, keepdims=True)
    o = o_ref[...]
    mean = jnp.sum(o, axis=0, keepdims=True) * (1.0 / HEAD_SIZE)
    c = o - mean
    var = jnp.sum(c * c, axis=0, keepdims=True) * (1.0 / HEAD_SIZE)
    coef = coef_ref[pl.ds(pl.program_id(0), 1), :]
    y_ref[...] = c * lax.rsqrt(var + GN_EPS) * lng_ref[...] + lnb_ref[...] + coef * v_ref[...]


def _step(seqs, coef, lng, lnb, states, layer):
    nb = coef.shape[1]
    seq_spec = pl.BlockSpec((None, HEAD_SIZE, nb), lambda h: (h, 0, 0))
    state_in = pl.BlockSpec((None, None, HEAD_SIZE, HEAD_SIZE, nb), lambda h: (layer, h, 0, 0, 0))
    state_spec = pl.BlockSpec((None, HEAD_SIZE, HEAD_SIZE, nb), lambda h: (h, 0, 0, 0))
    return pl.pallas_call(
        _step_kernel,
        grid=(H_A,),
        in_specs=[seq_spec] * 6 + [pl.BlockSpec((H_A, nb), lambda h: (0, 0)), seq_spec, seq_spec, state_in],
        out_specs=[seq_spec, state_spec],
        out_shape=[jax.ShapeDtypeStruct((H_A, HEAD_SIZE, nb), F32),
                   jax.ShapeDtypeStruct((H_A, HEAD_SIZE, HEAD_SIZE, nb), F32)],
        scratch_shapes=[pltpu.VMEM((HEAD_SIZE, nb), F32)],
        name="rwkv_step",
    )(*seqs, coef, lng, lnb, states)


def _prep_rows(u_ref, prev_ref, first, mu_ref, w0_ref, a0_ref, lora_ref, kk_ref, ka_ref, rk_ref, g_ref):
    u = u_ref[...]
    prev_row = jnp.where(first, 0.0, prev_ref[SUBLANES - 1:SUBLANES, :])
    row = lax.broadcasted_iota(jnp.int32, u.shape, 0)
    u_prev = jnp.where(row == 0, prev_row, pltpu.roll(u, 1, axis=0))
    xs = u + (u_prev - u) * mu_ref[...]
    r = xs[:, 0:W_A]
    k = xs[:, W_A:2 * W_A]
    v = xs[:, 2 * W_A:3 * W_A]
    tail = xs[:, 3 * W_A:SHIFT_W]
    lane = lax.broadcasted_iota(jnp.int32, tail.shape, 1)
    lora_in = jnp.where(lane < LORA, jnp.tanh(tail), tail).astype(BF16)
    up = _dot(lora_in, lora_ref[...])
    neg = -(w0_ref[...] + up[:, 0:W_A])
    softplus = jnp.maximum(neg, 0.0) + jnp.log(1.0 + jnp.exp(-jnp.abs(neg)))
    w_log = -softplus - 0.5
    a = jax.nn.sigmoid(a0_ref[...] + up[:, W_A:2 * W_A])
    kk = k * kk_ref[...]
    norm = jnp.sqrt(_head_sum(kk * kk, g_ref[...]))
    kk = kk / jnp.maximum(norm, 1e-12)
    k_mod = k * (1.0 + (a - 1.0) * ka_ref[...])
    lw = -jnp.exp(w_log)
    bonus = _head_sum(r * k_mod * rk_ref[...], g_ref[...]) * v
    return r, lw, k_mod, v, kk, kk * a, bonus


def _chunk_kernel(u_ref, prev_ref, mu_ref, w0_ref, a0_ref, lora_ref, kkp_ref, ka_ref, rk_ref, g_ref,
                  o_ref, bonus_ref, pf_ref, p_ref, *, nchunk):
    npair = W_A // PAIR
    first = pl.program_id(1) == 0

    @pl.when(first)
    def _():
        p_ref[...] = jnp.zeros(p_ref.shape, F32)

    seqs = _prep_rows(u_ref, prev_ref, first, mu_ref, w0_ref, a0_ref, lora_ref, kkp_ref, ka_ref, rk_ref, g_ref)
    bonus_ref[...] = seqs[6].astype(BF16)

    tq = lax.broadcasted_iota(jnp.int32, (CHUNK, GROUP), 0)
    lq = lax.broadcasted_iota(jnp.int32, (CHUNK, GROUP), 1)
    head_of = lq // HEAD_SIZE
    col = lq % HEAD_SIZE
    strict = tq > col
    incl = tq >= col
    eye = jnp.where(tq == col, 1.0, 0.0).astype(F32)
    ti = lax.broadcasted_iota(jnp.int32, (CHUNK, CHUNK), 0)
    tj = lax.broadcasted_iota(jnp.int32, (CHUNK, CHUNK), 1)
    tri = jnp.where(ti >= tj, 1.0, 0.0).astype(BF16)
    pi = lax.broadcasted_iota(jnp.int32, (PAIR, PAIR), 0)
    pj = lax.broadcasted_iota(jnp.int32, (PAIR, PAIR), 1)
    same_head = (pi < HEAD_SIZE) == (pj < HEAD_SIZE)
    on_diag = pi == pj
    bf = lambda x: x.astype(BF16)
    cat = lambda a, b: jnp.concatenate([a, b], axis=1)

    def spread(x):
        return jnp.concatenate([jnp.where(head_of == j, x, jnp.zeros_like(x)) for j in range(GROUP // HEAD_SIZE)],
                               axis=0)

    items = [(gr, c) for gr in range(W_A // GROUP) for c in range(nchunk)]
    pre = []
    for gr, c in items:
        r, lw, k, v, kk, b = (x[c * CHUNK:(c + 1) * CHUNK, gr * GROUP:(gr + 1) * GROUP] for x in seqs[:6])
        g = _dot(tri, bf(lw))
        g = g + _dot(tri, bf(lw - bf(lw).astype(F32)))
        g_end = g[CHUNK - 1:CHUNK, :]
        e_neg = jnp.exp(-g)
        e_end = jnp.exp(g_end - g)
        kkt = kk * jnp.exp(g - lw)
        rt = r * jnp.exp(g)
        pre.append(dict(rt=rt, lhs=bf(jnp.concatenate([kkt, rt], axis=0)), kkt_b=bf(kkt), bt_b=bf(b * e_neg),
                        kt_b=bf(k * e_neg), v_b=bf(v), bh_b=bf(b * e_end), kh_b=bf(k * e_end),
                        gamma_end=jnp.exp(g_end)))
    xb = [_dot_nt(it["lhs"], spread(it["bt_b"])) for it in pre]
    xk = [_dot_nt(it["lhs"], spread(it["kt_b"])) for it in pre]
    l_b = [jnp.where(strict, x[0:CHUNK], 0.0) for x in xb]
    l_k = [jnp.where(strict, x[0:CHUNK], 0.0) for x in xk]
    m_b = [jnp.where(incl, x[CHUNK:2 * CHUNK], 0.0) for x in xb]
    m_k = [jnp.where(incl, x[CHUNK:2 * CHUNK], 0.0) for x in xk]
    t_inv = [eye - x for x in l_b]
    power = l_b
    for _ in range(NEUMANN_STEPS):
        power = [_dot(bf(x), spread(bf(x))) for x in power]
        t_inv = [t + _dot(bf(t), spread(bf(x))) for t, x in zip(t_inv, power)]
    v_s = [spread(it["v_b"]) for it in pre]
    lkv = [_dot(bf(x), vs) for x, vs in zip(l_k, v_s)]
    tx = [_dot(bf(t), cat(spread(it["kkt_b"]), spread(bf(x)))) for t, it, x in zip(t_inv, pre, lkv)]
    mx = [_dot(bf(m), cat(spread(bf(x[:, 0:GROUP])), spread(bf(x[:, GROUP:2 * GROUP])))) for m, x in zip(m_b, tx)]
    mv = [_dot(bf(m), vs) for m, vs in zip(m_k, v_s)]
    affine = {}
    for (gr, c), it, x, y, z in zip(items, pre, tx, mx, mv):
        q_eff = it["rt"] - y[:, 0:GROUP]
        o_loc = z - y[:, GROUP:2 * GROUP]
        for half in range(GROUP // PAIR):
            lanes = slice(half * PAIR, (half + 1) * PAIR)
            wu = cat(bf(x[:, lanes]), bf(x[:, GROUP + half * PAIR:GROUP + (half + 1) * PAIR]))
            bx = _dot_tn(it["bh_b"][:, lanes], wu)
            kv = _dot_tn(it["kh_b"][:, lanes], it["v_b"][:, lanes])
            a_eff = jnp.where(on_diag, it["gamma_end"][:, lanes], 0.0) - jnp.where(same_head, bx[:, 0:PAIR], 0.0)
            p_loc = jnp.where(same_head, kv - bx[:, PAIR:2 * PAIR], 0.0)
            affine[(gr * (GROUP // PAIR) + half, c)] = (bf(a_eff), p_loc, bf(q_eff[:, lanes]), o_loc[:, lanes])

    state = [p_ref[pr] for pr in range(npair)]
    for c in range(nchunk):
        for pr in range(npair):
            a_eff, p_loc, q_eff, o_loc = affine[(pr, c)]
            p_b = bf(state[pr])
            o_ref[c * CHUNK:(c + 1) * CHUNK, pr * PAIR:(pr + 1) * PAIR] = (_dot(q_eff, p_b) + o_loc).astype(BF16)
            state[pr] = _dot(a_eff, p_b) + p_loc
    for pr in range(npair):
        p_ref[pr] = state[pr]

    @pl.when(pl.program_id(1) == pl.num_programs(1) - 1)
    def _():
        for pr in range(npair):
            pf_ref[2 * pr] = state[pr][0:HEAD_SIZE, 0:HEAD_SIZE].T
            pf_ref[2 * pr + 1] = state[pr][HEAD_SIZE:PAIR, HEAD_SIZE:PAIR].T


def _chunk_scan(ush, nseq, lw, g512, nchunk):
    m = ush.shape[0]
    tt = nchunk * CHUNK
    steps = m // (nseq * tt)
    row = lambda a: a.reshape(1, -1)
    vec = lambda n: pl.BlockSpec((1, n), lambda s, i: (0, 0))
    blk = lambda w: pl.BlockSpec((tt, w), lambda s, i: (s * steps + i, 0))
    return pl.pallas_call(
        functools.partial(_chunk_kernel, nchunk=nchunk),
        grid=(nseq, steps),
        in_specs=[blk(SHIFT_W),
                  pl.BlockSpec((SUBLANES, SHIFT_W),
                               lambda s, i: (jnp.maximum((s * steps + i) * (tt // SUBLANES) - 1, 0), 0)),
                  vec(SHIFT_W), vec(W_A), vec(W_A),
                  pl.BlockSpec((2 * LORA, 2 * W_A), lambda s, i: (0, 0)),
                  vec(W_A), vec(W_A), vec(W_A),
                  pl.BlockSpec((W_A, W_A), lambda s, i: (0, 0))],
        out_specs=[blk(W_A), blk(W_A),
                   pl.BlockSpec((None, H_A, HEAD_SIZE, HEAD_SIZE), lambda s, i: (s, 0, 0, 0))],
        out_shape=[jax.ShapeDtypeStruct((m, W_A), BF16), jax.ShapeDtypeStruct((m, W_A), BF16),
                   jax.ShapeDtypeStruct((nseq, H_A, HEAD_SIZE, HEAD_SIZE), F32)],
        scratch_shapes=[pltpu.VMEM((W_A // PAIR, PAIR, PAIR), F32)],
        compiler_params=pltpu.CompilerParams(dimension_semantics=("arbitrary", "arbitrary")),
        name="rwkv_chunk_scan",
    )(ush, ush, row(lw["mu"]), row(lw["w0"]), row(lw["a0"]), lw["lora"], row(lw["k_k"]), row(lw["k_a"]),
      row(lw["r_k"]), g512)


def _attn_kernel(cur_ref, prev_ref, zb_ref, gtab_ref, sink_ref, yb_ref, bias_ref, *, blocks):
    s = pl.program_id(1)
    rows = Q_PER_KV * WINDOW

    @pl.when((pl.program_id(0) == 0) & (s == 0))
    def _():
        qi = lax.broadcasted_iota(jnp.int32, (WINDOW, 2 * WINDOW), 0)
        kj = lax.broadcasted_iota(jnp.int32, (WINDOW, 2 * WINDOW), 1)
        dist = qi + WINDOW - kj
        band = (dist >= 0) & (dist <= WINDOW)
        for g in range(KV_HEADS):
            for j in range(Q_PER_KV):
                h = g * Q_PER_KV + j
                row = jnp.broadcast_to(gtab_ref[h:h + 1, :], (WINDOW, BIAS_SPAN))
                toe = pltpu.roll(row, 0, 1, stride=1, stride_axis=0)[:, WINDOW:BIAS_SPAN]
                bias_ref[g, j * WINDOW:(j + 1) * WINDOW, :] = jnp.where(band, toe, NEG_INF)

    cur = cur_ref[...]
    prev = prev_ref[...]
    k_all = jnp.concatenate([prev[:, W_B:W_B + KV_W], cur[:, W_B:W_B + KV_W]], axis=0)
    v_all = jnp.concatenate([prev[:, W_B + KV_W:QKV_W], cur[:, W_B + KV_W:QKV_W]], axis=0)
    lane = lax.broadcasted_iota(jnp.int32, k_all.shape, 1)
    zero = jnp.zeros_like(k_all)
    k_g = [jnp.where(lane < HEAD_DIM, k_all, zero), jnp.where(lane >= HEAD_DIM, k_all, zero)]
    v_g = [jnp.where(lane < HEAD_DIM, v_all, zero), jnp.where(lane >= HEAD_DIM, v_all, zero)]
    ones = jnp.ones((2 * WINDOW, LANES), BF16)
    kj = lax.broadcasted_iota(jnp.int32, (rows, 2 * WINDOW), 1)
    insts = [(i, g) for i in range(blocks) for g in range(KV_HEADS)]
    keys_of = lambda x, i: x[i * WINDOW:(i + 2) * WINDOW]
    q4 = [jnp.concatenate([cur[i * WINDOW:(i + 1) * WINDOW, j * LANES:(j + 1) * LANES] for j in range(Q_PER_KV)],
                          axis=0) for i in range(blocks)]
    logits = []
    for i, g in insts:
        lg = _dot_nt(q4[i], keys_of(k_g[g], i)) + bias_ref[g]
        if i == 0:
            lg = jnp.where((kj >= WINDOW) | (s > 0), lg, NEG_INF)
        logits.append(lg)
    tops = [jnp.maximum(jnp.max(lg, axis=-1, keepdims=True), sink_ref[g]) for (i, g), lg in zip(insts, logits)]
    probs = [jnp.exp(lg - m).astype(BF16) for lg, m in zip(logits, tops)]
    dens = [_dot(p, ones) + jnp.exp(sink_ref[g] - m) for (i, g), p, m in zip(insts, probs, tops)]
    outs = [_dot(p, keys_of(v_g[g], i)) / den for (i, g), p, den in zip(insts, probs, dens)]
    for i in range(blocks):
        acc = outs[KV_HEADS * i] + outs[KV_HEADS * i + 1]
        r0 = i * WINDOW
        for j in range(Q_PER_KV):
            zb = zb_ref[r0:r0 + WINDOW, j * LANES:(j + 1) * LANES]
            yb_ref[r0:r0 + WINDOW, j * LANES:(j + 1) * LANES] = (
                acc[j * WINDOW:(j + 1) * WINDOW] * _silu(zb)).astype(BF16)


def _attn(qkv, zb, nb_seq, gtab, sink_col):
    m = qkv.shape[0]
    nblk = m // (nb_seq * WINDOW)
    blocks = _pick((ATTN_BLOCKS, 2, 1), nblk)
    steps = nblk // blocks
    rows = Q_PER_KV * WINDOW
    return pl.pallas_call(
        functools.partial(_attn_kernel, blocks=blocks),
        grid=(nb_seq, steps),
        in_specs=[pl.BlockSpec((blocks * WINDOW, QKV_W), lambda b, s: (b * steps + s, 0)),
                  pl.BlockSpec((WINDOW, QKV_W), lambda b, s: (jnp.maximum((b * steps + s) * blocks - 1, 0), 0)),
                  pl.BlockSpec((blocks * WINDOW, W_B), lambda b, s: (b * steps + s, 0)),
                  pl.BlockSpec((H_B, BIAS_SPAN), lambda b, s: (0, 0)),
                  pl.BlockSpec((KV_HEADS, rows, 1), lambda b, s: (0, 0, 0))],
        out_specs=pl.BlockSpec((blocks * WINDOW, W_B), lambda b, s: (b * steps + s, 0)),
        out_shape=jax.ShapeDtypeStruct((m, W_B), BF16),
        scratch_shapes=[pltpu.VMEM((KV_HEADS, rows, 2 * WINDOW), F32)],
        compiler_params=pltpu.CompilerParams(dimension_semantics=("arbitrary", "arbitrary")),
        name="swa_prompt",
    )(qkv, qkv, zb, gtab, sink_col)


def _attn_step_kernel(q_ref, kn_ref, vn_ref, zb_ref, ck_ref, cv_ref, bias_ref, bias0_ref, sink_ref, g128_ref,
                      yb_ref, ko_ref, vo_ref):
    q = q_ref[...].astype(F32)
    kn = kn_ref[...]
    vn = vn_ref[...]

    def rows_of(ref):
        x = ref[...]
        return jnp.concatenate([jnp.swapaxes(x[:, g], 1, 2) for g in range(KV_HEADS)], axis=-1)

    def store(ref, y):
        for g in range(KV_HEADS):
            ref[:, g] = jnp.swapaxes(y[:, :, g * HEAD_DIM:(g + 1) * HEAD_DIM], 1, 2)

    ck = rows_of(ck_ref)
    cv = rows_of(cv_ref)
    bt = q.shape[0]
    for r in range(Q_PER_KV):
        qsel = q[:, r * LANES:(r + 1) * LANES]
        prod = (ck * qsel[:, None, :]).reshape(bt * WINDOW, KV_W)
        lg = _split_dot(prod, g128_ref[...]).reshape(bt, WINDOW, KV_W) + bias_ref[r]
        lgn = _split_dot(kn * qsel, g128_ref[...]) + bias0_ref[r]
        s = sink_ref[r]
        m = jnp.maximum(jnp.maximum(jnp.max(lg, axis=1), lgn), s)
        p = jnp.exp(lg - m[:, None, :])
        pn = jnp.exp(lgn - m)
        den = jnp.sum(p, axis=1) + pn + jnp.exp(s - m)
        o = (jnp.sum(p * cv, axis=1) + pn * vn) / den
        zb = zb_ref[:, r * LANES:(r + 1) * LANES]
        yb_ref[:, r * LANES:(r + 1) * LANES] = (o * _silu(zb)).astype(BF16)
    j = lax.broadcasted_iota(jnp.int32, ck.shape, 1)
    store(ko_ref, jnp.where(j == WINDOW - 1, kn[:, None, :], pltpu.roll(ck, WINDOW - 1, axis=1)))
    store(vo_ref, jnp.where(j == WINDOW - 1, vn[:, None, :], pltpu.roll(cv, WINDOW - 1, axis=1)))


def _attn_step(qkv, kv32, zb, cache_k, cache_v, layer, bias_rows, bias0, sink_rows, g128, bt):
    nb = qkv.shape[0]
    cache_in = pl.BlockSpec((None, bt, KV_HEADS, HEAD_DIM, WINDOW), lambda i: (layer, i, 0, 0, 0))
    cache_spec = pl.BlockSpec((bt, KV_HEADS, HEAD_DIM, WINDOW), lambda i: (i, 0, 0, 0))
    return pl.pallas_call(
        _attn_step_kernel,
        grid=(nb // bt,),
        in_specs=[pl.BlockSpec((bt, W_B), lambda i: (i, 0)),
                  pl.BlockSpec((bt, KV_W), lambda i: (i, 0)),
                  pl.BlockSpec((bt, KV_W), lambda i: (i, 1)),
                  pl.BlockSpec((bt, W_B), lambda i: (i, 0)),
                  cache_in, cache_in,
                  pl.BlockSpec((Q_PER_KV, WINDOW, KV_W), lambda i: (0, 0, 0)),
                  pl.BlockSpec((Q_PER_KV, 1, KV_W), lambda i: (0, 0, 0)),
                  pl.BlockSpec((Q_PER_KV, 1, KV_W), lambda i: (0, 0, 0)),
                  pl.BlockSpec((KV_W, KV_W), lambda i: (0, 0))],
        out_specs=[pl.BlockSpec((bt, W_B), lambda i: (i, 0)), cache_spec, cache_spec],
        out_shape=[jax.ShapeDtypeStruct((nb, W_B), BF16),
                   jax.ShapeDtypeStruct(cache_k.shape[1:], F32),
                   jax.ShapeDtypeStruct(cache_v.shape[1:], F32)],
        name="swa_step",
    )(qkv, kv32, kv32, zb, cache_k, cache_v, bias_rows, bias0, sink_rows, g128)


def _merge_kernel(yt_ref, zat_ref, yb_ref, ga_ref, gb_ref, x_ref, gate_ref, woa_ref, wob_ref, wout_ref, out_ref):
    yt = yt_ref[...]
    yat = (yt.reshape(W_A, yt.shape[2]) * _silu(zat_ref[...])).astype(BF16)
    pa = _dot_tn(yat, woa_ref[...])
    pb = _dot(yb_ref[...], wob_ref[...])
    merged = _sigmoid(ga_ref[...]) * pa + _sigmoid(gb_ref[...]) * pb
    out_ref[...] = x_ref[...] + gate_ref[...] * _dot(merged.astype(BF16), wout_ref[...])


def _merge(yt, zat, yb, gates, x, gate, woa_bf, wob_bf, wout_bf, tm, rows_per_seq, per_row):
    m = x.shape[0]
    full = lambda c: pl.BlockSpec((tm, D_MODEL), lambda i: (i, c))
    per_seq = rows_per_seq // tm
    y_spec = pl.BlockSpec((HEAD_SIZE, H_A, tm), lambda i: (0, i // per_seq, i % per_seq))
    return pl.pallas_call(
        _merge_kernel,
        grid=(m // tm,),
        in_specs=[y_spec, _t_spec(W_A, tm, rows_per_seq),
                  pl.BlockSpec((tm, W_B), lambda i: (i, 0)), full(0), full(1), full(0),
                  _mod_spec(per_row, tm, rows_per_seq),
                  pl.BlockSpec((W_A, D_MODEL), lambda i: (0, 0)),
                  pl.BlockSpec((W_B, D_MODEL), lambda i: (0, 0)),
                  pl.BlockSpec((D_MODEL, D_MODEL), lambda i: (0, 0))],
        out_specs=full(0),
        out_shape=jax.ShapeDtypeStruct((m, D_MODEL), F32),
        name="merge_out",
    )(yt, zat, yb, gates, gates, x, gate, woa_bf, wob_bf, wout_bf)


def _merge_rows_kernel(o_ref, bonus_ref, za_ref, yb_ref, ga_ref, gb_ref, x_ref, gate_ref, lng_ref, lnb_ref,
                       g512_ref, woa_ref, wob_ref, wout_ref, out_ref):
    o_b = o_ref[...]
    o = o_b.astype(F32)
    mean = _dot(o_b, g512_ref[...]) * (1.0 / HEAD_SIZE)
    c = o - mean
    var = _head_sum(c * c, g512_ref[...]) * (1.0 / HEAD_SIZE)
    on = c * lax.rsqrt(var + GN_EPS) * lng_ref[...] + lnb_ref[...]
    ya = (on + bonus_ref[...].astype(F32)) * _silu(za_ref[...])
    pa = _dot(ya.astype(BF16), woa_ref[...])
    pb = _dot(yb_ref[...], wob_ref[...])
    merged = _sigmoid(ga_ref[...]) * pa + _sigmoid(gb_ref[...]) * pb
    out_ref[...] = x_ref[...] + gate_ref[...] * _dot(merged.astype(BF16), wout_ref[...])


def _merge_rows(o, bonus, za, yb, gates, x, gate, lnx_g, lnx_b, g512, woa_bf, wob_bf, wout_bf, tm, rows_per_seq):
    m = x.shape[0]
    half = pl.BlockSpec((tm, W_A), lambda i: (i, 0))
    full = lambda c: pl.BlockSpec((tm, D_MODEL), lambda i: (i, c))
    vec = pl.BlockSpec((1, W_A), lambda i: (0, 0))
    return pl.pallas_call(
        _merge_rows_kernel,
        grid=(m // tm,),
        in_specs=[half, half, half, half, full(0), full(1), full(0),
                  _mod_spec(False, tm, rows_per_seq), vec, vec,
                  pl.BlockSpec((W_A, W_A), lambda i: (0, 0)),
                  pl.BlockSpec((W_A, D_MODEL), lambda i: (0, 0)),
                  pl.BlockSpec((W_B, D_MODEL), lambda i: (0, 0)),
                  pl.BlockSpec((D_MODEL, D_MODEL), lambda i: (0, 0))],
        out_specs=full(0),
        out_shape=jax.ShapeDtypeStruct((m, D_MODEL), F32),
        name="merge_out_rows",
    )(o, bonus, za, yb, gates, gates, x, gate, lnx_g.reshape(1, W_A), lnx_b.reshape(1, W_A), g512,
      woa_bf, wob_bf, wout_bf)


def _t5_bucket(dist):
    max_exact = N_BUCKETS // 2
    d = jnp.maximum(dist, 0)
    log_ratio = jnp.log(jnp.maximum(d, 1).astype(F32) / max_exact) / math.log(MAX_DISTANCE / max_exact)
    large = jnp.minimum(max_exact + (log_ratio * (N_BUCKETS - max_exact)).astype(jnp.int32), N_BUCKETS - 1)
    return jnp.where(d < max_exact, d, large)


def _block_ones(n, blk):
    i = np.arange(n) // blk
    return jnp.asarray((i[:, None] == i[None, :]).astype(np.float32), dtype=BF16)


def _rwkv_sample(usht, shift0, states, layer, lw):
    nb = usht.shape[2]
    outs = _prep(usht, _shift_major(shift0, 1).T[None], lw["mu"], lw["w0"], lw["a0"], lw["lorat"], lw["k_k"],
                 lw["k_a"], lw["r_k"], nb, True)
    tile = lambda a: jnp.broadcast_to(a.reshape(H_A, HEAD_SIZE, 1), (H_A, HEAD_SIZE, nb))
    y, state = _step([a.transpose(1, 0, 2) for a in outs[:6]], outs[6], tile(lw["lnx_g"]), tile(lw["lnx_b"]),
                     states, layer)
    return y.transpose(1, 0, 2), state.transpose(3, 0, 1, 2)


def kernel(x_prompt, x_sample, c_prompt, c_sample, state_wkv, state_shift, cache_k, cache_v, norm_g, w_ada, b_ada, w_in, mu_shift, w0, w_decay_up, a0, w_a_up, k_k, k_a, r_k, lnx_g, lnx_b, w_o_a, q_norm_g, k_norm_g, rel_bias, sinks, w_o_b, w_out):
    nbp, t_len, _ = x_prompt.shape
    nbs = x_sample.shape[0]
    depth = norm_g.shape[0]
    mp = nbp * t_len
    assert (nbp * H_A) % LANES == 0 and nbs % LANES == 0 and t_len % LANES == 0
    g512 = _block_ones(W_B, HEAD_DIM)
    g128 = _block_ones(KV_W, HEAD_DIM)

    gtab = rel_bias[_t5_bucket(2 * WINDOW - jnp.arange(BIAS_SPAN))].astype(F32).T
    bias_s = rel_bias[_t5_bucket(WINDOW - jnp.arange(WINDOW + 1))].astype(F32)
    pair = lambda a: jnp.concatenate([jnp.repeat(a[..., 0:Q_PER_KV, None], HEAD_DIM, axis=-1),
                                      jnp.repeat(a[..., Q_PER_KV:H_B, None], HEAD_DIM, axis=-1)], axis=-1)
    bias_rows = pair(bias_s[:WINDOW]).transpose(1, 0, 2)
    bias0 = pair(bias_s[WINDOW:]).transpose(1, 0, 2)

    states_t = state_wkv.transpose(0, 2, 3, 4, 1)
    cache_k_t = cache_k.transpose(0, 1, 3, 4, 2)
    cache_v_t = cache_v.transpose(0, 1, 3, 4, 2)
    c_all = jnp.concatenate([c_prompt, c_sample], axis=0)
    xp = x_prompt.reshape(mp, D_MODEL)
    xs = x_sample.reshape(nbs, D_MODEL)
    tm_p = _pick((1024, 512, 256, 128), t_len)
    tm_in = _pick((512, 256, 128), t_len)
    outs = {k: [] for k in ("wkv_p", "shift_p", "kw_p", "vw_p", "wkv_s", "shift_s", "kw_s", "vw_s")}
    for l in range(depth):
        wl = w_in[l].astype(BF16)
        base = SHIFT_W
        z_a, q, kb, vb, z_b, gts = (wl[:, base:base + 512], wl[:, base + 512:base + 1024],
                                    wl[:, base + 1024:base + 1152], wl[:, base + 1152:base + 1280],
                                    wl[:, base + 1280:base + 1792], wl[:, base + 1792:])
        w_bf = jnp.concatenate([_head_pairs(q, 1), kb, vb, _head_pairs(z_b, 1), gts], axis=1)
        wa_bf = jnp.concatenate([wl[:, :SHIFT_W], z_a], axis=1)
        wt_bf = jnp.concatenate([_shift_major(wl[:, :SHIFT_W], 1), _chan_major(z_a, 1)], axis=1).T
        zeros = jnp.zeros((LORA, W_A), F32)
        lora = jnp.concatenate([jnp.concatenate([w_decay_up[l], zeros], axis=1),
                                jnp.concatenate([zeros, w_a_up[l]], axis=1)], axis=0).astype(BF16)
        lorat = jnp.concatenate([jnp.concatenate([_chan_major(w_decay_up[l], 1), zeros], axis=1),
                                 jnp.concatenate([zeros, _chan_major(w_a_up[l], 1)], axis=1)],
                                axis=0).T.astype(BF16)
        cm = lambda a: _chan_major(a.reshape(-1), 0)
        lw_p = dict(mu=mu_shift[l], w0=w0[l], a0=a0[l], lora=lora, k_k=k_k[l], k_a=k_a[l], r_k=r_k[l].reshape(-1))
        lw = dict(mu=_shift_major(mu_shift[l], 0), w0=cm(w0[l]), a0=cm(a0[l]), lorat=lorat,
                  k_k=cm(k_k[l]), k_a=cm(k_a[l]), r_k=cm(r_k[l]), lnx_g=lnx_g[l], lnx_b=lnx_b[l])
        woa_rows_bf = w_o_a[l].astype(BF16)
        woa_bf, wout_bf = _chan_major(woa_rows_bf, 0), w_out[l].astype(BF16)
        wob_bf = _head_pairs(w_o_b[l].astype(BF16), 0)
        q_gain = jnp.tile(q_norm_g[l], H_B).reshape(1, W_B)
        k_gain = jnp.tile(k_norm_g[l], KV_HEADS).reshape(1, KV_W)
        sink_col = jnp.repeat(sinks[l].reshape(KV_HEADS, Q_PER_KV), WINDOW, axis=1).reshape(
            KV_HEADS, Q_PER_KV * WINDOW, 1)
        sink_rows = pair(sinks[l].reshape(1, H_B)).transpose(1, 0, 2)

        mod = _ada(c_all, w_ada[l].astype(BF16), b_ada[l])
        shift, scale, gate = mod[:, :D_MODEL], mod[:, D_MODEL:2 * D_MODEL], mod[:, 2 * D_MODEL:]

        sp, scp, gp = (a[:nbp].reshape(nbp, 1, D_MODEL) for a in (shift, scale, gate))
        ush, za, qkv, kv32, zb, gts_p = _in_proj(xp, norm_g[l], scp, sp, w_bf, wa_bf, q_gain, k_gain, g512, g128,
                                                 tm_in, t_len, False, False)
        o, bonus, wkv = _chunk_scan(ush, nbp, lw_p, g512, _pick((CHUNKS_PER_STEP, 1), t_len // CHUNK))
        yb = _attn(qkv, zb, nbp, gtab, sink_col)
        xp = _merge_rows(o, bonus, za, yb, gts_p, xp, gp, lnx_g[l], lnx_b[l], g512, woa_rows_bf, wob_bf, wout_bf,
                         tm_p, t_len)
        win = kv32.reshape(nbp, t_len, 2, KV_HEADS, HEAD_DIM)[:, t_len - WINDOW:]
        outs["wkv_p"].append(wkv)
        outs["shift_p"].append(ush.reshape(nbp, t_len, SHIFT_W)[:, t_len - 1])
        outs["kw_p"].append(win[:, :, 0])
        outs["vw_p"].append(win[:, :, 1])

        ss, scs, gs = shift[nbp:], scale[nbp:], gate[nbp:]
        usht, zat, qkv, kv32, zb, gts_s = _in_proj(xs, norm_g[l], scs, ss, w_bf, wt_bf, q_gain, k_gain, g512, g128,
                                                   nbs, nbs, True, True)
        yt, wkv = _rwkv_sample(usht, state_shift[l], states_t, l, lw)
        yb, kwin, vwin = _attn_step(qkv, kv32, zb, cache_k_t, cache_v_t, l, bias_rows, bias0, sink_rows, g128,
                                    _pick((16, 8), nbs))
        xs = _merge(yt, zat, yb, gts_s, xs, gs, woa_bf, wob_bf, wout_bf, nbs, nbs, True)
        outs["wkv_s"].append(wkv)
        outs["shift_s"].append(_shift_major(usht[0].T, 1, inverse=True))
        outs["kw_s"].append(kwin.transpose(0, 3, 1, 2))
        outs["vw_s"].append(vwin.transpose(0, 3, 1, 2))

    st = lambda k: jnp.stack(outs[k])
    return (xp.reshape(nbp, t_len, D_MODEL), xs.reshape(nbs, 1, D_MODEL),
            st("wkv_p"), st("shift_p"), st("kw_p"), st("vw_p"),
            st("wkv_s"), st("shift_s"), st("kw_s"), st("vw_s"))
```
